```python
import jax, jax.numpy as jnp
from jax import lax
import numpy as np

D_MODEL = 4096
BATCH = 16
SEQ = 256
DEPTH = 1
DEC_BATCH = 8
DEC_SEQ = 4096
PAST_LEN = 512

GRID_W = 64
F32 = jnp.float32
D_R = D_MODEL // 2
RWKV_HEAD = 64
H_R = D_R // RWKV_HEAD
W_LORA = 128
A_LORA = 128
G_LORA = 480
RWKV_SPLITS = (D_R, D_R, D_R, W_LORA, W_LORA, A_LORA, A_LORA, G_LORA)
RW_COLS = 3 * D_R + 2 * W_LORA + 2 * A_LORA + G_LORA
D_G = D_MODEL // 2
HGRN_EXPAND = 128
H_G = D_G // HGRN_EXPAND
HGRN_DV = D_G // H_G
HGRN_CHUNK = 64
HG_COLS = 5 * D_G
N_IN = RW_COLS + HG_COLS + 2 * D_MODEL
D_FF = 11008
FFN_HALF = 0.5
N_MOD = 9
RMS_EPS = 1e-6
GN_EPS = 64e-5

kernel_name = 'rwkv7_hgrn2_macaron_diffusion_step'


def _split(z, sizes):
    idx = []
    acc = 0
    for s in sizes[:-1]:
        acc += s
        idx.append(acc)
    return jnp.split(z, idx, axis=-1)


def rms_norm(x, gain):
    xf = x.astype(F32)
    y = xf * lax.rsqrt(jnp.mean(xf * xf, axis=-1, keepdims=True) + RMS_EPS)
    return (y * gain.astype(F32)).astype(x.dtype)


def swiglu(h, w_in, w_out):
    a, b = jnp.split(h @ w_in, 2, axis=-1)
    return (jax.nn.silu(a) * b) @ w_out


def centred_shift(z, mu_prev, mu_next):
    z_prev = jnp.pad(z[:, :-1], ((0, 0), (1, 0), (0, 0)))
    z_next = jnp.pad(z[:, 1:], ((0, 0), (0, 1), (0, 0)))
    return z + mu_prev * (z_prev - z) + mu_next * (z_next - z)


def rwkv7_scan(r, w, k, v, kk, a, s0, reverse):
    def step(s, inp):
        r_t, w_t, k_t, v_t, kk_t, a_t = inp
        sa = jnp.einsum('bhvk,bhk->bhv', s, -kk_t)
        s = (s * w_t[:, :, None, :]
             + sa[..., None] * (kk_t * a_t)[:, :, None, :]
             + v_t[..., None] * k_t[:, :, None, :])
        o = jnp.einsum('bhvk,bhk->bhv', s, r_t)
        return s, o
    xs = tuple(jnp.moveaxis(t, 1, 0) for t in (r, w, k, v, kk, a))
    s_fin, o = lax.scan(step, s0, xs, reverse=reverse)
    return jnp.moveaxis(o, 0, 1), s_fin


def rwkv7_branch(z, s0, mu_prev, mu_next, w0, w2, a0, a2, g2, k_k, k_a, r_k, ln_w, ln_b):
    B, T, _ = z.shape
    z = centred_shift(z.astype(F32), mu_prev.astype(F32), mu_next.astype(F32))
    r, k, v, wd_f, wd_b, ad_f, ad_b, gd = _split(z, RWKV_SPLITS)
    w0, w2, a0, a2, g2 = (t.astype(F32) for t in (w0, w2, a0, a2, g2))
    k_k, k_a, r_k = k_k.astype(F32), k_a.astype(F32), r_k.astype(F32)
    hd = lambda t: t.reshape(B, T, H_R, RWKV_HEAD)
    kk = hd(k * k_k)
    kk = kk / jnp.maximum(jnp.sqrt(jnp.sum(kk * kk, axis=-1, keepdims=True)), 1e-12)
    g = jax.nn.sigmoid(gd) @ g2
    s0 = s0.astype(F32)
    outs, bonuses, states = [], [], []
    for d, (wd, ad, rev) in enumerate(((wd_f, ad_f, False), (wd_b, ad_b, True))):
        w_log = -jax.nn.softplus(-(w0[d] + jnp.tanh(wd) @ w2[d])) - 0.5
        decay = jnp.exp(-jnp.exp(w_log))
        a = jax.nn.sigmoid(a0[d] + ad @ a2[d])
        k_d = k * (1.0 + (a - 1.0) * k_a)
        o, s = rwkv7_scan(hd(r), hd(decay), hd(k_d), hd(v), kk, hd(a), s0[:, d], rev)
        outs.append(o)
        bonuses.append(jnp.sum(hd(r) * hd(k_d) * r_k, axis=-1, keepdims=True) * hd(v))
        states.append(s)
    o = outs[0] + outs[1]
    mean = jnp.mean(o, axis=-1, keepdims=True)
    var = jnp.mean(jnp.square(o - mean), axis=-1, keepdims=True)
    o = ((o - mean) * lax.rsqrt(var + GN_EPS) * ln_w.astype(F32).reshape(H_R, RWKV_HEAD)
         + ln_b.astype(F32).reshape(H_R, RWKV_HEAD))
    y = (o + bonuses[0] + bonuses[1]).reshape(B, T, D_R) * g
    return y, jnp.stack(states, axis=1)


def hgrn2_chunk_scan(q, k, g, v, s0):
    B, T, H, _ = q.shape
    n = T // HGRN_CHUNK
    to_chunks = lambda t: jnp.moveaxis(t.reshape(B, n, HGRN_CHUNK, H, t.shape[-1]), 1, 0)
    causal = jnp.tril(jnp.ones((HGRN_CHUNK, HGRN_CHUNK), dtype=bool))

    def step(s, inp):
        q_c, k_c, g_c, v_c = inp
        b = jnp.cumsum(g_c, axis=1)
        diff = b[:, :, None] - b[:, None, :]
        decay = jnp.exp(jnp.where(causal[None, :, :, None, None], diff, -jnp.inf))
        attn = jnp.einsum('bthe,bshe,btshe->bhts', q_c, k_c, decay)
        o = (jnp.einsum('bthe,bhed->bthd', q_c * jnp.exp(b), s)
             + jnp.einsum('bhts,bshd->bthd', attn, v_c))
        b_last = b[:, -1]
        s = (jnp.exp(b_last)[..., None] * s
             + jnp.einsum('bshe,bshd->bhed', k_c * jnp.exp(b_last[:, None] - b), v_c))
        return s, o

    s_fin, o = lax.scan(step, s0, tuple(to_chunks(t) for t in (q, k, g, v)))
    return jnp.moveaxis(o, 0, 1).reshape(B, T, H, v.shape[-1]), s_fin


def hgrn2_branch(z, s0, lb, gain, is_latent):
    B, T, _ = z.shape
    q, f_f, f_b, i, gate = _split(z.astype(F32), (D_G,) * 5)
    if is_latent:
        rows = T // GRID_W
        to_col = lambda t: t.reshape(B, rows, GRID_W, t.shape[-1]).transpose(0, 2, 1, 3).reshape(B, T, t.shape[-1])
        q, f_f, f_b, i = to_col(q), to_col(f_f), to_col(f_b), to_col(i)
    q = jax.nn.silu(q).reshape(B, T, H_G, HGRN_EXPAND)
    v = i.reshape(B, T, H_G, HGRN_DV)
    lb_h = lb.astype(F32).reshape(H_G, HGRN_EXPAND)
    s0 = s0.astype(F32)
    outs, states = [], []
    for d, (f_pre, rev) in enumerate(((f_f, False), (f_b, True))):
        f = lb_h + (1.0 - lb_h) * jax.nn.sigmoid(f_pre.reshape(B, T, H_G, HGRN_EXPAND))
        args = (q, 1.0 - f, jnp.log(f), v)
        if rev:
            args = tuple(jnp.flip(t, axis=1) for t in args)
        o, s = hgrn2_chunk_scan(*args, s0[:, d])
        if rev:
            o = jnp.flip(o, axis=1)
        outs.append(o)
        states.append(s)
    o = outs[0] + outs[1]
    if is_latent:
        rows = T // GRID_W
        o = o.reshape(B, GRID_W, rows, H_G, HGRN_DV).transpose(0, 2, 1, 3, 4).reshape(B, T, H_G, HGRN_DV)
    o = o * lax.rsqrt(jnp.mean(o * o, axis=-1, keepdims=True) + RMS_EPS) * gain.astype(F32)
    y = o.reshape(B, T, D_G) * jax.nn.silu(gate)
    return y, jnp.stack(states, axis=1)


def mixer(h, s_r0, s_h0, P, l, lb, is_latent):
    z = h @ P['w_in'][l]
    z_r, z_h, z_g = _split(z, (RW_COLS, HG_COLS, 2 * D_MODEL))
    y_r, s_r = rwkv7_branch(z_r, s_r0, P['rwkv_mu_prev'][l], P['rwkv_mu_next'][l],
                            P['rwkv_w0'][l], P['rwkv_w2'][l], P['rwkv_a0'][l], P['rwkv_a2'][l],
                            P['rwkv_g2'][l], P['rwkv_k_k'][l], P['rwkv_k_a'][l], P['rwkv_r_k'][l],
                            P['rwkv_ln_w'][l], P['rwkv_ln_b'][l])
    y_h, s_h = hgrn2_branch(z_h, s_h0, lb, P['hgrn_norm'][l], is_latent)
    gate_r, gate_h = jnp.split(jax.nn.sigmoid(z_g.astype(F32)).astype(h.dtype), 2, axis=-1)
    merged = (gate_r * (y_r.astype(h.dtype) @ P['w_branch_rwkv'][l])
              + gate_h * (y_h.astype(h.dtype) @ P['w_branch_hgrn'][l]))
    return merged @ P['w_out'][l], s_r, s_h


def layer(x, cond, s_r0, s_h0, P, l, lb, is_latent):
    mod = jax.nn.silu(cond.astype(F32)) @ P['w_mod'][l] + P['b_mod'][l]
    sh1, sc1, g1, sh2, sc2, g2, sh3, sc3, g3 = jnp.split(mod[:, None, :].astype(x.dtype), N_MOD, axis=-1)
    h = rms_norm(x, P['norm_ffn1'][l]) * (1.0 + sc1) + sh1
    x = x + g1 * (FFN_HALF * swiglu(h, P['ffn1_w_in'][l], P['ffn1_w_out'][l]))
    h = rms_norm(x, P['norm_mix'][l]) * (1.0 + sc2) + sh2
    m, s_r, s_h = mixer(h, s_r0, s_h0, P, l, lb, is_latent)
    x = x + g2 * m
    h = rms_norm(x, P['norm_ffn2'][l]) * (1.0 + sc3) + sh3
    x = x + g3 * (FFN_HALF * swiglu(h, P['ffn2_w_in'][l], P['ffn2_w_out'][l]))
    return x, s_r, s_h


def setup_inputs(seed: int = 0) -> dict:
    key = jax.random.key(seed)
    ks = jax.random.split(key, 40)
    D = D_MODEL
    nrm = lambda k, shape, scale: jax.random.normal(k, shape, F32) * scale
    uni = lambda k, shape, lo, hi: jax.random.uniform(k, shape, F32, lo, hi)
    return {
        'x_prompt': nrm(ks[0], (BATCH, SEQ, D), 1.0),
        'x_sample': nrm(ks[1], (DEC_BATCH, DEC_SEQ, D), 1.0),
        'state_rwkv': nrm(ks[2], (DEC_BATCH, DEPTH, 2, H_R, RWKV_HEAD, RWKV_HEAD), 0.5),
        'state_hgrn': nrm(ks[3], (DEC_BATCH, DEPTH, 2, H_G, HGRN_EXPAND, HGRN_DV), 0.5),
        'c': nrm(ks[4], (DEC_BATCH, D), 1.0),
        'c_ctx': nrm(ks[5], (D,), 1.0),
        'w_mod': nrm(ks[6], (DEPTH, D, N_MOD * D), 0.5 * D ** -0.5),
        'b_mod': nrm(ks[7], (DEPTH, N_MOD * D), 0.01),
        'norm_ffn1': 1.0 + nrm(ks[8], (DEPTH, D), 0.02),
        'norm_mix': 1.0 + nrm(ks[9], (DEPTH, D), 0.02),
        'norm_ffn2': 1.0 + nrm(ks[10], (DEPTH, D), 0.02),
        'ffn1_w_in': nrm(ks[11], (DEPTH, D, 2 * D_FF), D ** -0.5),
        'ffn1_w_out': nrm(ks[12], (DEPTH, D_FF, D), D_FF ** -0.5),
        'ffn2_w_in': nrm(ks[13], (DEPTH, D, 2 * D_FF), D ** -0.5),
        'ffn2_w_out': nrm(ks[14], (DEPTH, D_FF, D), D_FF ** -0.5),
        'w_in': nrm(ks[15], (DEPTH, D, N_IN), D ** -0.5),
        'rwkv_mu_prev': uni(ks[16], (DEPTH, RW_COLS), 0.0, 0.5),
        'rwkv_mu_next': uni(ks[17], (DEPTH, RW_COLS), 0.0, 0.5),
        'rwkv_w0': uni(ks[18], (DEPTH, 2, D_R), -6.5, -1.5),
        'rwkv_w2': nrm(ks[19], (DEPTH, 2, W_LORA, D_R), 0.3 * W_LORA ** -0.5),
        'rwkv_a0': nrm(ks[20], (DEPTH, 2, D_R), 0.1),
        'rwkv_a2': nrm(ks[21], (DEPTH, 2, A_LORA, D_R), 0.5 * A_LORA ** -0.5),
        'rwkv_g2': nrm(ks[22], (DEPTH, G_LORA, D_R), G_LORA ** -0.5),
        'rwkv_k_k': 0.85 + nrm(ks[23], (DEPTH, D_R), 0.02),
        'rwkv_k_a': 1.0 + nrm(ks[24], (DEPTH, D_R), 0.02),
        'rwkv_r_k': nrm(ks[25], (DEPTH, H_R, RWKV_HEAD), 0.1),
        'rwkv_ln_w': 1.0 + nrm(ks[26], (DEPTH, D_R), 0.02),
        'rwkv_ln_b': nrm(ks[27], (DEPTH, D_R), 0.01),
        'hgrn_lb_logits': nrm(ks[28], (DEPTH + 1, D_G), 0.5),
        'hgrn_norm': 1.0 + nrm(ks[29], (DEPTH, HGRN_DV), 0.02),
        'w_branch_rwkv': nrm(ks[30], (DEPTH, D_R, D), D_R ** -0.5),
        'w_branch_hgrn': nrm(ks[31], (DEPTH, D_G, D), D_G ** -0.5),
        'w_out': nrm(ks[32], (DEPTH, D, D), D ** -0.5),
        'norm_final': 1.0 + nrm(ks[33], (D,), 0.02),
    }


def reference(x_prompt, x_sample, state_rwkv, state_hgrn, c, c_ctx, w_mod, b_mod,
              norm_ffn1, norm_mix, norm_ffn2, ffn1_w_in, ffn1_w_out, ffn2_w_in, ffn2_w_out,
              w_in, rwkv_mu_prev, rwkv_mu_next, rwkv_w0, rwkv_w2, rwkv_a0, rwkv_a2, rwkv_g2,
              rwkv_k_k, rwkv_k_a, rwkv_r_k, rwkv_ln_w, rwkv_ln_b, hgrn_lb_logits, hgrn_norm,
              w_branch_rwkv, w_branch_hgrn, w_out, norm_final):
    P = {
        'w_mod': w_mod, 'b_mod': b_mod, 'norm_ffn1': norm_ffn1, 'norm_mix': norm_mix,
        'norm_ffn2': norm_ffn2, 'ffn1_w_in': ffn1_w_in, 'ffn1_w_out': ffn1_w_out,
        'ffn2_w_in': ffn2_w_in, 'ffn2_w_out': ffn2_w_out, 'w_in': w_in,
        'rwkv_mu_prev': rwkv_mu_prev, 'rwkv_mu_next': rwkv_mu_next, 'rwkv_w0': rwkv_w0,
        'rwkv_w2': rwkv_w2, 'rwkv_a0': rwkv_a0, 'rwkv_a2': rwkv_a2, 'rwkv_g2': rwkv_g2,
        'rwkv_k_k': rwkv_k_k, 'rwkv_k_a': rwkv_k_a, 'rwkv_r_k': rwkv_r_k,
        'rwkv_ln_w': rwkv_ln_w, 'rwkv_ln_b': rwkv_ln_b, 'hgrn_norm': hgrn_norm,
        'w_branch_rwkv': w_branch_rwkv, 'w_branch_hgrn': w_branch_hgrn, 'w_out': w_out,
    }
    lb_all = jnp.cumsum(jax.nn.softmax(hgrn_lb_logits.astype(F32), axis=0), axis=0)

    nb = x_prompt.shape[0]
    zero_r = jnp.zeros((nb, 2, H_R, RWKV_HEAD, RWKV_HEAD), F32)
    zero_h = jnp.zeros((nb, 2, H_G, HGRN_EXPAND, HGRN_DV), F32)
    x = x_prompt
    new_r, new_h = [], []
    for l in range(DEPTH):
        x, s_r, s_h = layer(x, c_ctx[None, :], zero_r, zero_h, P, l, lb_all[l], False)
        new_r.append(s_r)
        new_h.append(s_h)
    y_prompt = rms_norm(x, norm_final)
    new_state_rwkv = jnp.stack(new_r, axis=1)
    new_state_hgrn = jnp.stack(new_h, axis=1)

    x = x_sample
    for l in range(DEPTH):
        x, _, _ = layer(x, c, state_rwkv[:, l], state_hgrn[:, l], P, l, lb_all[l], True)
    y_sample = rms_norm(x, norm_final)
    return (y_prompt, y_sample, new_state_rwkv, new_state_hgrn)
```

```python
import functools

import jax
import jax.numpy as jnp
from jax import lax
from jax.experimental import pallas as pl
from jax.experimental.pallas import tpu as pltpu

F32 = jnp.float32
BF16 = jnp.bfloat16

CHUNK = 64
GRID_W = 64
RWKV_HEAD = 64
HGRN_EXPAND = 128
N_MOD = 9
RMS_EPS = 1e-6
GN_EPS = 64e-5
FFN_HALF = 0.5
LANES = 128
MOD_ROWS = 16
VMEM_LIMIT_BYTES = 56 * 1024 * 1024


def _cparams(*sem):
    return pltpu.CompilerParams(dimension_semantics=sem, vmem_limit_bytes=VMEM_LIMIT_BYTES)


def _dot(a, b):
    return jnp.dot(a.astype(BF16), b.astype(BF16), preferred_element_type=F32)


def _dot_nt(a, b):
    return lax.dot_general(a.astype(BF16), b.astype(BF16), (((1,), (1,)), ((), ())),
                           preferred_element_type=F32)


def _dot_tn(a, b):
    return jnp.dot(a.T.astype(BF16), b.astype(BF16), preferred_element_type=F32)


def _split2(x):
    hi = x.astype(BF16)
    return hi, (x - hi.astype(F32)).astype(BF16)


def _dot_split(a, b):
    a_h, a_l = _split2(a)
    b_h, b_l = _split2(b)
    return jnp.dot(jnp.concatenate([a_h, a_l, a_h], axis=1),
                   jnp.concatenate([b_h, b_h, b_l], axis=0), preferred_element_type=F32)


def _split3(x):
    x1 = x.astype(BF16)
    d1 = x - x1.astype(F32)
    x2 = d1.astype(BF16)
    x3 = (d1 - x2.astype(F32)).astype(BF16)
    return x1, x2, x3


def _exact_rowmix(mat, x):
    x1, x2, x3 = _split3(x)
    y = jnp.dot(mat, jnp.concatenate([x1, x2, x3], axis=1), preferred_element_type=F32)
    n = x.shape[1]
    return y[:, :n] + y[:, n:2 * n] + y[:, 2 * n:]


def _group_sum(x, bd_ones):
    x1 = x.astype(BF16)
    x2 = (x - x1.astype(F32)).astype(BF16)
    return (jnp.dot(x1, bd_ones, preferred_element_type=F32)
            + jnp.dot(x2, bd_ones, preferred_element_type=F32))


def _mod_kernel(c_ref, w_ref, b_ref, o_ref):
    c = c_ref[...]
    s = c * jax.nn.sigmoid(c)
    o_ref[...] = _dot(s, w_ref[...]) + b_ref[...]


def _modulation(cond, w_mod, b_mod, tn):
    d, n = w_mod.shape
    return pl.pallas_call(
        _mod_kernel,
        out_shape=jax.ShapeDtypeStruct((MOD_ROWS, n), F32),
        grid=(n // tn,),
        in_specs=[pl.BlockSpec((MOD_ROWS, d), lambda j: (0, 0)),
                  pl.BlockSpec((d, tn), lambda j: (0, j)),
                  pl.BlockSpec((1, tn), lambda j: (0, j))],
        out_specs=pl.BlockSpec((MOD_ROWS, tn), lambda j: (0, j)),
        compiler_params=_cparams("arbitrary"),
        name="modulation",
    )(cond, w_mod, b_mod.reshape(1, n))


def _resnorm_kernel(*refs, has_delta, res_scale, gate_idx, mod_idx, emit_x):
    it = iter(refs)
    x_ref = next(it)
    delta_ref = next(it) if has_delta else None
    mod_ref = next(it) if (has_delta or mod_idx is not None) else None
    gain_ref = next(it)
    xo_ref = next(it) if emit_x else None
    h_ref = next(it)
    x = x_ref[...]
    if has_delta:
        g = mod_ref[gate_idx:gate_idx + 1, :]
        x = x + g * (res_scale * delta_ref[...])
    if emit_x:
        xo_ref[...] = x
    y = x * lax.rsqrt(jnp.mean(x * x, axis=-1, keepdims=True) + RMS_EPS)
    y = y * gain_ref[...]
    if mod_idx is not None:
        sh = mod_ref[mod_idx:mod_idx + 1, :]
        sc = mod_ref[mod_idx + 1:mod_idx + 2, :]
        y = y * (1.0 + sc) + sh
    h_ref[...] = y.astype(h_ref.dtype)


def _resnorm(x, delta, mod3, gain, dims, *, res_scale, gate_idx, mod_idx, emit_x, out_dtype,
             tm, row_off=0, rows=None, x_row_off=None):
    mc, tc, tl, nlb = dims
    d = x.shape[1]
    m_all = rows if rows is not None else x.shape[0]
    ro = row_off // tm
    xro = ro if x_row_off is None else x_row_off // tm
    n_ctx_tiles = mc // tm
    tiles_per_lat = tl // tm

    def mod_row(i):
        gi = i + ro
        return jnp.where(gi < n_ctx_tiles, nlb, (gi - n_ctx_tiles) // tiles_per_lat)

    has_delta = delta is not None
    in_specs = [pl.BlockSpec((tm, d), lambda i: (i + xro, 0))]
    args = [x]
    if has_delta:
        in_specs.append(pl.BlockSpec((tm, d), lambda i: (i + ro, 0)))
        args.append(delta)
    if has_delta or mod_idx is not None:
        in_specs.append(pl.BlockSpec((None, N_MOD, d), lambda i: (mod_row(i), 0, 0)))
        args.append(mod3)
    in_specs.append(pl.BlockSpec((1, d), lambda i: (0, 0)))
    args.append(gain.reshape(1, d))
    out_shape, out_specs = [], []
    if emit_x:
        out_shape.append(jax.ShapeDtypeStruct((m_all, d), F32))
        out_specs.append(pl.BlockSpec((tm, d), lambda i: (i, 0)))
    out_shape.append(jax.ShapeDtypeStruct((m_all, d), out_dtype))
    out_specs.append(pl.BlockSpec((tm, d), lambda i: (i, 0)))
    kern = functools.partial(_resnorm_kernel, has_delta=has_delta, res_scale=res_scale,
                             gate_idx=gate_idx, mod_idx=mod_idx, emit_x=emit_x)
    return pl.pallas_call(
        kern, out_shape=out_shape, grid=(m_all // tm,), in_specs=in_specs, out_specs=out_specs,
        compiler_params=_cparams("arbitrary"), name="resnorm",
    )(*args)


def _permnorm_kernel(x_ref, mod_ref, gain_ref, h_ref, *, mod_idx):
    x = x_ref[...]
    y = x * lax.rsqrt(jnp.mean(x * x, axis=-1, keepdims=True) + RMS_EPS)
    y = y * gain_ref[...]
    sh = mod_ref[mod_idx:mod_idx + 1, :]
    sc = mod_ref[mod_idx + 1:mod_idx + 2, :]
    h_ref[...] = (y * (1.0 + sc) + sh).astype(h_ref.dtype)


def _permnorm(x, mod3, gain, dims, *, mod_idx):
    mc, tc, tl, nlb = dims
    m, d = x.shape
    rows = tl // GRID_W
    assert mc % tl == 0 and rows % 16 == 0
    seq_off = mc // tl
    x3 = x.reshape(m // tl, rows, GRID_W * d)
    return pl.pallas_call(
        functools.partial(_permnorm_kernel, mod_idx=mod_idx),
        out_shape=jax.ShapeDtypeStruct((nlb * tl, d), BF16),
        grid=(nlb, GRID_W),
        in_specs=[pl.BlockSpec((None, rows, d), lambda b, w: (b + seq_off, 0, w)),
                  pl.BlockSpec((None, N_MOD, d), lambda b, w: (b, 0, 0)),
                  pl.BlockSpec((1, d), lambda b, w: (0, 0))],
        out_specs=pl.BlockSpec((rows, d), lambda b, w: (b * GRID_W + w, 0)),
        compiler_params=_cparams("arbitrary", "arbitrary"), name="permnorm",
    )(x3, mod3, gain.reshape(1, d))


def _mm_kernel(a_ref, w_ref, o_ref):
    o_ref[...] = jnp.dot(a_ref[...], w_ref[...], preferred_element_type=F32).astype(o_ref.dtype)


def _matmul(a, w, *, out_dtype, tm, tn, row_off=0, rows=None, name="matmul"):
    k = a.shape[1]
    n = w.shape[1]
    rows = a.shape[0] if rows is None else rows
    ro = row_off // tm
    return pl.pallas_call(
        _mm_kernel,
        out_shape=jax.ShapeDtypeStruct((rows, n), out_dtype),
        grid=(rows // tm, n // tn),
        in_specs=[pl.BlockSpec((tm, k), lambda i, j: (i + ro, 0)),
                  pl.BlockSpec((k, tn), lambda i, j: (0, j))],
        out_specs=pl.BlockSpec((tm, tn), lambda i, j: (i, j)),
        compiler_params=_cparams("arbitrary", "arbitrary"), name=name,
    )(a, w)


def _ffn_kernel(h_ref, wa_ref, wb_ref, wo_ref, o_ref):
    j = pl.program_id(1)
    h = h_ref[...]
    a = jnp.dot(h, wa_ref[...], preferred_element_type=F32)
    b = jnp.dot(h, wb_ref[...], preferred_element_type=F32)
    u = (a * jax.nn.sigmoid(a) * b).astype(BF16)
    part = jnp.dot(u, wo_ref[...], preferred_element_type=F32)

    @pl.when(j == 0)
    def _():
        o_ref[...] = part

    @pl.when(j > 0)
    def _():
        o_ref[...] += part


def _ffn(h, w_in, w_out, *, tm, tf):
    m, d = h.shape
    dff = w_out.shape[0]
    nf = dff // tf
    return pl.pallas_call(
        _ffn_kernel,
        out_shape=jax.ShapeDtypeStruct((m, d), F32),
        grid=(m // tm, nf),
        in_specs=[pl.BlockSpec((tm, d), lambda i, j: (i, 0)),
                  pl.BlockSpec((d, tf), lambda i, j: (0, j)),
                  pl.BlockSpec((d, tf), lambda i, j: (0, j + nf)),
                  pl.BlockSpec((tf, d), lambda i, j: (j, 0))],
        out_specs=pl.BlockSpec((tm, d), lambda i, j: (i, 0)),
        compiler_params=_cparams("arbitrary", "arbitrary"), name="ffn",
    )(h, w_in, w_in, w_out)


def _merge_kernel(yr_ref, yh_ref, wr_ref, wh_ref, gr_ref, gh_ref, o_ref):
    pr = jnp.dot(yr_ref[...], wr_ref[...], preferred_element_type=F32)
    ph = jnp.dot(yh_ref[...], wh_ref[...], preferred_element_type=F32)
    o_ref[...] = (jax.nn.sigmoid(gr_ref[...]) * pr
                  + jax.nn.sigmoid(gh_ref[...]) * ph).astype(o_ref.dtype)


def _merge(y_r, y_h, w_br, w_bh, z_rest, col_gr, col_gh, *, tm, tn):
    m, kr = y_r.shape
    kh = y_h.shape[1]
    d = w_br.shape[1]
    cgr, cgh = col_gr // tn, col_gh // tn
    return pl.pallas_call(
        _merge_kernel,
        out_shape=jax.ShapeDtypeStruct((m, d), BF16),
        grid=(m // tm, d // tn),
        in_specs=[pl.BlockSpec((tm, kr), lambda i, j: (i, 0)),
                  pl.BlockSpec((tm, kh), lambda i, j: (i, 0)),
                  pl.BlockSpec((kr, tn), lambda i, j: (0, j)),
                  pl.BlockSpec((kh, tn), lambda i, j: (0, j)),
                  pl.BlockSpec((tm, tn), lambda i, j: (i, j + cgr)),
                  pl.BlockSpec((tm, tn), lambda i, j: (i, j + cgh))],
        out_specs=pl.BlockSpec((tm, tn), lambda i, j: (i, j)),
        compiler_params=_cparams("arbitrary", "arbitrary"), name="merge",
    )(y_r, y_h, w_br, w_bh, z_rest, z_rest)


def _log2(n):
    assert n & (n - 1) == 0
    return n.bit_length() - 1


def _head_ones(n, group):
    sh = _log2(group)
    i = lax.broadcasted_iota(jnp.int32, (n, n), 0) >> sh
    j = lax.broadcasted_iota(jnp.int32, (n, n), 1) >> sh
    return jnp.where(i == j, 1.0, 0.0).astype(BF16)


def _rwkv_prep_kernel(z_ref, zp_ref, zn_ref, mup_ref, mun_ref, w0_ref, w2_ref, a0_ref, a2_ref,
                      g2_ref, kk_ref,
                      r_o, k_o, v_o, kkn_o, af_o, ab_o, lwf_o, lwb_o, g_o,
                      *, tm, mc, tc, tl, dr, wl, al, gl):
    row0 = pl.program_id(0) * tm
    in_ctx = row0 < mc
    pos = jnp.where(in_ctx, row0 % tc, (row0 - mc) % tl)
    seqlen = jnp.where(in_ctx, tc, tl)
    keep_prev = jnp.where(pos == 0, 0.0, 1.0)
    keep_next = jnp.where(pos + tm == seqlen, 0.0, 1.0)
    rows = lax.broadcasted_iota(jnp.int32, (tm, 1), 0)

    def shifted(c0, c1):
        z = z_ref[:, c0:c1]
        zp = jnp.where(rows == 0, keep_prev * zp_ref[7:8, c0:c1], pltpu.roll(z, 1, 0))
        zn = jnp.where(rows == tm - 1, keep_next * zn_ref[0:1, c0:c1], pltpu.roll(z, tm - 1, 0))
        return z + mup_ref[:, c0:c1] * (zp - z) + mun_ref[:, c0:c1] * (zn - z)

    bd = _head_ones(LANES, RWKV_HEAD)
    for j in range(dr // LANES):
        c0 = j * LANES
        r_o[:, c0:c0 + LANES] = shifted(c0, c0 + LANES)
        k = shifted(dr + c0, dr + c0 + LANES)
        k_o[:, c0:c0 + LANES] = k
        v_o[:, c0:c0 + LANES] = shifted(2 * dr + c0, 2 * dr + c0 + LANES)
        kk = k * kk_ref[:, c0:c0 + LANES]
        ss = _group_sum(kk * kk, bd)
        kkn_o[:, c0:c0 + LANES] = kk / jnp.maximum(jnp.sqrt(ss), 1e-12)

    c = 3 * dr
    for d, (lw_o, a_o) in enumerate(((lwf_o, af_o), (lwb_o, ab_o))):
        wd = shifted(c + d * wl, c + (d + 1) * wl)
        ad = shifted(c + 2 * wl + d * al, c + 2 * wl + (d + 1) * al)
        wpre = w0_ref[d:d + 1, :] + _dot(jnp.tanh(wd), w2_ref[d])
        nx = -wpre
        softplus = jnp.maximum(nx, 0.0) + jnp.log1p(jnp.exp(-jnp.abs(nx)))
        w_log = -softplus - 0.5
        lw_o[...] = -jnp.exp(w_log)
        a_o[...] = jax.nn.sigmoid(a0_ref[d:d + 1, :] + _dot(ad, a2_ref[d]))
    gd = shifted(c + 2 * wl + 2 * al, c + 2 * wl + 2 * al + gl)
    g_o[...] = _dot(jax.nn.sigmoid(gd), g2_ref[...])


def _rwkv_prep(z_rest, p, dims, *, tm, zr_cols, dr, wl, al, gl):
    mc, tc, tl, nlb = dims
    m = z_rest.shape[0]
    nblk8 = m // 8
    t8 = tm // 8
    full = lambda shape: pl.BlockSpec(shape, lambda i: (0,) * len(shape))
    outs = [jax.ShapeDtypeStruct((m, dr), F32)] * 9
    kern = functools.partial(_rwkv_prep_kernel, tm=tm, mc=mc, tc=tc, tl=tl, dr=dr, wl=wl, al=al,
                             gl=gl)
    return pl.pallas_call(
        kern, out_shape=outs, grid=(m // tm,),
        in_specs=[pl.BlockSpec((tm, zr_cols), lambda i: (i, 0)),
                  pl.BlockSpec((8, zr_cols), lambda i: (jnp.maximum(i * t8 - 1, 0), 0)),
                  pl.BlockSpec((8, zr_cols), lambda i: (jnp.minimum((i + 1) * t8, nblk8 - 1), 0)),
                  full((1, zr_cols)), full((1, zr_cols)),
                  full((2, dr)), full((2, wl, dr)), full((2, dr)), full((2, al, dr)),
                  full((gl, dr)), full((1, dr))],
        out_specs=[pl.BlockSpec((tm, dr), lambda i: (i, 0))] * 9,
        compiler_params=_cparams("arbitrary"), name="rwkv_prep",
    )(z_rest, z_rest, z_rest, p["mu_prev"], p["mu_next"], p["w0"], p["w2"], p["a0"], p["a2"],
      p["g2"], p["k_k"])


def _stack_heads(x, lo):
    return jnp.concatenate([jnp.where(lo, x, 0.0), jnp.where(lo, 0.0, x)], axis=0)


def _rwkv_chunk(r, k, v, kk, a, lw, k_a, s, rev):
    c = CHUNK
    n2 = 2 * c
    ti = lax.broadcasted_iota(jnp.int32, (c, c), 0)
    tj = lax.broadcasted_iota(jnp.int32, (c, c), 1)
    cum = jnp.where((tj >= ti) if rev else (tj <= ti), 1.0, 0.0).astype(BF16)
    bc = _exact_rowmix(cum, lw)
    bex = bc - lw
    btot = bc[0:1, :] if rev else bc[c - 1:c, :]
    kd = k * (1.0 + (a - 1.0) * k_a)
    bv = kk * a
    e_neg = jnp.exp(-bc)
    e_tot = jnp.exp(btot - bc)
    lo = lax.broadcasted_iota(jnp.int32, (c, LANES), 1) < RWKV_HEAD
    at = _stack_heads(-kk * jnp.exp(bex), lo).astype(BF16)
    rt_f = _stack_heads(r * jnp.exp(bc), lo)
    rt = rt_f.astype(BF16)
    kh = _stack_heads(kd * e_neg, lo).astype(BF16)
    bh = _stack_heads(bv * e_neg, lo).astype(BF16)
    vb = _stack_heads(v, lo).astype(BF16)
    kc = _stack_heads(kd * e_tot, lo).astype(BF16)
    bcc = _stack_heads(bv * e_tot, lo).astype(BF16)

    ii = lax.broadcasted_iota(jnp.int32, (n2, n2), 0)
    jj = lax.broadcasted_iota(jnp.int32, (n2, n2), 1)
    im, jm = ii & (c - 1), jj & (c - 1)
    strict = (jm > im) if rev else (jm < im)
    incl = (jm >= im) if rev else (jm <= im)
    nmat = jnp.where(strict, _dot_nt(at, bh), 0.0)
    a_ka = jnp.where(strict, _dot_nt(at, kh), 0.0)
    a_rk = jnp.where(incl, _dot_nt(rt, kh), 0.0).astype(BF16)
    a_rb = jnp.where(incl, _dot_nt(rt, bh), 0.0).astype(BF16)
    tm_ = jnp.where(ii == jj, 1.0, 0.0) + nmat
    pw = nmat
    for _ in range(5):
        pw = _dot_split(pw, pw)
        tm_ = tm_ + _dot_split(tm_, pw)
    tmb = tm_.astype(BF16)
    ah = _dot(tmb, at)
    w = _dot(tmb, _dot(a_ka, vb))
    rr = rt_f + _dot(a_rb, ah)
    q = _dot(a_rk, vb) + _dot(a_rb, w)
    mm = _dot_tn(ah, bcc)
    nn = _dot_tn(vb.astype(F32), kc) + _dot_tn(w, bcc)
    o_bd = _dot_nt(rr, s) + q
    o = o_bd[:c, :] + o_bd[c:, :]
    s_new = s * jnp.exp(btot) + _dot(s, mm) + nn
    return o, s_new


def _rwkv_scan_kernel(*refs, has_init, emit_final, nchunk):
    it = iter(refs)
    fwd = [next(it) for _ in range(6)]
    bwd = [next(it) for _ in range(6)]
    ka_ref = next(it)
    s0_ref = next(it) if has_init else None
    of_ref = next(it)
    ob_ref = next(it)
    sfin_ref = next(it) if emit_final else None
    s_scr = next(it)
    ci = pl.program_id(2)

    @pl.when(ci == 0)
    def _():
        if has_init:
            s_scr[...] = s0_ref[...]
        else:
            s_scr[...] = jnp.zeros_like(s_scr)

    k_a = ka_ref[...]
    for d, (ins, o_ref) in enumerate(((fwd, of_ref), (bwd, ob_ref))):
        r, k, v, kk, a, lw = (x[...] for x in ins)
        o, s_new = _rwkv_chunk(r, k, v, kk, a, lw, k_a, s_scr[d], rev=(d == 1))
        o_ref[...] = o
        s_scr[d] = s_new

    if emit_final:
        @pl.when(ci == nchunk - 1)
        def _():
            sfin_ref[...] = s_scr[...]


def _rwkv_scan(pre, k_a, s0_bd, *, row_off, nb, t, emit_final):
    m, dr = pre["r"].shape
    npair = dr // LANES
    nchunk = t // CHUNK
    rb = row_off // CHUNK
    has_init = s0_bd is not None
    fmap = lambda b, h, c: (rb + b * nchunk + c, h)
    bmap = lambda b, h, c: (rb + b * nchunk + (nchunk - 1 - c), h)
    blk = (CHUNK, LANES)
    in_specs = ([pl.BlockSpec(blk, fmap)] * 6 + [pl.BlockSpec(blk, bmap)] * 6
                + [pl.BlockSpec((1, LANES), lambda b, h, c: (0, h))])
    args = [pre["r"], pre["k"], pre["v"], pre["kk"], pre["a_f"], pre["lw_f"],
            pre["r"], pre["k"], pre["v"], pre["kk"], pre["a_b"], pre["lw_b"], k_a]
    sblk = pl.BlockSpec((None, 2, None, LANES, LANES), lambda b, h, c: (b, 0, h, 0, 0))
    if has_init:
        in_specs.append(sblk)
        args.append(s0_bd)
    rows = nb * t
    orb = 0
    out_shape = [jax.ShapeDtypeStruct((rows, dr), F32)] * 2
    out_specs = [pl.BlockSpec(blk, lambda b, h, c: (orb + b * nchunk + c, h)),
                 pl.BlockSpec(blk, lambda b, h, c: (orb + b * nchunk + (nchunk - 1 - c), h))]
    if emit_final:
        out_shape.append(jax.ShapeDtypeStruct((nb, 2, npair, LANES, LANES), F32))
        out_specs.append(sblk)
    kern = functools.partial(_rwkv_scan_kernel, has_init=has_init, emit_final=emit_final,
                             nchunk=nchunk)
    return pl.pallas_call(
        kern, out_shape=out_shape, grid=(nb, npair, nchunk), in_specs=in_specs,
        out_specs=out_specs, scratch_shapes=[pltpu.VMEM((2, LANES, LANES), F32)],
        compiler_params=_cparams("arbitrary", "arbitrary", "arbitrary"), name="rwkv_scan",
    )(*args)


def _rwkv_post_kernel(of_ref, ob_ref, r_ref, k_ref, v_ref, af_ref, ab_ref, g_ref,
                      ka_ref, rk_ref, lnw_ref, lnb_ref, y_ref, *, dr):
    bd = _head_ones(LANES, RWKV_HEAD)
    inv = 1.0 / RWKV_HEAD
    for j in range(dr // LANES):
        sl = slice(j * LANES, (j + 1) * LANES)
        o = of_ref[:, sl] + ob_ref[:, sl]
        mean = _group_sum(o, bd) * inv
        oc = o - mean
        var = _group_sum(oc * oc, bd) * inv
        on = oc * lax.rsqrt(var + GN_EPS) * lnw_ref[:, sl] + lnb_ref[:, sl]
        r, k, v = r_ref[:, sl], k_ref[:, sl], v_ref[:, sl]
        k_a, r_k = ka_ref[:, sl], rk_ref[:, sl]
        bonus = 0.0
        for a_ref in (af_ref, ab_ref):
            kd = k * (1.0 + (a_ref[:, sl] - 1.0) * k_a)
            bonus = bonus + _group_sum(r * kd * r_k, bd) * v
        y_ref[:, sl] = ((on + bonus) * g_ref[:, sl]).astype(y_ref.dtype)


def _rwkv_post(o_f, o_b, pre, p, *, tm, row_off):
    rows, dr = o_f.shape
    ro = row_off // tm
    own = pl.BlockSpec((tm, dr), lambda i: (i, 0))
    glob = pl.BlockSpec((tm, dr), lambda i: (i + ro, 0))
    par = pl.BlockSpec((1, dr), lambda i: (0, 0))
    return pl.pallas_call(
        functools.partial(_rwkv_post_kernel, dr=dr),
        out_shape=jax.ShapeDtypeStruct((rows, dr), BF16),
        grid=(rows // tm,),
        in_specs=[own, own] + [glob] * 6 + [par] * 4,
        out_specs=own,
        compiler_params=_cparams("arbitrary"), name="rwkv_post",
    )(o_f, o_b, pre["r"], pre["k"], pre["v"], pre["a_f"], pre["a_b"], pre["g"],
      p["k_a"], p["r_k"], p["ln_w"], p["ln_b"])


HGRN_LEVELS = 6


def _hgrn_mix_matrix(rev):
    c = CHUNK
    t = lax.broadcasted_iota(jnp.int32, (c, c), 0)
    i = lax.broadcasted_iota(jnp.int32, (c, c), 1)
    mats = []
    cum = (i >= t) if rev else (i <= t)
    mats.append(jnp.where(cum, 1.0, 0.0))
    for lvl in range(HGRN_LEVELS):
        half = 1 << lvl
        base = (t >> (lvl + 1)) << (lvl + 1)
        if rev:
            mid = base + half
            ref_ = i >= mid
        else:
            mid = base + half - 1
            ref_ = i <= mid
        mats.append(jnp.where(cum, 1.0, 0.0) - jnp.where(ref_, 1.0, 0.0))
    return jnp.concatenate(mats, axis=0).astype(BF16)


def _hgrn_chunk(q_pre, f_pre, v, lb, st, rev):
    c = CHUNK
    q = q_pre * jax.nn.sigmoid(q_pre)
    f = lb + (1.0 - lb) * jax.nn.sigmoid(f_pre)
    kf = 1.0 - f
    g = jnp.log(f)
    ex = _exact_rowmix(_hgrn_mix_matrix(rev), g)
    b = ex[:c, :]
    btot = b[0:1, :] if rev else b[c - 1:c, :]
    ti = lax.broadcasted_iota(jnp.int32, (c, c), 0)
    tj = lax.broadcasted_iota(jnp.int32, (c, c), 1)
    attn = jnp.where(ti == tj, _dot_nt(q, kf), 0.0)
    for lvl in range(HGRN_LEVELS):
        dq = ex[(lvl + 1) * c:(lvl + 2) * c, :]
        qs = q * jnp.exp(jnp.minimum(dq, 0.0))
        ks = kf * jnp.exp(jnp.minimum(-dq, 0.0))
        late, early = (tj, ti) if rev else (ti, tj)
        sel = ((late >> lvl) == (early >> lvl) + 1) & (((early >> lvl) & 1) == 0)
        attn = attn + jnp.where(sel, _dot_nt(qs, ks), 0.0)
    o = _dot_nt(q * jnp.exp(b), st) + _dot(attn, v)
    st_new = st * jnp.exp(btot) + _dot_tn(v, kf * jnp.exp(btot - b))
    return o, st_new


def _hgrn_scan_kernel(*refs, has_init, emit_final, nchunk):
    it = iter(refs)
    qf_ref, ff_ref, vf_ref = next(it), next(it), next(it)
    qb_ref, fb_ref, vb_ref = next(it), next(it), next(it)
    lb_ref = next(it)
    s0_ref = next(it) if has_init else None
    of_ref, ob_ref = next(it), next(it)
    sfin_ref = next(it) if emit_final else None
    s_scr = next(it)
    ci = pl.program_id(2)

    @pl.when(ci == 0)
    def _():
        if has_init:
            s_scr[0] = s0_ref[0].T
            s_scr[1] = s0_ref[1].T
        else:
            s_scr[...] = jnp.zeros_like(s_scr)

    lb = lb_ref[...]
    for d, (ins, o_ref) in enumerate((((qf_ref, ff_ref, vf_ref), of_ref),
                                      ((qb_ref, fb_ref, vb_ref), ob_ref))):
        qp, fp, v = (x[...] for x in ins)
        o, s_new = _hgrn_chunk(qp, fp, v, lb, s_scr[d], rev=(d == 1))
        o_ref[...] = o
        s_scr[d] = s_new

    if emit_final:
        @pl.when(ci == nchunk - 1)
        def _():
            sfin_ref[0] = s_scr[0].T
            sfin_ref[1] = s_scr[1].T


def _hgrn_scan(zhs, lb, s0, *, nb, t, dg, emit_final):
    nh = dg // LANES
    nchunk = t // CHUNK
    has_init = s0 is not None
    blk = (CHUNK, LANES)

    def spec(group, rev):
        if rev:
            return pl.BlockSpec(blk, lambda b, h, c: (b * nchunk + nchunk - 1 - c, group * nh + h))
        return pl.BlockSpec(blk, lambda b, h, c: (b * nchunk + c, group * nh + h))

    in_specs = [spec(0, False), spec(1, False), spec(3, False),
                spec(0, True), spec(2, True), spec(3, True),
                pl.BlockSpec((1, LANES), lambda b, h, c: (0, h))]
    args = [zhs] * 6 + [lb]
    sblk = pl.BlockSpec((None, 2, None, LANES, LANES), lambda b, h, c: (b, 0, h, 0, 0))
    if has_init:
        in_specs.append(sblk)
        args.append(s0)
    out_shape = [jax.ShapeDtypeStruct((nb * t, dg), F32)] * 2
    out_specs = [pl.BlockSpec(blk, lambda b, h, c: (b * nchunk + c, h)),
                 pl.BlockSpec(blk, lambda b, h, c: (b * nchunk + nchunk - 1 - c, h))]
    if emit_final:
        out_shape.append(jax.ShapeDtypeStruct((nb, 2, nh, LANES, LANES), F32))
        out_specs.append(sblk)
    kern = functools.partial(_hgrn_scan_kernel, has_init=has_init, emit_final=emit_final,
                             nchunk=nchunk)
    return pl.pallas_call(
        kern, out_shape=out_shape, grid=(nb, nh, nchunk), in_specs=in_specs, out_specs=out_specs,
        scratch_shapes=[pltpu.VMEM((2, LANES, LANES), F32)],
        compiler_params=_cparams("arbitrary", "arbitrary", "arbitrary"), name="hgrn_scan",
    )(*args)


def _hgrn_post_kernel(of_ref, ob_ref, gate_ref, gain_ref, y_ref, *, dg):
    gain = gain_ref[...]
    for j in range(dg // LANES):
        sl = slice(j * LANES, (j + 1) * LANES)
        o = of_ref[:, sl] + ob_ref[:, sl]
        on = o * lax.rsqrt(jnp.mean(o * o, axis=-1, keepdims=True) + RMS_EPS) * gain
        gt = gate_ref[:, sl]
        y_ref[:, sl] = (on * (gt * jax.nn.sigmoid(gt))).astype(y_ref.dtype)


def _hgrn_post(o_f, o_b, z_rest, gain, *, gate_col, row_off, nb, t, permuted):
    rows_all, dg = o_f.shape
    kern = functools.partial(_hgrn_post_kernel, dg=dg)
    gcb = gate_col // dg
    par = pl.BlockSpec((1, LANES), lambda *_: (0, 0))
    if permuted:
        nrow = t // GRID_W
        tm = GRID_W
        o3f = o_f.reshape(nb, GRID_W, nrow * dg)
        o3b = o_b.reshape(nb, GRID_W, nrow * dg)
        ospec = pl.BlockSpec((None, GRID_W, dg), lambda b, r: (b, 0, r))
        rb = row_off // tm
        return pl.pallas_call(
            kern, out_shape=jax.ShapeDtypeStruct((rows_all, dg), BF16), grid=(nb, nrow),
            in_specs=[ospec, ospec,
                      pl.BlockSpec((tm, dg), lambda b, r: (rb + b * nrow + r, gcb)), par],
            out_specs=pl.BlockSpec((tm, dg), lambda b, r: (b * nrow + r, 0)),
            compiler_params=_cparams("arbitrary", "arbitrary"), name="hgrn_post_perm",
        )(o3f, o3b, z_rest, gain)
    tm = min(256, rows_all)
    rb = row_off // tm
    ospec = pl.BlockSpec((tm, dg), lambda i: (i, 0))
    return pl.pallas_call(
        kern, out_shape=jax.ShapeDtypeStruct((rows_all, dg), BF16), grid=(rows_all // tm,),
        in_specs=[ospec, ospec, pl.BlockSpec((tm, dg), lambda i: (rb + i, gcb)), par],
        out_specs=ospec,
        compiler_params=_cparams("arbitrary"), name="hgrn_post",
    )(o_f, o_b, z_rest, gain)


def _largest_divisor(n, candidates):
    for cand in candidates:
        if n % cand == 0:
            return cand
    raise ValueError(f"no tile in {candidates} divides {n}")


def _pad_to(x, axis, size):
    pad = size - x.shape[axis]
    if pad == 0:
        return x
    widths = [(0, 0)] * x.ndim
    widths[axis] = (0, pad)
    return jnp.pad(x, widths)


def _pair_state_to_bd(s):
    b, two, h, n, _ = s.shape
    s = s.reshape(b, two, h // 2, 2, n, n)
    z = jnp.zeros_like(s[:, :, :, 0])
    top = jnp.concatenate([s[:, :, :, 0], z], axis=-1)
    bot = jnp.concatenate([z, s[:, :, :, 1]], axis=-1)
    return jnp.concatenate([top, bot], axis=-2)


def _bd_to_pair_state(sbd):
    b, two, hp, n2, _ = sbd.shape
    n = n2 // 2
    s0 = sbd[:, :, :, :n, :n]
    s1 = sbd[:, :, :, n:, n:]
    return jnp.stack([s0, s1], axis=3).reshape(b, two, hp * 2, n, n)


def kernel(x_prompt, x_sample, state_rwkv, state_hgrn, c, c_ctx, w_mod, b_mod, norm_ffn1, norm_mix, norm_ffn2, ffn1_w_in, ffn1_w_out, ffn2_w_in, ffn2_w_out, w_in, rwkv_mu_prev, rwkv_mu_next, rwkv_w0, rwkv_w2, rwkv_a0, rwkv_a2, rwkv_g2, rwkv_k_k, rwkv_k_a, rwkv_r_k, rwkv_ln_w, rwkv_ln_b, hgrn_lb_logits, hgrn_norm, w_branch_rwkv, w_branch_hgrn, w_out, norm_final):
    nbc, tc, d = x_prompt.shape
    nbl, tl, _ = x_sample.shape
    depth = w_mod.shape[0]
    mc, ml = nbc * tc, nbl * tl
    m = mc + ml
    dims = (mc, tc, tl, nbl)
    dr = rwkv_w0.shape[-1]
    dg = w_branch_hgrn.shape[1]
    dff = ffn1_w_out.shape[1]
    wl, al, gl = rwkv_w2.shape[2], rwkv_a2.shape[2], rwkv_g2.shape[1]
    glp = -(-gl // LANES) * LANES
    rw_cols = 3 * dr + 2 * wl + 2 * al + gl
    zr_cols = 3 * dr + 2 * wl + 2 * al + glp
    assert nbl < MOD_ROWS and tl % GRID_W == 0 and tl // GRID_W == CHUNK and tc % CHUNK == 0

    tm_norm = min(256, tc)
    tm_mm = min(512, mc)
    tn_of = lambda n: _largest_divisor(n, (512, 256, 128))
    tf = _largest_divisor(dff, (256, 128))

    lb_all = jnp.cumsum(jax.nn.softmax(hgrn_lb_logits.astype(F32), axis=0), axis=0)
    cond = jnp.concatenate([c, c_ctx[None, :], jnp.zeros((MOD_ROWS - nbl - 1, d), F32)], axis=0)

    x = jnp.concatenate([x_prompt.reshape(mc, d), x_sample.reshape(ml, d)], axis=0)
    new_r, new_h = [], []
    for l in range(depth):
        mod3 = _modulation(cond, w_mod[l], b_mod[l], tn_of(N_MOD * d)).reshape(MOD_ROWS, N_MOD, d)

        h = _resnorm(x, None, mod3, norm_ffn1[l], dims, res_scale=None, gate_idx=None, mod_idx=0,
                     emit_x=False, out_dtype=BF16, tm=tm_norm)[0]
        delta = _ffn(h, ffn1_w_in[l].astype(BF16), ffn1_w_out[l].astype(BF16), tm=tm_mm, tf=tf)

        x, h = _resnorm(x, delta, mod3, norm_mix[l], dims, res_scale=FFN_HALF, gate_idx=2,
                        mod_idx=3, emit_x=True, out_dtype=BF16, tm=tm_norm)
        h_perm = _permnorm(x, mod3, norm_mix[l], dims, mod_idx=3)
        wi = w_in[l]
        hg0 = rw_cols
        w_r = _pad_to(wi[:, :rw_cols], 1, zr_cols).astype(BF16)
        w_hs = wi[:, hg0:hg0 + 4 * dg].astype(BF16)
        w_gates = wi[:, hg0 + 4 * dg:].astype(BF16)
        col_gate, col_gr, col_gh = 0, dg, dg + d
        z_r = _matmul(h, w_r, out_dtype=F32, tm=tm_mm, tn=tn_of(zr_cols), name="proj_rwkv")
        z_g = _matmul(h, w_gates, out_dtype=F32, tm=tm_mm, tn=tn_of(dg), name="proj_gates")
        zhs_c = _matmul(h, w_hs, out_dtype=F32, tm=tm_mm, tn=tn_of(dg), rows=mc,
                        name="proj_hs_ctx")
        zhs_l = _matmul(h_perm, w_hs, out_dtype=F32, tm=tm_mm, tn=tn_of(dg), name="proj_hs_lat")

        p = {
            "mu_prev": _pad_to(rwkv_mu_prev[l][None, :], 1, zr_cols),
            "mu_next": _pad_to(rwkv_mu_next[l][None, :], 1, zr_cols),
            "w0": rwkv_w0[l], "w2": rwkv_w2[l].astype(BF16),
            "a0": rwkv_a0[l], "a2": rwkv_a2[l].astype(BF16),
            "g2": _pad_to(rwkv_g2[l], 0, glp).astype(BF16),
            "k_k": rwkv_k_k[l][None, :], "k_a": rwkv_k_a[l][None, :],
            "r_k": rwkv_r_k[l].reshape(1, dr),
            "ln_w": rwkv_ln_w[l][None, :], "ln_b": rwkv_ln_b[l][None, :],
        }
        names = ("r", "k", "v", "kk", "a_f", "a_b", "lw_f", "lw_b", "g")
        pre = dict(zip(names, _rwkv_prep(z_r, p, dims, tm=min(128, tc), zr_cols=zr_cols, dr=dr,
                                         wl=wl, al=al, gl=glp)))
        of_c, ob_c, sr_c = _rwkv_scan(pre, p["k_a"], None, row_off=0, nb=nbc, t=tc,
                                      emit_final=True)
        of_l, ob_l = _rwkv_scan(pre, p["k_a"], _pair_state_to_bd(state_rwkv[:, l].astype(F32)),
                                row_off=mc, nb=nbl, t=tl, emit_final=False)
        yr_c = _rwkv_post(of_c, ob_c, pre, p, tm=min(256, tc), row_off=0)
        yr_l = _rwkv_post(of_l, ob_l, pre, p, tm=min(256, tc), row_off=mc)
        y_r = jnp.concatenate([yr_c, yr_l], axis=0)
        new_r.append(_bd_to_pair_state(sr_c))

        lb = lb_all[l][None, :]
        gain_h = hgrn_norm[l][None, :]
        hf_c, hb_c, sh_c = _hgrn_scan(zhs_c, lb, None, nb=nbc, t=tc, dg=dg, emit_final=True)
        hf_l, hb_l = _hgrn_scan(zhs_l, lb, state_hgrn[:, l].astype(F32), nb=nbl, t=tl, dg=dg,
                                emit_final=False)
        yh_c = _hgrn_post(hf_c, hb_c, z_g, gain_h, gate_col=col_gate, row_off=0, nb=nbc, t=tc,
                          permuted=False)
        yh_l = _hgrn_post(hf_l, hb_l, z_g, gain_h, gate_col=col_gate, row_off=mc, nb=nbl, t=tl,
                          permuted=True)
        y_h = jnp.concatenate([yh_c, yh_l], axis=0)
        new_h.append(sh_c)

        merged = _merge(y_r, y_h, w_branch_rwkv[l].astype(BF16), w_branch_hgrn[l].astype(BF16),
                        z_g, col_gr, col_gh, tm=tm_mm, tn=tn_of(dg))
        delta = _matmul(merged, w_out[l].astype(BF16), out_dtype=F32, tm=tm_mm, tn=tn_of(d),
                        name="mix_out")

        x, h = _resnorm(x, delta, mod3, norm_ffn2[l], dims, res_scale=1.0, gate_idx=5, mod_idx=6,
                        emit_x=True, out_dtype=BF16, tm=tm_norm)
        delta = _ffn(h, ffn2_w_in[l].astype(BF16), ffn2_w_out[l].astype(BF16), tm=tm_mm, tf=tf)
        last = l == depth - 1
        if not last:
            x = _resnorm(x, delta, mod3, norm_ffn2[l], dims, res_scale=FFN_HALF, gate_idx=8,
                         mod_idx=None, emit_x=True, out_dtype=F32, tm=tm_norm)[0]

    y_c = _resnorm(x, delta, mod3, norm_final, dims, res_scale=FFN_HALF, gate_idx=8, mod_idx=None,
                   emit_x=False, out_dtype=F32, tm=tm_norm, row_off=0, rows=mc)[0]
    y_l = _resnorm(x, delta, mod3, norm_final, dims, res_scale=FFN_HALF, gate_idx=8, mod_idx=None,
                   emit_x=False, out_dtype=F32, tm=tm_norm, row_off=mc, rows=ml)[0]
    return (y_c.reshape(nbc, tc, d), y_l.reshape(nbl, tl, d),
            jnp.stack(new_r, axis=1), jnp.stack(new_h, axis=1))
```

```python
import functools

import jax
import jax.numpy as jnp
from jax import lax
from jax.experimental import pallas as pl
from jax.experimental.pallas import tpu as pltpu

F32 = jnp.float32
BF16 = jnp.bfloat16

CHUNK = 64
GRID_W = 64
RWKV_HEAD = 64
HGRN_EXPAND = 128
N_MOD = 9
RMS_EPS = 1e-6
GN_EPS = 64e-5
FFN_HALF = 0.5
LANES = 128
MOD_ROWS = 16
VMEM_LIMIT_BYTES = 56 * 1024 * 1024


def _cparams(*sem):
    return pltpu.CompilerParams(dimension_semantics=sem, vmem_limit_bytes=VMEM_LIMIT_BYTES)


def _dot(a, b):
    return jnp.dot(a.astype(BF16), b.astype(BF16), preferred_element_type=F32)


def _dot_nt(a, b):
    return lax.dot_general(a.astype(BF16), b.astype(BF16), (((1,), (1,)), ((), ())),
                           preferred_element_type=F32)


def _dot_tn(a, b):
    return jnp.dot(a.T.astype(BF16), b.astype(BF16), preferred_element_type=F32)


def _split2(x):
    hi = x.astype(BF16)
    return hi, (x - hi.astype(F32)).astype(BF16)


def _dot_split(a, b):
    a_h, a_l = _split2(a)
    b_h, b_l = _split2(b)
    return jnp.dot(jnp.concatenate([a_h, a_l, a_h], axis=1),
                   jnp.concatenate([b_h, b_h, b_l], axis=0), preferred_element_type=F32)


def _split3(x):
    x1 = x.astype(BF16)
    d1 = x - x1.astype(F32)
    x2 = d1.astype(BF16)
    x3 = (d1 - x2.astype(F32)).astype(BF16)
    return x1, x2, x3


def _exact_rowmix(mat, x):
    x1, x2, x3 = _split3(x)
    y = jnp.dot(mat, jnp.concatenate([x1, x2, x3], axis=1), preferred_element_type=F32)
    n = x.shape[1]
    return y[:, :n] + y[:, n:2 * n] + y[:, 2 * n:]


def _group_sum(x, bd_ones):
    x1 = x.astype(BF16)
    x2 = (x - x1.astype(F32)).astype(BF16)
    return (jnp.dot(x1, bd_ones, preferred_element_type=F32)
            + jnp.dot(x2, bd_ones, preferred_element_type=F32))


def _lockstep(chains):
    results = [None] * len(chains)
    active = list(enumerate(chains))
    while active:
        still = []
        for idx, chain in active:
            try:
                next(chain)
                still.append((idx, chain))
            except StopIteration as done:
                results[idx] = done.value
        active = still
    return results


def _mod_kernel(c_ref, w_ref, b_ref, o_ref):
    c = c_ref[...]
    s = c * jax.nn.sigmoid(c)
    o_ref[...] = _dot(s, w_ref[...]) + b_ref[...]


def _modulation(cond, w_mod, b_mod, tn):
    d, n = w_mod.shape
    return pl.pallas_call(
        _mod_kernel,
        out_shape=jax.ShapeDtypeStruct((MOD_ROWS, n), F32),
        grid=(n // tn,),
        in_specs=[pl.BlockSpec((MOD_ROWS, d), lambda j: (0, 0)),
                  pl.BlockSpec((d, tn), lambda j: (0, j)),
                  pl.BlockSpec((1, tn), lambda j: (0, j))],
        out_specs=pl.BlockSpec((MOD_ROWS, tn), lambda j: (0, j)),
        compiler_params=_cparams("arbitrary"),
        name="modulation",
    )(cond, w_mod, b_mod.reshape(1, n))


def _resnorm_kernel(*refs, has_delta, res_scale, gate_idx, mod_idx, emit_x):
    it = iter(refs)
    x_ref = next(it)
    delta_ref = next(it) if has_delta else None
    mod_ref = next(it) if (has_delta or mod_idx is not None) else None
    gain_ref = next(it)
    xo_ref = next(it) if emit_x else None
    h_ref = next(it)
    x = x_ref[...]
    if has_delta:
        g = mod_ref[gate_idx:gate_idx + 1, :]
        x = x + g * (res_scale * delta_ref[...])
    if emit_x:
        xo_ref[...] = x
    y = x * lax.rsqrt(jnp.mean(x * x, axis=-1, keepdims=True) + RMS_EPS)
    y = y * gain_ref[...]
    if mod_idx is not None:
        sh = mod_ref[mod_idx:mod_idx + 1, :]
        sc = mod_ref[mod_idx + 1:mod_idx + 2, :]
        y = y * (1.0 + sc) + sh
    h_ref[...] = y.astype(h_ref.dtype)


def _resnorm(x, delta, mod3, gain, dims, *, res_scale, gate_idx, mod_idx, emit_x, out_dtype,
             tm, row_off=0, rows=None, x_row_off=None):
    mc, tc, tl, nlb = dims
    d = x.shape[1]
    m_all = rows if rows is not None else x.shape[0]
    ro = row_off // tm
    xro = ro if x_row_off is None else x_row_off // tm
    n_ctx_tiles = mc // tm
    tiles_per_lat = tl // tm

    def mod_row(i):
        gi = i + ro
        return jnp.where(gi < n_ctx_tiles, nlb, (gi - n_ctx_tiles) // tiles_per_lat)

    has_delta = delta is not None
    in_specs = [pl.BlockSpec((tm, d), lambda i: (i + xro, 0))]
    args = [x]
    if has_delta:
        in_specs.append(pl.BlockSpec((tm, d), lambda i: (i + ro, 0)))
        args.append(delta)
    if has_delta or mod_idx is not None:
        in_specs.append(pl.BlockSpec((None, N_MOD, d), lambda i: (mod_row(i), 0, 0)))
        args.append(mod3)
    in_specs.append(pl.BlockSpec((1, d), lambda i: (0, 0)))
    args.append(gain.reshape(1, d))
    out_shape, out_specs = [], []
    if emit_x:
        out_shape.append(jax.ShapeDtypeStruct((m_all, d), F32))
        out_specs.append(pl.BlockSpec((tm, d), lambda i: (i, 0)))
    out_shape.append(jax.ShapeDtypeStruct((m_all, d), out_dtype))
    out_specs.append(pl.BlockSpec((tm, d), lambda i: (i, 0)))
    kern = functools.partial(_resnorm_kernel, has_delta=has_delta, res_scale=res_scale,
                             gate_idx=gate_idx, mod_idx=mod_idx, emit_x=emit_x)
    return pl.pallas_call(
        kern, out_shape=out_shape, grid=(m_all // tm,), in_specs=in_specs, out_specs=out_specs,
        compiler_params=_cparams("arbitrary"), name="resnorm",
    )(*args)


def _permnorm_kernel(x_ref, mod_ref, gain_ref, h_ref, *, mod_idx):
    x = x_ref[...]
    y = x * lax.rsqrt(jnp.mean(x * x, axis=-1, keepdims=True) + RMS_EPS)
    y = y * gain_ref[...]
    sh = mod_ref[mod_idx:mod_idx + 1, :]
    sc = mod_ref[mod_idx + 1:mod_idx + 2, :]
    h_ref[...] = (y * (1.0 + sc) + sh).astype(h_ref.dtype)


def _permnorm(x, mod3, gain, dims, *, mod_idx):
    mc, tc, tl, nlb = dims
    m, d = x.shape
    rows = tl // GRID_W
    assert mc % tl == 0 and rows % 16 == 0
    seq_off = mc // tl
    x3 = x.reshape(m // tl, rows, GRID_W * d)
    return pl.pallas_call(
        functools.partial(_permnorm_kernel, mod_idx=mod_idx),
        out_shape=jax.ShapeDtypeStruct((nlb * tl, d), BF16),
        grid=(nlb, GRID_W),
        in_specs=[pl.BlockSpec((None, rows, d), lambda b, w: (b + seq_off, 0, w)),
                  pl.BlockSpec((None, N_MOD, d), lambda b, w: (b, 0, 0)),
                  pl.BlockSpec((1, d), lambda b, w: (0, 0))],
        out_specs=pl.BlockSpec((rows, d), lambda b, w: (b * GRID_W + w, 0)),
        compiler_params=_cparams("arbitrary", "arbitrary"), name="permnorm",
    )(x3, mod3, gain.reshape(1, d))


def _mm_kernel(a_ref, w_ref, o_ref):
    o_ref[...] = jnp.dot(a_ref[...], w_ref[...], preferred_element_type=F32).astype(o_ref.dtype)


def _matmul(a, w, *, out_dtype, tm, tn, row_off=0, rows=None, name="matmul"):
    k = a.shape[1]
    n = w.shape[1]
    rows = a.shape[0] if rows is None else rows
    ro = row_off // tm
    return pl.pallas_call(
        _mm_kernel,
        out_shape=jax.ShapeDtypeStruct((rows, n), out_dtype),
        grid=(rows // tm, n // tn),
        in_specs=[pl.BlockSpec((tm, k), lambda i, j: (i + ro, 0)),
                  pl.BlockSpec((k, tn), lambda i, j: (0, j))],
        out_specs=pl.BlockSpec((tm, tn), lambda i, j: (i, j)),
        compiler_params=_cparams("arbitrary", "arbitrary"), name=name,
    )(a, w)


def _ffn_kernel(h_ref, wa_ref, wb_ref, wo_ref, o_ref):
    @pl.when(pl.program_id(1) == 0)
    def _():
        o_ref[...] = jnp.zeros_like(o_ref)

    h = h_ref[...]
    a = jnp.dot(h, wa_ref[...], preferred_element_type=F32)
    b = jnp.dot(h, wb_ref[...], preferred_element_type=F32)
    u = (a * jax.nn.sigmoid(a) * b).astype(BF16)
    o_ref[...] += jnp.dot(u, wo_ref[...], preferred_element_type=F32)


def _ffn(h, w_in, w_out, *, tm, tf):
    m, d = h.shape
    dff = w_out.shape[0]
    nf = dff // tf
    return pl.pallas_call(
        _ffn_kernel,
        out_shape=jax.ShapeDtypeStruct((m, d), F32),
        grid=(m // tm, nf),
        in_specs=[pl.BlockSpec((tm, d), lambda i, j: (i, 0)),
                  pl.BlockSpec((d, tf), lambda i, j: (0, j)),
                  pl.BlockSpec((d, tf), lambda i, j: (0, j + nf)),
                  pl.BlockSpec((tf, d), lambda i, j: (j, 0))],
        out_specs=pl.BlockSpec((tm, d), lambda i, j: (i, 0)),
        compiler_params=_cparams("arbitrary", "arbitrary"), name="ffn",
    )(h, w_in, w_in, w_out)


def _merge_kernel(yr_ref, yh_ref, wr_ref, wh_ref, gr_ref, gh_ref, o_ref):
    pr = jnp.dot(yr_ref[...], wr_ref[...], preferred_element_type=F32)
    ph = jnp.dot(yh_ref[...], wh_ref[...], preferred_element_type=F32)
    o_ref[...] = (jax.nn.sigmoid(gr_ref[...]) * pr
                  + jax.nn.sigmoid(gh_ref[...]) * ph).astype(o_ref.dtype)


def _merge(y_r, y_h, w_br, w_bh, z_rest, col_gr, col_gh, *, tm, tn):
    m, kr = y_r.shape
    kh = y_h.shape[1]
    d = w_br.shape[1]
    cgr, cgh = col_gr // tn, col_gh // tn
    return pl.pallas_call(
        _merge_kernel,
        out_shape=jax.ShapeDtypeStruct((m, d), BF16),
        grid=(m // tm, d // tn),
        in_specs=[pl.BlockSpec((tm, kr), lambda i, j: (i, 0)),
                  pl.BlockSpec((tm, kh), lambda i, j: (i, 0)),
                  pl.BlockSpec((kr, tn), lambda i, j: (0, j)),
                  pl.BlockSpec((kh, tn), lambda i, j: (0, j)),
                  pl.BlockSpec((tm, tn), lambda i, j: (i, j + cgr)),
                  pl.BlockSpec((tm, tn), lambda i, j: (i, j + cgh))],
        out_specs=pl.BlockSpec((tm, tn), lambda i, j: (i, j)),
        compiler_params=_cparams("arbitrary", "arbitrary"), name="merge",
    )(y_r, y_h, w_br, w_bh, z_rest, z_rest)


def _log2(n):
    assert n & (n - 1) == 0
    return n.bit_length() - 1


def _head_ones(n, group):
    sh = _log2(group)
    i = lax.broadcasted_iota(jnp.int32, (n, n), 0) >> sh
    j = lax.broadcasted_iota(jnp.int32, (n, n), 1) >> sh
    return jnp.where(i == j, 1.0, 0.0).astype(BF16)


def _rwkv_prep_kernel(z_ref, zp_ref, zn_ref, mup_ref, mun_ref, w0_ref, w2_ref, a0_ref, a2_ref,
                      g2_ref, kk_ref,
                      r_o, k_o, v_o, kkn_o, af_o, ab_o, lwf_o, lwb_o, g_o,
                      *, tm, mc, tc, tl, dr, wl, al, gl):
    row0 = pl.program_id(0) * tm
    in_ctx = row0 < mc
    pos = jnp.where(in_ctx, row0 % tc, (row0 - mc) % tl)
    seqlen = jnp.where(in_ctx, tc, tl)
    keep_prev = jnp.where(pos == 0, 0.0, 1.0)
    keep_next = jnp.where(pos + tm == seqlen, 0.0, 1.0)
    rows = lax.broadcasted_iota(jnp.int32, (tm, 1), 0)

    def shifted(c0, c1):
        z = z_ref[:, c0:c1]
        zp = jnp.where(rows == 0, keep_prev * zp_ref[7:8, c0:c1], pltpu.roll(z, 1, 0))
        zn = jnp.where(rows == tm - 1, keep_next * zn_ref[0:1, c0:c1], pltpu.roll(z, tm - 1, 0))
        return z + mup_ref[:, c0:c1] * (zp - z) + mun_ref[:, c0:c1] * (zn - z)

    bd = _head_ones(LANES, RWKV_HEAD)
    for j in range(dr // LANES):
        c0 = j * LANES
        r_o[:, c0:c0 + LANES] = shifted(c0, c0 + LANES)
        k = shifted(dr + c0, dr + c0 + LANES)
        k_o[:, c0:c0 + LANES] = k
        v_o[:, c0:c0 + LANES] = shifted(2 * dr + c0, 2 * dr + c0 + LANES)
        kk = k * kk_ref[:, c0:c0 + LANES]
        ss = _group_sum(kk * kk, bd)
        kkn_o[:, c0:c0 + LANES] = kk / jnp.maximum(jnp.sqrt(ss), 1e-12)

    c = 3 * dr
    for d, (lw_o, a_o) in enumerate(((lwf_o, af_o), (lwb_o, ab_o))):
        wd = shifted(c + d * wl, c + (d + 1) * wl)
        ad = shifted(c + 2 * wl + d * al, c + 2 * wl + (d + 1) * al)
        wpre = w0_ref[d:d + 1, :] + _dot(jnp.tanh(wd), w2_ref[d])
        nx = -wpre
        softplus = jnp.maximum(nx, 0.0) + jnp.log1p(jnp.exp(-jnp.abs(nx)))
        w_log = -softplus - 0.5
        lw_o[...] = -jnp.exp(w_log)
        a_o[...] = jax.nn.sigmoid(a0_ref[d:d + 1, :] + _dot(ad, a2_ref[d]))
    gd = shifted(c + 2 * wl + 2 * al, c + 2 * wl + 2 * al + gl)
    g_o[...] = _dot(jax.nn.sigmoid(gd), g2_ref[...])


def _rwkv_prep(z_rest, p, dims, *, tm, zr_cols, dr, wl, al, gl):
    mc, tc, tl, nlb = dims
    m = z_rest.shape[0]
    nblk8 = m // 8
    t8 = tm // 8
    full = lambda shape: pl.BlockSpec(shape, lambda i: (0,) * len(shape))
    outs = [jax.ShapeDtypeStruct((m, dr), F32)] * 9
    kern = functools.partial(_rwkv_prep_kernel, tm=tm, mc=mc, tc=tc, tl=tl, dr=dr, wl=wl, al=al,
                             gl=gl)
    return pl.pallas_call(
        kern, out_shape=outs, grid=(m // tm,),
        in_specs=[pl.BlockSpec((tm, zr_cols), lambda i: (i, 0)),
                  pl.BlockSpec((8, zr_cols), lambda i: (jnp.maximum(i * t8 - 1, 0), 0)),
                  pl.BlockSpec((8, zr_cols), lambda i: (jnp.minimum((i + 1) * t8, nblk8 - 1), 0)),
                  full((1, zr_cols)), full((1, zr_cols)),
                  full((2, dr)), full((2, wl, dr)), full((2, dr)), full((2, al, dr)),
                  full((gl, dr)), full((1, dr))],
        out_specs=[pl.BlockSpec((tm, dr), lambda i: (i, 0))] * 9,
        compiler_params=_cparams("arbitrary"), name="rwkv_prep",
    )(z_rest, z_rest, z_rest, p["mu_prev"], p["mu_next"], p["w0"], p["w2"], p["a0"], p["a2"],
      p["g2"], p["k_k"])


def _stack_heads(x, lo):
    return jnp.concatenate([jnp.where(lo, x, 0.0), jnp.where(lo, 0.0, x)], axis=0)


SPLIT_LEVELS = 4


def _rwkv_masks(rev):
    c = CHUNK
    n2 = 2 * c
    ti = lax.broadcasted_iota(jnp.int32, (c, c), 0)
    tj = lax.broadcasted_iota(jnp.int32, (c, c), 1)
    cum = jnp.where((tj >= ti) if rev else (tj <= ti), 1.0, 0.0).astype(BF16)
    lo = lax.broadcasted_iota(jnp.int32, (c, LANES), 1) < RWKV_HEAD
    ii = lax.broadcasted_iota(jnp.int32, (n2, n2), 0)
    jj = lax.broadcasted_iota(jnp.int32, (n2, n2), 1)
    im, jm = ii & (c - 1), jj & (c - 1)
    strict = (jm > im) if rev else (jm < im)
    incl = (jm >= im) if rev else (jm <= im)
    eye = jnp.where(ii == jj, 1.0, 0.0)
    return cum, lo, strict, incl, eye


def _rwkv_chunk(r, k, v, kk, a, lw, k_a, s, rev, masks):
    c = CHUNK
    n2 = 2 * c
    cum, lo, strict, incl, eye = masks
    bc = _exact_rowmix(cum, lw)
    yield
    bex = bc - lw
    btot = bc[0:1, :] if rev else bc[c - 1:c, :]
    kd = k * (1.0 + (a - 1.0) * k_a)
    bv = kk * a
    e_neg = jnp.exp(-bc)
    e_tot = jnp.exp(btot - bc)
    at = _stack_heads(-kk * jnp.exp(bex), lo).astype(BF16)
    rt_f = _stack_heads(r * jnp.exp(bc), lo)
    kh = _stack_heads(kd * e_neg, lo).astype(BF16)
    bh = _stack_heads(bv * e_neg, lo).astype(BF16)
    vb_f = _stack_heads(v, lo)
    vb = vb_f.astype(BF16)
    kc = _stack_heads(kd * e_tot, lo).astype(BF16)
    bcc = _stack_heads(bv * e_tot, lo).astype(BF16)

    gram = _dot_nt(jnp.concatenate([at, rt_f.astype(BF16)], axis=0),
                   jnp.concatenate([bh, kh], axis=0))
    yield
    nmat = jnp.where(strict, gram[:n2, :n2], 0.0)
    a_ka = jnp.where(strict, gram[:n2, n2:], 0.0)
    a_rb = jnp.where(incl, gram[n2:, :n2], 0.0)
    a_rk = jnp.where(incl, gram[n2:, n2:], 0.0)
    tm_ = eye + nmat
    pw = nmat
    for lvl in range(5):
        dot = _dot_split if lvl < SPLIT_LEVELS else _dot
        pw = dot(pw, pw)
        yield
        tm_ = tm_ + dot(tm_, pw)
    x1 = _dot(a_ka, vb)
    yield
    aw = _dot(tm_, jnp.concatenate([at, x1.astype(BF16)], axis=1))
    yield
    ah, w = aw[:, :n2], aw[:, n2:]
    rhs = jnp.concatenate(
        [jnp.concatenate([vb, jnp.zeros_like(vb)], axis=1),
         jnp.concatenate([w.astype(BF16), ah.astype(BF16)], axis=1)], axis=0)
    qr = _dot(jnp.concatenate([a_rk, a_rb], axis=1), rhs)
    q = qr[:, :n2]
    rr = rt_f + qr[:, n2:]
    mm = _dot_tn(ah, bcc)
    nn = _dot_tn(jnp.concatenate([vb_f, w], axis=0), jnp.concatenate([kc, bcc], axis=0))
    yield
    o_bd = _dot_nt(rr, s) + q
    o = o_bd[:c, :] + o_bd[c:, :]
    s_new = s * jnp.exp(btot) + _dot(s, mm) + nn
    return o, s_new


def _rwkv_scan_kernel(*refs, has_init, emit_final, nchunk, npairs):
    it = iter(refs)
    fwd = [next(it) for _ in range(6)]
    bwd = [next(it) for _ in range(6)]
    ka_ref = next(it)
    s0_ref = next(it) if has_init else None
    of_ref = next(it)
    ob_ref = next(it)
    sfin_ref = next(it) if emit_final else None
    s_scr = next(it)
    ci = pl.program_id(2)

    @pl.when(ci == 0)
    def _():
        if has_init:
            s_scr[...] = s0_ref[...]
        else:
            s_scr[...] = jnp.zeros_like(s_scr)

    chains, sinks = [], []
    for d, (ins, o_ref) in enumerate(((fwd, of_ref), (bwd, ob_ref))):
        masks = _rwkv_masks(rev=(d == 1))
        for hp in range(npairs):
            sl = slice(hp * LANES, (hp + 1) * LANES)
            r, k, v, kk, a, lw = (x[:, sl] for x in ins)
            chains.append(_rwkv_chunk(r, k, v, kk, a, lw, ka_ref[:, sl], s_scr[d, hp],
                                      rev=(d == 1), masks=masks))
            sinks.append((o_ref, sl, d, hp))
    for (o, s_new), (o_ref, sl, d, hp) in zip(_lockstep(chains), sinks):
        o_ref[:, sl] = o
        s_scr[d, hp] = s_new

    if emit_final:
        @pl.when(ci == nchunk - 1)
        def _():
            sfin_ref[...] = s_scr[...]


def _rwkv_scan(pre, k_a, s0_bd, *, row_off, nb, t, emit_final, npairs):
    m, dr = pre["r"].shape
    ngroup = dr // (LANES * npairs)
    nchunk = t // CHUNK
    rb = row_off // CHUNK
    has_init = s0_bd is not None
    fmap = lambda b, h, c: (rb + b * nchunk + c, h)
    bmap = lambda b, h, c: (rb + b * nchunk + (nchunk - 1 - c), h)
    blk = (CHUNK, LANES * npairs)
    in_specs = ([pl.BlockSpec(blk, fmap)] * 6 + [pl.BlockSpec(blk, bmap)] * 6
                + [pl.BlockSpec((1, LANES * npairs), lambda b, h, c: (0, h))])
    args = [pre["r"], pre["k"], pre["v"], pre["kk"], pre["a_f"], pre["lw_f"],
            pre["r"], pre["k"], pre["v"], pre["kk"], pre["a_b"], pre["lw_b"], k_a]
    sblk = pl.BlockSpec((None, 2, npairs, LANES, LANES), lambda b, h, c: (b, 0, h, 0, 0))
    if has_init:
        in_specs.append(sblk)
        args.append(s0_bd)
    rows = nb * t
    out_shape = [jax.ShapeDtypeStruct((rows, dr), F32)] * 2
    out_specs = [pl.BlockSpec(blk, lambda b, h, c: (b * nchunk + c, h)),
                 pl.BlockSpec(blk, lambda b, h, c: (b * nchunk + (nchunk - 1 - c), h))]
    if emit_final:
        out_shape.append(jax.ShapeDtypeStruct((nb, 2, dr // LANES, LANES, LANES), F32))
        out_specs.append(sblk)
    kern = functools.partial(_rwkv_scan_kernel, has_init=has_init, emit_final=emit_final,
                             nchunk=nchunk, npairs=npairs)
    return pl.pallas_call(
        kern, out_shape=out_shape, grid=(nb, ngroup, nchunk), in_specs=in_specs,
        out_specs=out_specs, scratch_shapes=[pltpu.VMEM((2, npairs, LANES, LANES), F32)],
        compiler_params=_cparams("arbitrary", "arbitrary", "arbitrary"), name="rwkv_scan",
    )(*args)


def _rwkv_post_kernel(of_ref, ob_ref, r_ref, k_ref, v_ref, af_ref, ab_ref, g_ref,
                      ka_ref, rk_ref, lnw_ref, lnb_ref, y_ref, *, dr):
    bd = _head_ones(LANES, RWKV_HEAD)
    inv = 1.0 / RWKV_HEAD
    for j in range(dr // LANES):
        sl = slice(j * LANES, (j + 1) * LANES)
        o = of_ref[:, sl] + ob_ref[:, sl]
        mean = _group_sum(o, bd) * inv
        oc = o - mean
        var = _group_sum(oc * oc, bd) * inv
        on = oc * lax.rsqrt(var + GN_EPS) * lnw_ref[:, sl] + lnb_ref[:, sl]
        r, k, v = r_ref[:, sl], k_ref[:, sl], v_ref[:, sl]
        k_a, r_k = ka_ref[:, sl], rk_ref[:, sl]
        bonus = 0.0
        for a_ref in (af_ref, ab_ref):
            kd = k * (1.0 + (a_ref[:, sl] - 1.0) * k_a)
            bonus = bonus + _group_sum(r * kd * r_k, bd) * v
        y_ref[:, sl] = ((on + bonus) * g_ref[:, sl]).astype(y_ref.dtype)


def _rwkv_post(o_f, o_b, pre, p, *, tm, row_off):
    rows, dr = o_f.shape
    ro = row_off // tm
    own = pl.BlockSpec((tm, dr), lambda i: (i, 0))
    glob = pl.BlockSpec((tm, dr), lambda i: (i + ro, 0))
    par = pl.BlockSpec((1, dr), lambda i: (0, 0))
    return pl.pallas_call(
        functools.partial(_rwkv_post_kernel, dr=dr),
        out_shape=jax.ShapeDtypeStruct((rows, dr), BF16),
        grid=(rows // tm,),
        in_specs=[own, own] + [glob] * 6 + [par] * 4,
        out_specs=own,
        compiler_params=_cparams("arbitrary"), name="rwkv_post",
    )(o_f, o_b, pre["r"], pre["k"], pre["v"], pre["a_f"], pre["a_b"], pre["g"],
      p["k_a"], p["r_k"], p["ln_w"], p["ln_b"])


HGRN_LEVELS = 6


def _hgrn_mix_matrix(rev):
    c = CHUNK
    t = lax.broadcasted_iota(jnp.int32, (c, c), 0)
    i = lax.broadcasted_iota(jnp.int32, (c, c), 1)
    mats = []
    cum = (i >= t) if rev else (i <= t)
    mats.append(jnp.where(cum, 1.0, 0.0))
    for lvl in range(HGRN_LEVELS):
        half = 1 << lvl
        base = (t >> (lvl + 1)) << (lvl + 1)
        if rev:
            mid = base + half
            ref_ = i >= mid
        else:
            mid = base + half - 1
            ref_ = i <= mid
        mats.append(jnp.where(cum, 1.0, 0.0) - jnp.where(ref_, 1.0, 0.0))
    return jnp.concatenate(mats, axis=0).astype(BF16)


def _hgrn_masks(rev):
    c = CHUNK
    ti = lax.broadcasted_iota(jnp.int32, (c, c), 0)
    tj = lax.broadcasted_iota(jnp.int32, (c, c), 1)
    late, early = (tj, ti) if rev else (ti, tj)
    sels = [((late >> lvl) == (early >> lvl) + 1) & (((early >> lvl) & 1) == 0)
            for lvl in range(HGRN_LEVELS)]
    return _hgrn_mix_matrix(rev), ti == tj, sels


def _hgrn_chunk(q_pre, f_pre, v, lb, st, rev, masks):
    c = CHUNK
    mix, diag, sels = masks
    q = q_pre * jax.nn.sigmoid(q_pre)
    f = lb + (1.0 - lb) * jax.nn.sigmoid(f_pre)
    kf = 1.0 - f
    g = jnp.log(f)
    ex = _exact_rowmix(mix, g)
    yield
    b = ex[:c, :]
    btot = b[0:1, :] if rev else b[c - 1:c, :]
    attn = jnp.where(diag, _dot_nt(q, kf), 0.0)
    for lvl in range(HGRN_LEVELS):
        dq = ex[(lvl + 1) * c:(lvl + 2) * c, :]
        qs = q * jnp.exp(jnp.minimum(dq, 0.0))
        ks = kf * jnp.exp(jnp.minimum(-dq, 0.0))
        attn = attn + jnp.where(sels[lvl], _dot_nt(qs, ks), 0.0)
    yield
    o = _dot_nt(q * jnp.exp(b), st) + _dot(attn, v)
    st_new = st * jnp.exp(btot) + _dot_tn(v, kf * jnp.exp(btot - b))
    return o, st_new


def _hgrn_scan_kernel(*refs, has_init, emit_final, nchunk, nheads):
    it = iter(refs)
    qf_ref, ff_ref, vf_ref = next(it), next(it), next(it)
    qb_ref, fb_ref, vb_ref = next(it), next(it), next(it)
    lb_ref = next(it)
    s0_ref = next(it) if has_init else None
    of_ref, ob_ref = next(it), next(it)
    sfin_ref = next(it) if emit_final else None
    s_scr = next(it)
    ci = pl.program_id(2)

    @pl.when(ci == 0)
    def _():
        if has_init:
            for d in range(2):
                for h in range(nheads):
                    s_scr[d, h] = s0_ref[d, h].T
        else:
            s_scr[...] = jnp.zeros_like(s_scr)

    chains, sinks = [], []
    for d, (ins, o_ref) in enumerate((((qf_ref, ff_ref, vf_ref), of_ref),
                                      ((qb_ref, fb_ref, vb_ref), ob_ref))):
        masks = _hgrn_masks(rev=(d == 1))
        for h in range(nheads):
            sl = slice(h * LANES, (h + 1) * LANES)
            qp, fp, v = (x[:, sl] for x in ins)
            chains.append(_hgrn_chunk(qp, fp, v, lb_ref[:, sl], s_scr[d, h], rev=(d == 1),
                                      masks=masks))
            sinks.append((o_ref, sl, d, h))
    for (o, s_new), (o_ref, sl, d, h) in zip(_lockstep(chains), sinks):
        o_ref[:, sl] = o
        s_scr[d, h] = s_new

    if emit_final:
        @pl.when(ci == nchunk - 1)
        def _():
            for d in range(2):
                for h in range(nheads):
                    sfin_ref[d, h] = s_scr[d, h].T


def _hgrn_scan(zhs, lb, s0, *, nb, t, dg, emit_final, nheads):
    ng = dg // (LANES * nheads)
    nchunk = t // CHUNK
    has_init = s0 is not None
    blk = (CHUNK, LANES * nheads)

    def spec(group, rev):
        if rev:
            return pl.BlockSpec(blk, lambda b, h, c: (b * nchunk + nchunk - 1 - c, group * ng + h))
        return pl.BlockSpec(blk, lambda b, h, c: (b * nchunk + c, group * ng + h))

    in_specs = [spec(0, False), spec(1, False), spec(3, False),
                spec(0, True), spec(2, True), spec(3, True),
                pl.BlockSpec((1, LANES * nheads), lambda b, h, c: (0, h))]
    args = [zhs] * 6 + [lb]
    sblk = pl.BlockSpec((None, 2, nheads, LANES, LANES), lambda b, h, c: (b, 0, h, 0, 0))
    if has_init:
        in_specs.append(sblk)
        args.append(s0)
    out_shape = [jax.ShapeDtypeStruct((nb * t, dg), F32)] * 2
    out_specs = [pl.BlockSpec(blk, lambda b, h, c: (b * nchunk + c, h)),
                 pl.BlockSpec(blk, lambda b, h, c: (b * nchunk + nchunk - 1 - c, h))]
    if emit_final:
        out_shape.append(jax.ShapeDtypeStruct((nb, 2, dg // LANES, LANES, LANES), F32))
        out_specs.append(sblk)
    kern = functools.partial(_hgrn_scan_kernel, has_init=has_init, emit_final=emit_final,
                             nchunk=nchunk, nheads=nheads)
    return pl.pallas_call(
        kern, out_shape=out_shape, grid=(nb, ng, nchunk), in_specs=in_specs, out_specs=out_specs,
        scratch_shapes=[pltpu.VMEM((2, nheads, LANES, LANES), F32)],
        compiler_params=_cparams("arbitrary", "arbitrary", "arbitrary"), name="hgrn_scan",
    )(*args)


def _hgrn_post_kernel(of_ref, ob_ref, gate_ref, gain_ref, y_ref, *, dg):
    gain = gain_ref[...]
    for j in range(dg // LANES):
        sl = slice(j * LANES, (j + 1) * LANES)
        o = of_ref[:, sl] + ob_ref[:, sl]
        on = o * lax.rsqrt(jnp.mean(o * o, axis=-1, keepdims=True) + RMS_EPS) * gain
        gt = gate_ref[:, sl]
        y_ref[:, sl] = (on * (gt * jax.nn.sigmoid(gt))).astype(y_ref.dtype)


def _hgrn_post(o_f, o_b, z_rest, gain, *, gate_col, row_off, nb, t, permuted):
    rows_all, dg = o_f.shape
    kern = functools.partial(_hgrn_post_kernel, dg=dg)
    gcb = gate_col // dg
    par = pl.BlockSpec((1, LANES), lambda *_: (0, 0))
    if permuted:
        nrow = t // GRID_W
        tm = GRID_W
        o3f = o_f.reshape(nb, GRID_W, nrow * dg)
        o3b = o_b.reshape(nb, GRID_W, nrow * dg)
        ospec = pl.BlockSpec((None, GRID_W, dg), lambda b, r: (b, 0, r))
        rb = row_off // tm
        return pl.pallas_call(
            kern, out_shape=jax.ShapeDtypeStruct((rows_all, dg), BF16), grid=(nb, nrow),
            in_specs=[ospec, ospec,
                      pl.BlockSpec((tm, dg), lambda b, r: (rb + b * nrow + r, gcb)), par],
            out_specs=pl.BlockSpec((tm, dg), lambda b, r: (b * nrow + r, 0)),
            compiler_params=_cparams("arbitrary", "arbitrary"), name="hgrn_post_perm",
        )(o3f, o3b, z_rest, gain)
    tm = min(256, rows_all)
    rb = row_off // tm
    ospec = pl.BlockSpec((tm, dg), lambda i: (i, 0))
    return pl.pallas_call(
        kern, out_shape=jax.ShapeDtypeStruct((rows_all, dg), BF16), grid=(rows_all // tm,),
        in_specs=[ospec, ospec, pl.BlockSpec((tm, dg), lambda i: (rb + i, gcb)), par],
        out_specs=ospec,
        compiler_params=_cparams("arbitrary"), name="hgrn_post",
    )(o_f, o_b, z_rest, gain)


def _largest_divisor(n, candidates):
    for cand in candidates:
        if n % cand == 0:
            return cand
    raise ValueError(f"no tile in {candidates} divides {n}")


def _pad_to(x, axis, size):
    pad = size - x.shape[axis]
    if pad == 0:
        return x
    widths = [(0, 0)] * x.ndim
    widths[axis] = (0, pad)
    return jnp.pad(x, widths)


def _pair_state_to_bd(s):
    b, two, h, n, _ = s.shape
    s = s.reshape(b, two, h // 2, 2, n, n)
    z = jnp.zeros_like(s[:, :, :, 0])
    top = jnp.concatenate([s[:, :, :, 0], z], axis=-1)
    bot = jnp.concatenate([z, s[:, :, :, 1]], axis=-1)
    return jnp.concatenate([top, bot], axis=-2)


def _bd_to_pair_state(sbd):
    b, two, hp, n2, _ = sbd.shape
    n = n2 // 2
    s0 = sbd[:, :, :, :n, :n]
    s1 = sbd[:, :, :, n:, n:]
    return jnp.stack([s0, s1], axis=3).reshape(b, two, hp * 2, n, n)


def kernel(x_prompt, x_sample, state_rwkv, state_hgrn, c, c_ctx, w_mod, b_mod, norm_ffn1, norm_mix, norm_ffn2, ffn1_w_in, ffn1_w_out, ffn2_w_in, ffn2_w_out, w_in, rwkv_mu_prev, rwkv_mu_next, rwkv_w0, rwkv_w2, rwkv_a0, rwkv_a2, rwkv_g2, rwkv_k_k, rwkv_k_a, rwkv_r_k, rwkv_ln_w, rwkv_ln_b, hgrn_lb_logits, hgrn_norm, w_branch_rwkv, w_branch_hgrn, w_out, norm_final):
    nbc, tc, d = x_prompt.shape
    nbl, tl, _ = x_sample.shape
    depth = w_mod.shape[0]
    mc, ml = nbc * tc, nbl * tl
    m = mc + ml
    dims = (mc, tc, tl, nbl)
    dr = rwkv_w0.shape[-1]
    dg = w_branch_hgrn.shape[1]
    dff = ffn1_w_out.shape[1]
    wl, al, gl = rwkv_w2.shape[2], rwkv_a2.shape[2], rwkv_g2.shape[1]
    glp = -(-gl // LANES) * LANES
    rw_cols = 3 * dr + 2 * wl + 2 * al + gl
    zr_cols = 3 * dr + 2 * wl + 2 * al + glp
    assert nbl < MOD_ROWS and tl % GRID_W == 0 and tl // GRID_W == CHUNK and tc % CHUNK == 0

    tm_norm = min(256, tc)
    tm_mm = min(512, mc)
    tn_of = lambda n: _largest_divisor(n, (512, 256, 128))
    tf = _largest_divisor(dff, (256, 128))

    lb_all = jnp.cumsum(jax.nn.softmax(hgrn_lb_logits.astype(F32), axis=0), axis=0)
    cond = jnp.concatenate([c, c_ctx[None, :], jnp.zeros((MOD_ROWS - nbl - 1, d), F32)], axis=0)

    x = jnp.concatenate([x_prompt.reshape(mc, d), x_sample.reshape(ml, d)], axis=0)
    new_r, new_h = [], []
    for l in range(depth):
        mod3 = _modulation(cond, w_mod[l], b_mod[l], tn_of(N_MOD * d)).reshape(MOD_ROWS, N_MOD, d)

        h = _resnorm(x, None, mod3, norm_ffn1[l], dims, res_scale=None, gate_idx=None, mod_idx=0,
                     emit_x=False, out_dtype=BF16, tm=tm_norm)[0]
        delta = _ffn(h, ffn1_w_in[l].astype(BF16), ffn1_w_out[l].astype(BF16), tm=tm_mm, tf=tf)

        x, h = _resnorm(x, delta, mod3, norm_mix[l], dims, res_scale=FFN_HALF, gate_idx=2,
                        mod_idx=3, emit_x=True, out_dtype=BF16, tm=tm_norm)
        h_perm = _permnorm(x, mod3, norm_mix[l], dims, mod_idx=3)
        wi = w_in[l]
        hg0 = rw_cols
        w_r = _pad_to(wi[:, :rw_cols], 1, zr_cols).astype(BF16)
        w_hs = wi[:, hg0:hg0 + 4 * dg].astype(BF16)
        w_gates = wi[:, hg0 + 4 * dg:].astype(BF16)
        col_gate, col_gr, col_gh = 0, dg, dg + d
        z_r = _matmul(h, w_r, out_dtype=F32, tm=tm_mm, tn=tn_of(zr_cols), name="proj_rwkv")
        z_g = _matmul(h, w_gates, out_dtype=F32, tm=tm_mm, tn=tn_of(dg), name="proj_gates")
        zhs_c = _matmul(h, w_hs, out_dtype=F32, tm=tm_mm, tn=tn_of(dg), rows=mc,
                        name="proj_hs_ctx")
        zhs_l = _matmul(h_perm, w_hs, out_dtype=F32, tm=tm_mm, tn=tn_of(dg), name="proj_hs_lat")

        p = {
            "mu_prev": _pad_to(rwkv_mu_prev[l][None, :], 1, zr_cols),
            "mu_next": _pad_to(rwkv_mu_next[l][None, :], 1, zr_cols),
            "w0": rwkv_w0[l], "w2": rwkv_w2[l].astype(BF16),
            "a0": rwkv_a0[l], "a2": rwkv_a2[l].astype(BF16),
            "g2": _pad_to(rwkv_g2[l], 0, glp).astype(BF16),
            "k_k": rwkv_k_k[l][None, :], "k_a": rwkv_k_a[l][None, :],
            "r_k": rwkv_r_k[l].reshape(1, dr),
            "ln_w": rwkv_ln_w[l][None, :], "ln_b": rwkv_ln_b[l][None, :],
        }
        names = ("r", "k", "v", "kk", "a_f", "a_b", "lw_f", "lw_b", "g")
        pre = dict(zip(names, _rwkv_prep(z_r, p, dims, tm=min(128, tc), zr_cols=zr_cols, dr=dr,
                                         wl=wl, al=al, gl=glp)))
        npairs = _largest_divisor(dr // LANES, (4, 2, 1))
        of_c, ob_c, sr_c = _rwkv_scan(pre, p["k_a"], None, row_off=0, nb=nbc, t=tc,
                                      emit_final=True, npairs=npairs)
        of_l, ob_l = _rwkv_scan(pre, p["k_a"], _pair_state_to_bd(state_rwkv[:, l].astype(F32)),
                                row_off=mc, nb=nbl, t=tl, emit_final=False, npairs=npairs)
        yr_c = _rwkv_post(of_c, ob_c, pre, p, tm=min(256, tc), row_off=0)
        yr_l = _rwkv_post(of_l, ob_l, pre, p, tm=min(256, tc), row_off=mc)
        y_r = jnp.concatenate([yr_c, yr_l], axis=0)
        new_r.append(_bd_to_pair_state(sr_c))

        lb = lb_all[l][None, :]
        gain_h = hgrn_norm[l][None, :]
        nheads = _largest_divisor(dg // LANES, (4, 2, 1))
        hf_c, hb_c, sh_c = _hgrn_scan(zhs_c, lb, None, nb=nbc, t=tc, dg=dg, emit_final=True,
                                      nheads=nheads)
        hf_l, hb_l = _hgrn_scan(zhs_l, lb, state_hgrn[:, l].astype(F32), nb=nbl, t=tl, dg=dg,
                                emit_final=False, nheads=nheads)
        yh_c = _hgrn_post(hf_c, hb_c, z_g, gain_h, gate_col=col_gate, row_off=0, nb=nbc, t=tc,
                          permuted=False)
        yh_l = _hgrn_post(hf_l, hb_l, z_g, gain_h, gate_col=col_gate, row_off=mc, nb=nbl, t=tl,
                          permuted=True)
        y_h = jnp.concatenate([yh_c, yh_l], axis=0)
        new_h.append(sh_c)

        merged = _merge(y_r, y_h, w_branch_rwkv[l].astype(BF16), w_branch_hgrn[l].astype(BF16),
                        z_g, col_gr, col_gh, tm=tm_mm, tn=tn_of(dg))
        delta = _matmul(merged, w_out[l].astype(BF16), out_dtype=F32, tm=tm_mm, tn=tn_of(d),
                        name="mix_out")

        x, h = _resnorm(x, delta, mod3, norm_ffn2[l], dims, res_scale=1.0, gate_idx=5, mod_idx=6,
                        emit_x=True, out_dtype=BF16, tm=tm_norm)
        delta = _ffn(h, ffn2_w_in[l].astype(BF16), ffn2_w_out[l].astype(BF16), tm=tm_mm, tf=tf)
        last = l == depth - 1
        if not last:
            x = _resnorm(x, delta, mod3, norm_ffn2[l], dims, res_scale=FFN_HALF, gate_idx=8,
                         mod_idx=None, emit_x=True, out_dtype=F32, tm=tm_norm)[0]

    y_c = _resnorm(x, delta, mod3, norm_final, dims, res_scale=FFN_HALF, gate_idx=8, mod_idx=None,
                   emit_x=False, out_dtype=F32, tm=tm_norm, row_off=0, rows=mc)[0]
    y_l = _resnorm(x, delta, mod3, norm_final, dims, res_scale=FFN_HALF, gate_idx=8, mod_idx=None,
                   emit_x=False, out_dtype=F32, tm=tm_norm, row_off=mc, rows=ml)[0]
    return (y_c.reshape(nbc, tc, d), y_l.reshape(nbl, tl, d),
            jnp.stack(new_r, axis=1), jnp.stack(new_h, axis=1))
```

```python
import functools

import jax
import jax.numpy as jnp
from jax import lax
from jax.experimental import pallas as pl
from jax.experimental.pallas import tpu as pltpu

F32 = jnp.float32
BF16 = jnp.bfloat16

CHUNK = 64
GRID_W = 64
RWKV_HEAD = 64
HGRN_EXPAND = 128
N_MOD = 9
RMS_EPS = 1e-6
GN_EPS = 64e-5
FFN_HALF = 0.5
LANES = 128
MOD_ROWS = 16
VMEM_LIMIT_BYTES = 56 * 1024 * 1024


def _cparams(*sem):
    return pltpu.CompilerParams(dimension_semantics=sem, vmem_limit_bytes=VMEM_LIMIT_BYTES)


def _dot(a, b):
    return jnp.dot(a.astype(BF16), b.astype(BF16), preferred_element_type=F32)


def _dot_nt(a, b):
    return lax.dot_general(a.astype(BF16), b.astype(BF16), (((1,), (1,)), ((), ())),
                           preferred_element_type=F32)


def _dot_tn(a, b):
    return jnp.dot(a.T.astype(BF16), b.astype(BF16), preferred_element_type=F32)


def _split2(x):
    hi = x.astype(BF16)
    return hi, (x - hi.astype(F32)).astype(BF16)


def _split3(x):
    x1 = x.astype(BF16)
    d1 = x - x1.astype(F32)
    x2 = d1.astype(BF16)
    x3 = (d1 - x2.astype(F32)).astype(BF16)
    return x1, x2, x3


def _exact_rowmix(mat, x):
    x1, x2, x3 = _split3(x)
    y = jnp.dot(mat, jnp.concatenate([x1, x2, x3], axis=1), preferred_element_type=F32)
    n = x.shape[1]
    return y[:, :n] + y[:, n:2 * n] + y[:, 2 * n:]


def _group_sum(x, bd_ones):
    x1, x2 = _split2(x)
    return (jnp.dot(x1, bd_ones, preferred_element_type=F32)
            + jnp.dot(x2, bd_ones, preferred_element_type=F32))


def _lockstep(chains):
    results = [None] * len(chains)
    active = list(enumerate(chains))
    while active:
        still = []
        for idx, chain in active:
            try:
                next(chain)
                still.append((idx, chain))
            except StopIteration as done:
                results[idx] = done.value
        active = still
    return results


def _mod_kernel(c_ref, w_ref, b_ref, o_ref):
    c = c_ref[...]
    s = c * jax.nn.sigmoid(c)
    o_ref[...] = _dot(s, w_ref[...]) + b_ref[...]


def _modulation(cond, w_mod, b_mod, tn):
    d, n = w_mod.shape
    return pl.pallas_call(
        _mod_kernel,
        out_shape=jax.ShapeDtypeStruct((MOD_ROWS, n), F32),
        grid=(n // tn,),
        in_specs=[pl.BlockSpec((MOD_ROWS, d), lambda j: (0, 0)),
                  pl.BlockSpec((d, tn), lambda j: (0, j)),
                  pl.BlockSpec((1, tn), lambda j: (0, j))],
        out_specs=pl.BlockSpec((MOD_ROWS, tn), lambda j: (0, j)),
        compiler_params=_cparams("arbitrary"),
        name="modulation",
    )(cond, w_mod, b_mod.reshape(1, n))


def _resnorm_kernel(*refs, has_delta, res_scale, gate_idx, mod_idx, emit_x, ctx_tiles):
    it = iter(refs)
    x_ref = next(it)
    xl_ref = next(it) if ctx_tiles is not None else None
    delta_ref = next(it) if has_delta else None
    mod_ref = next(it) if (has_delta or mod_idx is not None) else None
    gain_ref = next(it)
    xo_ref = next(it) if emit_x else None
    h_ref = next(it)
    x = x_ref[...]
    if ctx_tiles is not None:
        x = jnp.where(pl.program_id(0) < ctx_tiles, x, xl_ref[...])
    if has_delta:
        g = mod_ref[gate_idx:gate_idx + 1, :]
        x = x + g * (res_scale * delta_ref[...])
    if emit_x:
        xo_ref[...] = x
    y = x * lax.rsqrt(jnp.mean(x * x, axis=-1, keepdims=True) + RMS_EPS)
    y = y * gain_ref[...]
    if mod_idx is not None:
        sh = mod_ref[mod_idx:mod_idx + 1, :]
        sc = mod_ref[mod_idx + 1:mod_idx + 2, :]
        y = y * (1.0 + sc) + sh
    h_ref[...] = y.astype(h_ref.dtype)


def _resnorm(x, delta, mod3, gain, dims, *, res_scale, gate_idx, mod_idx, emit_x, out_dtype,
             tm, row_off=0, rows=None):
    mc, tc, tl, nlb = dims
    n_ctx_tiles = mc // tm
    tiles_per_lat = tl // tm
    ro = row_off // tm
    split_x = isinstance(x, tuple)
    if split_x:
        assert row_off == 0 and rows is None
        d = x[0].shape[1]
        m_all = x[0].shape[0] + x[1].shape[0]
        in_specs = [pl.BlockSpec((tm, d), lambda i: (jnp.minimum(i, n_ctx_tiles - 1), 0)),
                    pl.BlockSpec((tm, d), lambda i: (jnp.maximum(i - n_ctx_tiles, 0), 0))]
        args = list(x)
    else:
        d = x.shape[1]
        m_all = rows if rows is not None else x.shape[0]
        in_specs = [pl.BlockSpec((tm, d), lambda i: (i + ro, 0))]
        args = [x]

    def mod_row(i):
        gi = i + ro
        return jnp.where(gi < n_ctx_tiles, nlb, (gi - n_ctx_tiles) // tiles_per_lat)

    has_delta = delta is not None
    if has_delta:
        in_specs.append(pl.BlockSpec((tm, d), lambda i: (i + ro, 0)))
        args.append(delta)
    if has_delta or mod_idx is not None:
        in_specs.append(pl.BlockSpec((None, N_MOD, d), lambda i: (mod_row(i), 0, 0)))
        args.append(mod3)
    in_specs.append(pl.BlockSpec((1, d), lambda i: (0, 0)))
    args.append(gain.reshape(1, d))
    out_shape, out_specs = [], []
    if emit_x:
        out_shape.append(jax.ShapeDtypeStruct((m_all, d), F32))
        out_specs.append(pl.BlockSpec((tm, d), lambda i: (i, 0)))
    out_shape.append(jax.ShapeDtypeStruct((m_all, d), out_dtype))
    out_specs.append(pl.BlockSpec((tm, d), lambda i: (i, 0)))
    kern = functools.partial(_resnorm_kernel, has_delta=has_delta, res_scale=res_scale,
                             gate_idx=gate_idx, mod_idx=mod_idx, emit_x=emit_x,
                             ctx_tiles=n_ctx_tiles if split_x else None)
    return pl.pallas_call(
        kern, out_shape=out_shape, grid=(m_all // tm,), in_specs=in_specs, out_specs=out_specs,
        compiler_params=_cparams("arbitrary"), name="resnorm",
    )(*args)


def _permnorm_kernel(x_ref, mod_ref, gain_ref, h_ref, *, mod_idx):
    x = x_ref[...]
    y = x * lax.rsqrt(jnp.mean(x * x, axis=-1, keepdims=True) + RMS_EPS)
    y = y * gain_ref[...]
    sh = mod_ref[mod_idx:mod_idx + 1, :]
    sc = mod_ref[mod_idx + 1:mod_idx + 2, :]
    h_ref[...] = (y * (1.0 + sc) + sh).astype(h_ref.dtype)


def _permnorm(x, mod3, gain, dims, *, mod_idx):
    mc, tc, tl, nlb = dims
    m, d = x.shape
    rows = tl // GRID_W
    assert mc % tl == 0 and rows % 16 == 0
    seq_off = mc // tl
    x3 = x.reshape(m // tl, rows, GRID_W * d)
    return pl.pallas_call(
        functools.partial(_permnorm_kernel, mod_idx=mod_idx),
        out_shape=jax.ShapeDtypeStruct((nlb * tl, d), BF16),
        grid=(nlb, GRID_W),
        in_specs=[pl.BlockSpec((None, rows, d), lambda b, w: (b + seq_off, 0, w)),
                  pl.BlockSpec((None, N_MOD, d), lambda b, w: (b, 0, 0)),
                  pl.BlockSpec((1, d), lambda b, w: (0, 0))],
        out_specs=pl.BlockSpec((rows, d), lambda b, w: (b * GRID_W + w, 0)),
        compiler_params=_cparams("arbitrary", "arbitrary"), name="permnorm",
    )(x3, mod3, gain.reshape(1, d))


def _mm_kernel(a_ref, w_ref, o_ref):
    o_ref[...] = jnp.dot(a_ref[...], w_ref[...], preferred_element_type=F32).astype(o_ref.dtype)


def _matmul(a, w, *, out_dtype, tm, tn, row_off=0, rows=None, name="matmul"):
    k = a.shape[1]
    n = w.shape[1]
    rows = a.shape[0] if rows is None else rows
    ro = row_off // tm
    return pl.pallas_call(
        _mm_kernel,
        out_shape=jax.ShapeDtypeStruct((rows, n), out_dtype),
        grid=(rows // tm, n // tn),
        in_specs=[pl.BlockSpec((tm, k), lambda i, j: (i + ro, 0)),
                  pl.BlockSpec((k, tn), lambda i, j: (0, j))],
        out_specs=pl.BlockSpec((tm, tn), lambda i, j: (i, j)),
        compiler_params=_cparams("arbitrary", "arbitrary"), name=name,
    )(a, w)


def _ffn_kernel(h_ref, wa_ref, wb_ref, wo_ref, o_ref):
    @pl.when(pl.program_id(1) == 0)
    def _():
        o_ref[...] = jnp.zeros_like(o_ref)

    h = h_ref[...]
    a = jnp.dot(h, wa_ref[...], preferred_element_type=F32)
    b = jnp.dot(h, wb_ref[...], preferred_element_type=F32)
    u = (a * jax.nn.sigmoid(a) * b).astype(BF16)
    o_ref[...] += jnp.dot(u, wo_ref[...], preferred_element_type=F32)


def _ffn(h, w_in, w_out, *, tm, tf):
    m, d = h.shape
    dff = w_out.shape[0]
    nf = dff // tf
    return pl.pallas_call(
        _ffn_kernel,
        out_shape=jax.ShapeDtypeStruct((m, d), F32),
        grid=(m // tm, nf),
        in_specs=[pl.BlockSpec((tm, d), lambda i, j: (i, 0)),
                  pl.BlockSpec((d, tf), lambda i, j: (0, j)),
                  pl.BlockSpec((d, tf), lambda i, j: (0, j + nf)),
                  pl.BlockSpec((tf, d), lambda i, j: (j, 0))],
        out_specs=pl.BlockSpec((tm, d), lambda i, j: (i, 0), pipeline_mode=pl.Buffered(1)),
        compiler_params=_cparams("arbitrary", "arbitrary"), name="ffn",
    )(h, w_in, w_in, w_out)


def _merge_kernel(yr_ref, yh_ref, wr_ref, wh_ref, gr_ref, gh_ref, o_ref):
    pr = jnp.dot(yr_ref[...], wr_ref[...], preferred_element_type=F32)
    ph = jnp.dot(yh_ref[...], wh_ref[...], preferred_element_type=F32)
    o_ref[...] = (jax.nn.sigmoid(gr_ref[...]) * pr
                  + jax.nn.sigmoid(gh_ref[...]) * ph).astype(o_ref.dtype)


def _merge(y_r, y_h, w_br, w_bh, z_rest, col_gr, col_gh, *, tm, tn):
    m, kr = y_r.shape
    kh = y_h.shape[1]
    d = w_br.shape[1]
    cgr, cgh = col_gr // tn, col_gh // tn
    return pl.pallas_call(
        _merge_kernel,
        out_shape=jax.ShapeDtypeStruct((m, d), BF16),
        grid=(m // tm, d // tn),
        in_specs=[pl.BlockSpec((tm, kr), lambda i, j: (i, 0)),
                  pl.BlockSpec((tm, kh), lambda i, j: (i, 0)),
                  pl.BlockSpec((kr, tn), lambda i, j: (0, j)),
                  pl.BlockSpec((kh, tn), lambda i, j: (0, j)),
                  pl.BlockSpec((tm, tn), lambda i, j: (i, j + cgr)),
                  pl.BlockSpec((tm, tn), lambda i, j: (i, j + cgh))],
        out_specs=pl.BlockSpec((tm, tn), lambda i, j: (i, j)),
        compiler_params=_cparams("arbitrary", "arbitrary"), name="merge",
    )(y_r, y_h, w_br, w_bh, z_rest, z_rest)


def _log2(n):
    assert n & (n - 1) == 0
    return n.bit_length() - 1


def _head_ones(n, group):
    sh = _log2(group)
    i = lax.broadcasted_iota(jnp.int32, (n, n), 0) >> sh
    j = lax.broadcasted_iota(jnp.int32, (n, n), 1) >> sh
    return jnp.where(i == j, 1.0, 0.0).astype(BF16)


def _rwkv_prep_kernel(z_ref, zp_ref, zn_ref, mup_ref, mun_ref, w0_ref, w2_ref, a0_ref, a2_ref,
                      g2_ref, kk_ref,
                      r_o, k_o, v_o, kkn_o, af_o, ab_o, lwf_o, lwb_o, g_o,
                      *, tm, mc, tc, tl, dr, wl, al, gl):
    row0 = pl.program_id(0) * tm
    in_ctx = row0 < mc
    pos = jnp.where(in_ctx, row0 % tc, (row0 - mc) % tl)
    seqlen = jnp.where(in_ctx, tc, tl)
    keep_prev = jnp.where(pos == 0, 0.0, 1.0)
    keep_next = jnp.where(pos + tm == seqlen, 0.0, 1.0)
    rows = lax.broadcasted_iota(jnp.int32, (tm, 1), 0)

    def shifted(c0, c1):
        z = z_ref[:, c0:c1]
        zp = jnp.where(rows == 0, keep_prev * zp_ref[7:8, c0:c1], pltpu.roll(z, 1, 0))
        zn = jnp.where(rows == tm - 1, keep_next * zn_ref[0:1, c0:c1], pltpu.roll(z, tm - 1, 0))
        return z + mup_ref[:, c0:c1] * (zp - z) + mun_ref[:, c0:c1] * (zn - z)

    bd = _head_ones(LANES, RWKV_HEAD)
    for j in range(dr // LANES):
        c0 = j * LANES
        r_o[:, c0:c0 + LANES] = shifted(c0, c0 + LANES)
        k = shifted(dr + c0, dr + c0 + LANES)
        k_o[:, c0:c0 + LANES] = k
        v_o[:, c0:c0 + LANES] = shifted(2 * dr + c0, 2 * dr + c0 + LANES)
        kk = k * kk_ref[:, c0:c0 + LANES]
        ss = _group_sum(kk * kk, bd)
        kkn_o[:, c0:c0 + LANES] = kk / jnp.maximum(jnp.sqrt(ss), 1e-12)

    c = 3 * dr
    for d, (lw_o, a_o) in enumerate(((lwf_o, af_o), (lwb_o, ab_o))):
        wd = shifted(c + d * wl, c + (d + 1) * wl)
        ad = shifted(c + 2 * wl + d * al, c + 2 * wl + (d + 1) * al)
        wpre = w0_ref[d:d + 1, :] + _dot(jnp.tanh(wd), w2_ref[d])
        nx = -wpre
        softplus = jnp.maximum(nx, 0.0) + jnp.log1p(jnp.exp(-jnp.abs(nx)))
        w_log = -softplus - 0.5
        lw_o[...] = -jnp.exp(w_log)
        a_o[...] = jax.nn.sigmoid(a0_ref[d:d + 1, :] + _dot(ad, a2_ref[d]))
    gd = shifted(c + 2 * wl + 2 * al, c + 2 * wl + 2 * al + gl)
    g_o[...] = _dot(jax.nn.sigmoid(gd), g2_ref[...])


def _rwkv_prep(z_rest, p, dims, *, tm, zr_cols, dr, wl, al, gl):
    mc, tc, tl, nlb = dims
    m = z_rest.shape[0]
    nblk8 = m // 8
    t8 = tm // 8
    full = lambda shape: pl.BlockSpec(shape, lambda i: (0,) * len(shape))
    outs = [jax.ShapeDtypeStruct((m, dr), F32)] * 9
    kern = functools.partial(_rwkv_prep_kernel, tm=tm, mc=mc, tc=tc, tl=tl, dr=dr, wl=wl, al=al,
                             gl=gl)
    return pl.pallas_call(
        kern, out_shape=outs, grid=(m // tm,),
        in_specs=[pl.BlockSpec((tm, zr_cols), lambda i: (i, 0)),
                  pl.BlockSpec((8, zr_cols), lambda i: (jnp.maximum(i * t8 - 1, 0), 0)),
                  pl.BlockSpec((8, zr_cols), lambda i: (jnp.minimum((i + 1) * t8, nblk8 - 1), 0)),
                  full((1, zr_cols)), full((1, zr_cols)),
                  full((2, dr)), full((2, wl, dr)), full((2, dr)), full((2, al, dr)),
                  full((gl, dr)), full((1, dr))],
        out_specs=[pl.BlockSpec((tm, dr), lambda i: (i, 0))] * 9,
        compiler_params=_cparams("arbitrary"), name="rwkv_prep",
    )(z_rest, z_rest, z_rest, p["mu_prev"], p["mu_next"], p["w0"], p["w2"], p["a0"], p["a2"],
      p["g2"], p["k_k"])


def _stack_heads(x, lo):
    return jnp.concatenate([jnp.where(lo, x, 0.0), jnp.where(lo, 0.0, x)], axis=0)


SUB_BLOCK = 16
assert CHUNK // SUB_BLOCK == 4


def _rwkv_masks(rev):
    c = CHUNK
    n2 = 2 * c
    ti = lax.broadcasted_iota(jnp.int32, (c, c), 0)
    tj = lax.broadcasted_iota(jnp.int32, (c, c), 1)
    cum = jnp.where((tj >= ti) if rev else (tj <= ti), 1.0, 0.0).astype(BF16)
    lo = lax.broadcasted_iota(jnp.int32, (c, LANES), 1) < RWKV_HEAD
    ii = lax.broadcasted_iota(jnp.int32, (n2, n2), 0)
    jj = lax.broadcasted_iota(jnp.int32, (n2, n2), 1)
    im, jm = ii & (c - 1), jj & (c - 1)
    strict = (jm > im) if rev else (jm < im)
    incl = (jm >= im) if rev else (jm <= im)
    eye = jnp.where(ii == jj, 1.0, 0.0)
    sub = (ii >> _log2(SUB_BLOCK)) == (jj >> _log2(SUB_BLOCK))
    return cum, lo, strict, incl, eye, sub


def _rwkv_chunk(r, k, v, kk, a, lw, k_a, s, rev, masks):
    c = CHUNK
    n2 = 2 * c
    cum, lo, strict, incl, eye, sub = masks
    bc = _exact_rowmix(cum, lw)
    yield
    bex = bc - lw
    btot = bc[0:1, :] if rev else bc[c - 1:c, :]
    kd = k * (1.0 + (a - 1.0) * k_a)
    bv = kk * a
    e_neg = jnp.exp(-bc)
    e_tot = jnp.exp(btot - bc)
    at = _stack_heads(-kk * jnp.exp(bex), lo).astype(BF16)
    rt_f = _stack_heads(r * jnp.exp(bc), lo)
    kh = _stack_heads(kd * e_neg, lo).astype(BF16)
    bh = _stack_heads(bv * e_neg, lo).astype(BF16)
    vb_f = _stack_heads(v, lo)
    vb = vb_f.astype(BF16)
    kc = _stack_heads(kd * e_tot, lo).astype(BF16)
    bcc = _stack_heads(bv * e_tot, lo).astype(BF16)

    gram = _dot_nt(jnp.concatenate([at, rt_f.astype(BF16)], axis=0),
                   jnp.concatenate([bh, kh], axis=0))
    yield
    nmat = jnp.where(strict, gram[:n2, :n2], 0.0)
    a_ka = jnp.where(strict, gram[:n2, n2:], 0.0)
    a_rb = jnp.where(incl, gram[n2:, :n2], 0.0)
    a_rk = jnp.where(incl, gram[n2:, n2:], 0.0)
    n_d = jnp.where(sub, nmat, 0.0)
    t_d = eye + n_d
    pw = n_d
    for _ in range(_log2(SUB_BLOCK) - 1):
        pw = _dot(pw, pw)
        yield
        t_d = t_d + _dot(t_d, pw)
    x1 = _dot(a_ka, vb)
    yield
    xm = _dot(t_d, jnp.where(sub, 0.0, nmat))
    yield
    xm2 = _dot(xm, xm)
    yield
    ym = eye + xm + xm2 + _dot(xm, xm2)
    yield
    tm_ = _dot(ym, t_d)
    yield
    aw = _dot(tm_, jnp.concatenate([at, x1.astype(BF16)], axis=1))
    yield
    ah, w = aw[:, :n2], aw[:, n2:]
    rhs = jnp.concatenate(
        [jnp.concatenate([vb, jnp.zeros_like(vb)], axis=1),
         jnp.concatenate([w.astype(BF16), ah.astype(BF16)], axis=1)], axis=0)
    qr = _dot(jnp.concatenate([a_rk, a_rb], axis=1), rhs)
    q = qr[:, :n2]
    rr = rt_f + qr[:, n2:]
    mm = _dot_tn(ah, bcc)
    nn = _dot_tn(jnp.concatenate([vb_f, w], axis=0), jnp.concatenate([kc, bcc], axis=0))
    yield
    o_bd = _dot_nt(rr, s) + q
    o = o_bd[:c, :] + o_bd[c:, :]
    s_new = s * jnp.exp(btot) + _dot(s, mm) + nn
    return o, s_new


def _rwkv_scan_kernel(*refs, has_init, emit_final, nchunk, npairs):
    it = iter(refs)
    fwd = [next(it) for _ in range(6)]
    bwd = [next(it) for _ in range(6)]
    ka_ref = next(it)
    s0_ref = next(it) if has_init else None
    of_ref = next(it)
    ob_ref = next(it)
    sfin_ref = next(it) if emit_final else None
    s_scr = next(it)
    ci = pl.program_id(2)

    @pl.when(ci == 0)
    def _():
        if has_init:
            s_scr[...] = s0_ref[...]
        else:
            s_scr[...] = jnp.zeros_like(s_scr)

    chains, sinks = [], []
    for d, (ins, o_ref) in enumerate(((fwd, of_ref), (bwd, ob_ref))):
        masks = _rwkv_masks(rev=(d == 1))
        for hp in range(npairs):
            sl = slice(hp * LANES, (hp + 1) * LANES)
            r, k, v, kk, a, lw = (x[:, sl] for x in ins)
            chains.append(_rwkv_chunk(r, k, v, kk, a, lw, ka_ref[:, sl], s_scr[d, hp],
                                      rev=(d == 1), masks=masks))
            sinks.append((o_ref, sl, d, hp))
    for (o, s_new), (o_ref, sl, d, hp) in zip(_lockstep(chains), sinks):
        o_ref[:, sl] = o
        s_scr[d, hp] = s_new

    if emit_final:
        @pl.when(ci == nchunk - 1)
        def _():
            sfin_ref[...] = s_scr[...]


def _rwkv_scan(pre, k_a, s0_bd, *, row_off, nb, t, emit_final, npairs):
    m, dr = pre["r"].shape
    ngroup = dr // (LANES * npairs)
    nchunk = t // CHUNK
    rb = row_off // CHUNK
    has_init = s0_bd is not None
    fmap = lambda b, h, c: (rb + b * nchunk + c, h)
    bmap = lambda b, h, c: (rb + b * nchunk + (nchunk - 1 - c), h)
    blk = (CHUNK, LANES * npairs)
    in_specs = ([pl.BlockSpec(blk, fmap)] * 6 + [pl.BlockSpec(blk, bmap)] * 6
                + [pl.BlockSpec((1, LANES * npairs), lambda b, h, c: (0, h))])
    args = [pre["r"], pre["k"], pre["v"], pre["kk"], pre["a_f"], pre["lw_f"],
            pre["r"], pre["k"], pre["v"], pre["kk"], pre["a_b"], pre["lw_b"], k_a]
    sblk = pl.BlockSpec((None, 2, npairs, LANES, LANES), lambda b, h, c: (b, 0, h, 0, 0))
    if has_init:
        in_specs.append(sblk)
        args.append(s0_bd)
    rows = nb * t
    out_shape = [jax.ShapeDtypeStruct((rows, dr), F32)] * 2
    out_specs = [pl.BlockSpec(blk, lambda b, h, c: (b * nchunk + c, h)),
                 pl.BlockSpec(blk, lambda b, h, c: (b * nchunk + (nchunk - 1 - c), h))]
    if emit_final:
        out_shape.append(jax.ShapeDtypeStruct((nb, 2, dr // LANES, LANES, LANES), F32))
        out_specs.append(sblk)
    kern = functools.partial(_rwkv_scan_kernel, has_init=has_init, emit_final=emit_final,
                             nchunk=nchunk, npairs=npairs)
    return pl.pallas_call(
        kern, out_shape=out_shape, grid=(nb, ngroup, nchunk), in_specs=in_specs,
        out_specs=out_specs, scratch_shapes=[pltpu.VMEM((2, npairs, LANES, LANES), F32)],
        compiler_params=_cparams("arbitrary", "arbitrary", "arbitrary"), name="rwkv_scan",
    )(*args)


def _rwkv_post_kernel(of_ref, ob_ref, r_ref, k_ref, v_ref, af_ref, ab_ref, g_ref,
                      ka_ref, rk_ref, lnw_ref, lnb_ref, y_ref, *, dr):
    bd = _head_ones(LANES, RWKV_HEAD)
    inv = 1.0 / RWKV_HEAD
    for j in range(dr // LANES):
        sl = slice(j * LANES, (j + 1) * LANES)
        o = of_ref[:, sl] + ob_ref[:, sl]
        mean = _group_sum(o, bd) * inv
        oc = o - mean
        var = _group_sum(oc * oc, bd) * inv
        on = oc * lax.rsqrt(var + GN_EPS) * lnw_ref[:, sl] + lnb_ref[:, sl]
        r, k, v = r_ref[:, sl], k_ref[:, sl], v_ref[:, sl]
        k_a, r_k = ka_ref[:, sl], rk_ref[:, sl]
        bonus = 0.0
        for a_ref in (af_ref, ab_ref):
            kd = k * (1.0 + (a_ref[:, sl] - 1.0) * k_a)
            bonus = bonus + _group_sum(r * kd * r_k, bd) * v
        y_ref[:, sl] = ((on + bonus) * g_ref[:, sl]).astype(y_ref.dtype)


def _rwkv_post(o_f, o_b, pre, p, *, tm, row_off):
    rows, dr = o_f.shape
    ro = row_off // tm
    own = pl.BlockSpec((tm, dr), lambda i: (i, 0))
    glob = pl.BlockSpec((tm, dr), lambda i: (i + ro, 0))
    par = pl.BlockSpec((1, dr), lambda i: (0, 0))
    return pl.pallas_call(
        functools.partial(_rwkv_post_kernel, dr=dr),
        out_shape=jax.ShapeDtypeStruct((rows, dr), BF16),
        grid=(rows // tm,),
        in_specs=[own, own] + [glob] * 6 + [par] * 4,
        out_specs=own,
        compiler_params=_cparams("arbitrary"), name="rwkv_post",
    )(o_f, o_b, pre["r"], pre["k"], pre["v"], pre["a_f"], pre["a_b"], pre["g"],
      p["k_a"], p["r_k"], p["ln_w"], p["ln_b"])


HGRN_LEVELS = 6


def _hgrn_mix_matrix(rev):
    c = CHUNK
    t = lax.broadcasted_iota(jnp.int32, (c, c), 0)
    i = lax.broadcasted_iota(jnp.int32, (c, c), 1)
    mats = []
    cum = (i >= t) if rev else (i <= t)
    mats.append(jnp.where(cum, 1.0, 0.0))
    for lvl in range(HGRN_LEVELS):
        half = 1 << lvl
        base = (t >> (lvl + 1)) << (lvl + 1)
        if rev:
            mid = base + half
            ref_ = i >= mid
        else:
            mid = base + half - 1
            ref_ = i <= mid
        mats.append(jnp.where(cum, 1.0, 0.0) - jnp.where(ref_, 1.0, 0.0))
    return jnp.concatenate(mats, axis=0).astype(BF16)


def _hgrn_masks(rev):
    c = CHUNK
    ti = lax.broadcasted_iota(jnp.int32, (c, c), 0)
    tj = lax.broadcasted_iota(jnp.int32, (c, c), 1)
    late, early = (tj, ti) if rev else (ti, tj)
    sels = [((late >> lvl) == (early >> lvl) + 1) & (((early >> lvl) & 1) == 0)
            for lvl in range(HGRN_LEVELS)]
    return _hgrn_mix_matrix(rev), ti == tj, sels


def _hgrn_chunk(q_pre, f_pre, v, lb, st, rev, masks):
    c = CHUNK
    mix, diag, sels = masks
    q = q_pre * jax.nn.sigmoid(q_pre)
    f = lb + (1.0 - lb) * jax.nn.sigmoid(f_pre)
    kf = 1.0 - f
    g = jnp.log(f)
    ex = _exact_rowmix(mix, g)
    yield
    b = ex[:c, :]
    btot = b[0:1, :] if rev else b[c - 1:c, :]
    attn = jnp.where(diag, _dot_nt(q, kf), 0.0)
    for lvl in range(HGRN_LEVELS):
        dq = ex[(lvl + 1) * c:(lvl + 2) * c, :]
        qs = q * jnp.exp(jnp.minimum(dq, 0.0))
        ks = kf * jnp.exp(jnp.minimum(-dq, 0.0))
        attn = attn + jnp.where(sels[lvl], _dot_nt(qs, ks), 0.0)
    yield
    o = _dot_nt(q * jnp.exp(b), st) + _dot(attn, v)
    st_new = st * jnp.exp(btot) + _dot_tn(v, kf * jnp.exp(btot - b))
    return o, st_new


def _hgrn_scan_kernel(*refs, has_init, emit_final, nchunk, nheads):
    it = iter(refs)
    qf_ref, ff_ref, vf_ref = next(it), next(it), next(it)
    qb_ref, fb_ref, vb_ref = next(it), next(it), next(it)
    lb_ref = next(it)
    s0_ref = next(it) if has_init else None
    of_ref, ob_ref = next(it), next(it)
    sfin_ref = next(it) if emit_final else None
    s_scr = next(it)
    ci = pl.program_id(2)

    @pl.when(ci == 0)
    def _():
        if has_init:
            for d in range(2):
                for h in range(nheads):
                    s_scr[d, h] = s0_ref[d, h].T
        else:
            s_scr[...] = jnp.zeros_like(s_scr)

    chains, sinks = [], []
    for d, (ins, o_ref) in enumerate((((qf_ref, ff_ref, vf_ref), of_ref),
                                      ((qb_ref, fb_ref, vb_ref), ob_ref))):
        masks = _hgrn_masks(rev=(d == 1))
        for h in range(nheads):
            sl = slice(h * LANES, (h + 1) * LANES)
            qp, fp, v = (x[:, sl] for x in ins)
            chains.append(_hgrn_chunk(qp, fp, v, lb_ref[:, sl], s_scr[d, h], rev=(d == 1),
                                      masks=masks))
            sinks.append((o_ref, sl, d, h))
    for (o, s_new), (o_ref, sl, d, h) in zip(_lockstep(chains), sinks):
        o_ref[:, sl] = o
        s_scr[d, h] = s_new

    if emit_final:
        @pl.when(ci == nchunk - 1)
        def _():
            for d in range(2):
                for h in range(nheads):
                    sfin_ref[d, h] = s_scr[d, h].T


def _hgrn_scan(zhs, lb, s0, *, nb, t, dg, emit_final, nheads):
    ng = dg // (LANES * nheads)
    nchunk = t // CHUNK
    has_init = s0 is not None
    blk = (CHUNK, LANES * nheads)

    def spec(group, rev):
        if rev:
            return pl.BlockSpec(blk, lambda b, h, c: (b * nchunk + nchunk - 1 - c, group * ng + h))
        return pl.BlockSpec(blk, lambda b, h, c: (b * nchunk + c, group * ng + h))

    in_specs = [spec(0, False), spec(1, False), spec(3, False),
                spec(0, True), spec(2, True), spec(3, True),
                pl.BlockSpec((1, LANES * nheads), lambda b, h, c: (0, h))]
    args = [zhs] * 6 + [lb]
    sblk = pl.BlockSpec((None, 2, nheads, LANES, LANES), lambda b, h, c: (b, 0, h, 0, 0))
    if has_init:
        in_specs.append(sblk)
        args.append(s0)
    out_shape = [jax.ShapeDtypeStruct((nb * t, dg), F32)] * 2
    out_specs = [pl.BlockSpec(blk, lambda b, h, c: (b * nchunk + c, h)),
                 pl.BlockSpec(blk, lambda b, h, c: (b * nchunk + nchunk - 1 - c, h))]
    if emit_final:
        out_shape.append(jax.ShapeDtypeStruct((nb, 2, dg // LANES, LANES, LANES), F32))
        out_specs.append(sblk)
    kern = functools.partial(_hgrn_scan_kernel, has_init=has_init, emit_final=emit_final,
                             nchunk=nchunk, nheads=nheads)
    return pl.pallas_call(
        kern, out_shape=out_shape, grid=(nb, ng, nchunk), in_specs=in_specs, out_specs=out_specs,
        scratch_shapes=[pltpu.VMEM((2, nheads, LANES, LANES), F32)],
        compiler_params=_cparams("arbitrary", "arbitrary", "arbitrary"), name="hgrn_scan",
    )(*args)


def _hgrn_post_kernel(of_ref, ob_ref, gate_ref, gain_ref, y_ref, *, dg):
    gain = gain_ref[...]
    for j in range(dg // LANES):
        sl = slice(j * LANES, (j + 1) * LANES)
        o = of_ref[:, sl] + ob_ref[:, sl]
        on = o * lax.rsqrt(jnp.mean(o * o, axis=-1, keepdims=True) + RMS_EPS) * gain
        gt = gate_ref[:, sl]
        y_ref[:, sl] = (on * (gt * jax.nn.sigmoid(gt))).astype(y_ref.dtype)


def _hgrn_post(o_f, o_b, z_rest, gain, *, gate_col, row_off, nb, t, permuted):
    rows_all, dg = o_f.shape
    kern = functools.partial(_hgrn_post_kernel, dg=dg)
    gcb = gate_col // dg
    par = pl.BlockSpec((1, LANES), lambda *_: (0, 0))
    if permuted:
        nrow = t // GRID_W
        tm = GRID_W
        o3f = o_f.reshape(nb, GRID_W, nrow * dg)
        o3b = o_b.reshape(nb, GRID_W, nrow * dg)
        ospec = pl.BlockSpec((None, GRID_W, dg), lambda b, r: (b, 0, r))
        rb = row_off // tm
        return pl.pallas_call(
            kern, out_shape=jax.ShapeDtypeStruct((rows_all, dg), BF16), grid=(nb, nrow),
            in_specs=[ospec, ospec,
                      pl.BlockSpec((tm, dg), lambda b, r: (rb + b * nrow + r, gcb)), par],
            out_specs=pl.BlockSpec((tm, dg), lambda b, r: (b * nrow + r, 0)),
            compiler_params=_cparams("arbitrary", "arbitrary"), name="hgrn_post_perm",
        )(o3f, o3b, z_rest, gain)
    tm = min(256, rows_all)
    rb = row_off // tm
    ospec = pl.BlockSpec((tm, dg), lambda i: (i, 0))
    return pl.pallas_call(
        kern, out_shape=jax.ShapeDtypeStruct((rows_all, dg), BF16), grid=(rows_all // tm,),
        in_specs=[ospec, ospec, pl.BlockSpec((tm, dg), lambda i: (rb + i, gcb)), par],
        out_specs=ospec,
        compiler_params=_cparams("arbitrary"), name="hgrn_post",
    )(o_f, o_b, z_rest, gain)


def _largest_divisor(n, candidates):
    for cand in candidates:
        if n % cand == 0:
            return cand
    raise ValueError(f"no tile in {candidates} divides {n}")


def _pad_to(x, axis, size):
    pad = size - x.shape[axis]
    if pad == 0:
        return x
    widths = [(0, 0)] * x.ndim
    widths[axis] = (0, pad)
    return jnp.pad(x, widths)


def _pair_state_to_bd(s):
    b, two, h, n, _ = s.shape
    s = s.reshape(b, two, h // 2, 2, n, n)
    z = jnp.zeros_like(s[:, :, :, 0])
    top = jnp.concatenate([s[:, :, :, 0], z], axis=-1)
    bot = jnp.concatenate([z, s[:, :, :, 1]], axis=-1)
    return jnp.concatenate([top, bot], axis=-2)


def _bd_to_pair_state(sbd):
    b, two, hp, n2, _ = sbd.shape
    n = n2 // 2
    s0 = sbd[:, :, :, :n, :n]
    s1 = sbd[:, :, :, n:, n:]
    return jnp.stack([s0, s1], axis=3).reshape(b, two, hp * 2, n, n)


def kernel(x_prompt, x_sample, state_rwkv, state_hgrn, c, c_ctx, w_mod, b_mod, norm_ffn1, norm_mix, norm_ffn2, ffn1_w_in, ffn1_w_out, ffn2_w_in, ffn2_w_out, w_in, rwkv_mu_prev, rwkv_mu_next, rwkv_w0, rwkv_w2, rwkv_a0, rwkv_a2, rwkv_g2, rwkv_k_k, rwkv_k_a, rwkv_r_k, rwkv_ln_w, rwkv_ln_b, hgrn_lb_logits, hgrn_norm, w_branch_rwkv, w_branch_hgrn, w_out, norm_final):
    nbc, tc, d = x_prompt.shape
    nbl, tl, _ = x_sample.shape
    depth = w_mod.shape[0]
    mc, ml = nbc * tc, nbl * tl
    m = mc + ml
    dims = (mc, tc, tl, nbl)
    dr = rwkv_w0.shape[-1]
    dg = w_branch_hgrn.shape[1]
    dff = ffn1_w_out.shape[1]
    wl, al, gl = rwkv_w2.shape[2], rwkv_a2.shape[2], rwkv_g2.shape[1]
    glp = -(-gl // LANES) * LANES
    rw_cols = 3 * dr + 2 * wl + 2 * al + gl
    zr_cols = 3 * dr + 2 * wl + 2 * al + glp
    assert nbl < MOD_ROWS and tl % GRID_W == 0 and tl // GRID_W == CHUNK and tc % CHUNK == 0

    tm_norm = min(256, tc)
    tm_mm = min(512, mc)
    tm_proj = min(1024, mc)
    tn_of = lambda n: _largest_divisor(n, (1024, 512, 256, 128))
    tf = _largest_divisor(dff, (256, 128))

    lb_all = jnp.cumsum(jax.nn.softmax(hgrn_lb_logits.astype(F32), axis=0), axis=0)
    cond = jnp.concatenate([c, c_ctx[None, :], jnp.zeros((MOD_ROWS - nbl - 1, d), F32)], axis=0)

    x = (x_prompt.reshape(mc, d), x_sample.reshape(ml, d))
    new_r, new_h = [], []
    for l in range(depth):
        mod3 = _modulation(cond, w_mod[l], b_mod[l], tn_of(N_MOD * d)).reshape(MOD_ROWS, N_MOD, d)

        h = _resnorm(x, None, mod3, norm_ffn1[l], dims, res_scale=None, gate_idx=None, mod_idx=0,
                     emit_x=False, out_dtype=BF16, tm=tm_norm)[0]
        delta = _ffn(h, ffn1_w_in[l].astype(BF16), ffn1_w_out[l].astype(BF16), tm=tm_proj, tf=tf)

        x, h = _resnorm(x, delta, mod3, norm_mix[l], dims, res_scale=FFN_HALF, gate_idx=2,
                        mod_idx=3, emit_x=True, out_dtype=BF16, tm=tm_norm)
        h_perm = _permnorm(x, mod3, norm_mix[l], dims, mod_idx=3)
        wi = w_in[l]
        hg0 = rw_cols
        w_r = _pad_to(wi[:, :rw_cols], 1, zr_cols).astype(BF16)
        w_hs = wi[:, hg0:hg0 + 4 * dg].astype(BF16)
        w_gates = wi[:, hg0 + 4 * dg:].astype(BF16)
        col_gate, col_gr, col_gh = 0, dg, dg + d
        z_r = _matmul(h, w_r, out_dtype=F32, tm=tm_proj, tn=tn_of(zr_cols), name="proj_rwkv")
        z_g = _matmul(h, w_gates, out_dtype=F32, tm=tm_proj, tn=tn_of(dg), name="proj_gates")
        zhs_c = _matmul(h, w_hs, out_dtype=F32, tm=tm_proj, tn=tn_of(dg), rows=mc,
                        name="proj_hs_ctx")
        zhs_l = _matmul(h_perm, w_hs, out_dtype=F32, tm=tm_proj, tn=tn_of(dg),
                        name="proj_hs_lat")

        p = {
            "mu_prev": _pad_to(rwkv_mu_prev[l][None, :], 1, zr_cols),
            "mu_next": _pad_to(rwkv_mu_next[l][None, :], 1, zr_cols),
            "w0": rwkv_w0[l], "w2": rwkv_w2[l].astype(BF16),
            "a0": rwkv_a0[l], "a2": rwkv_a2[l].astype(BF16),
            "g2": _pad_to(rwkv_g2[l], 0, glp).astype(BF16),
            "k_k": rwkv_k_k[l][None, :], "k_a": rwkv_k_a[l][None, :],
            "r_k": rwkv_r_k[l].reshape(1, dr),
            "ln_w": rwkv_ln_w[l][None, :], "ln_b": rwkv_ln_b[l][None, :],
        }
        names = ("r", "k", "v", "kk", "a_f", "a_b", "lw_f", "lw_b", "g")
        pre = dict(zip(names, _rwkv_prep(z_r, p, dims, tm=min(128, tc), zr_cols=zr_cols, dr=dr,
                                         wl=wl, al=al, gl=glp)))
        npairs = _largest_divisor(dr // LANES, (4, 2, 1))
        of_c, ob_c, sr_c = _rwkv_scan(pre, p["k_a"], None, row_off=0, nb=nbc, t=tc,
                                      emit_final=True, npairs=npairs)
        of_l, ob_l = _rwkv_scan(pre, p["k_a"], _pair_state_to_bd(state_rwkv[:, l].astype(F32)),
                                row_off=mc, nb=nbl, t=tl, emit_final=False, npairs=npairs)
        yr_c = _rwkv_post(of_c, ob_c, pre, p, tm=min(256, tc), row_off=0)
        yr_l = _rwkv_post(of_l, ob_l, pre, p, tm=min(256, tc), row_off=mc)
        y_r = jnp.concatenate([yr_c, yr_l], axis=0)
        new_r.append(_bd_to_pair_state(sr_c))

        lb = lb_all[l][None, :]
        gain_h = hgrn_norm[l][None, :]
        nheads = _largest_divisor(dg // LANES, (8, 4, 2, 1))
        hf_c, hb_c, sh_c = _hgrn_scan(zhs_c, lb, None, nb=nbc, t=tc, dg=dg, emit_final=True,
                                      nheads=nheads)
        hf_l, hb_l = _hgrn_scan(zhs_l, lb, state_hgrn[:, l].astype(F32), nb=nbl, t=tl, dg=dg,
                                emit_final=False, nheads=nheads)
        yh_c = _hgrn_post(hf_c, hb_c, z_g, gain_h, gate_col=col_gate, row_off=0, nb=nbc, t=tc,
                          permuted=False)
        yh_l = _hgrn_post(hf_l, hb_l, z_g, gain_h, gate_col=col_gate, row_off=mc, nb=nbl, t=tl,
                          permuted=True)
        y_h = jnp.concatenate([yh_c, yh_l], axis=0)
        new_h.append(sh_c)

        merged = _merge(y_r, y_h, w_branch_rwkv[l].astype(BF16), w_branch_hgrn[l].astype(BF16),
                        z_g, col_gr, col_gh, tm=tm_mm, tn=tn_of(dg))
        delta = _matmul(merged, w_out[l].astype(BF16), out_dtype=F32, tm=tm_proj, tn=tn_of(d),
                        name="mix_out")

        x, h = _resnorm(x, delta, mod3, norm_ffn2[l], dims, res_scale=1.0, gate_idx=5, mod_idx=6,
                        emit_x=True, out_dtype=BF16, tm=tm_norm)
        delta = _ffn(h, ffn2_w_in[l].astype(BF16), ffn2_w_out[l].astype(BF16), tm=tm_proj, tf=tf)
        last = l == depth - 1
        if not last:
            x = _resnorm(x, delta, mod3, norm_ffn2[l], dims, res_scale=FFN_HALF, gate_idx=8,
                         mod_idx=None, emit_x=True, out_dtype=F32, tm=tm_norm)[0]

    y_c = _resnorm(x, delta, mod3, norm_final, dims, res_scale=FFN_HALF, gate_idx=8, mod_idx=None,
                   emit_x=False, out_dtype=F32, tm=tm_norm, row_off=0, rows=mc)[0]
    y_l = _resnorm(x, delta, mod3, norm_final, dims, res_scale=FFN_HALF, gate_idx=8, mod_idx=None,
                   emit_x=False, out_dtype=F32, tm=tm_norm, row_off=mc, rows=ml)[0]
    return (y_c.reshape(nbc, tc, d), y_l.reshape(nbl, tl, d),
            jnp.stack(new_r, axis=1), jnp.stack(new_h, axis=1))
```

```python
import functools

import jax
import jax.numpy as jnp
from jax import lax
from jax.experimental import pallas as pl
from jax.experimental.pallas import tpu as pltpu

F32 = jnp.float32
BF16 = jnp.bfloat16

CHUNK = 64
GRID_W = 64
RWKV_HEAD = 64
HGRN_EXPAND = 128
N_MOD = 9
RMS_EPS = 1e-6
GN_EPS = 64e-5
FFN_HALF = 0.5
LANES = 128
MOD_ROWS = 16
VMEM_LIMIT_BYTES = 56 * 1024 * 1024


def _cparams(*sem):
    return pltpu.CompilerParams(dimension_semantics=sem, vmem_limit_bytes=VMEM_LIMIT_BYTES)


def _dot(a, b):
    return jnp.dot(a.astype(BF16), b.astype(BF16), preferred_element_type=F32)


def _dot_nt(a, b):
    return lax.dot_general(a.astype(BF16), b.astype(BF16), (((1,), (1,)), ((), ())),
                           preferred_element_type=F32)


def _dot_tn(a, b):
    return jnp.dot(a.T.astype(BF16), b.astype(BF16), preferred_element_type=F32)


def _split2(x):
    hi = x.astype(BF16)
    return hi, (x - hi.astype(F32)).astype(BF16)


def _split3(x):
    x1 = x.astype(BF16)
    d1 = x - x1.astype(F32)
    x2 = d1.astype(BF16)
    x3 = (d1 - x2.astype(F32)).astype(BF16)
    return x1, x2, x3


def _exact_rowmix(mat, x):
    x1, x2, x3 = _split3(x)
    y = jnp.dot(mat, jnp.concatenate([x1, x2, x3], axis=1), preferred_element_type=F32)
    n = x.shape[1]
    return y[:, :n] + y[:, n:2 * n] + y[:, 2 * n:]


def _group_sum(x, bd_ones):
    x1, x2 = _split2(x)
    return (jnp.dot(x1, bd_ones, preferred_element_type=F32)
            + jnp.dot(x2, bd_ones, preferred_element_type=F32))


def _lockstep(chains):
    results = [None] * len(chains)
    active = list(enumerate(chains))
    while active:
        still = []
        for idx, chain in active:
            try:
                next(chain)
                still.append((idx, chain))
            except StopIteration as done:
                results[idx] = done.value
        active = still
    return results


def _mod_kernel(c_ref, w_ref, b_ref, o_ref):
    c = c_ref[...]
    s = c * jax.nn.sigmoid(c)
    o_ref[...] = _dot(s, w_ref[...]) + b_ref[...]


def _modulation(cond, w_mod, b_mod, tn):
    d, n = w_mod.shape
    return pl.pallas_call(
        _mod_kernel,
        out_shape=jax.ShapeDtypeStruct((MOD_ROWS, n), F32),
        grid=(n // tn,),
        in_specs=[pl.BlockSpec((MOD_ROWS, d), lambda j: (0, 0)),
                  pl.BlockSpec((d, tn), lambda j: (0, j)),
                  pl.BlockSpec((1, tn), lambda j: (0, j))],
        out_specs=pl.BlockSpec((MOD_ROWS, tn), lambda j: (0, j)),
        compiler_params=_cparams("arbitrary"),
        name="modulation",
    )(cond, w_mod, b_mod.reshape(1, n))


def _resnorm_kernel(*refs, has_delta, res_scale, gate_idx, mod_idx, emit_x, ctx_tiles):
    it = iter(refs)
    x_ref = next(it)
    xl_ref = next(it) if ctx_tiles is not None else None
    delta_ref = next(it) if has_delta else None
    mod_ref = next(it) if (has_delta or mod_idx is not None) else None
    gain_ref = next(it)
    xo_ref = next(it) if emit_x else None
    h_ref = next(it)
    x = x_ref[...]
    if ctx_tiles is not None:
        x = jnp.where(pl.program_id(0) < ctx_tiles, x, xl_ref[...])
    if has_delta:
        g = mod_ref[gate_idx:gate_idx + 1, :]
        x = x + g * (res_scale * delta_ref[...])
    if emit_x:
        xo_ref[...] = x
    y = x * lax.rsqrt(jnp.mean(x * x, axis=-1, keepdims=True) + RMS_EPS)
    y = y * gain_ref[...]
    if mod_idx is not None:
        sh = mod_ref[mod_idx:mod_idx + 1, :]
        sc = mod_ref[mod_idx + 1:mod_idx + 2, :]
        y = y * (1.0 + sc) + sh
    h_ref[...] = y.astype(h_ref.dtype)


def _resnorm(x, delta, mod3, gain, dims, *, res_scale, gate_idx, mod_idx, emit_x, out_dtype,
             tm, row_off=0, rows=None):
    mc, tc, tl, nlb = dims
    n_ctx_tiles = mc // tm
    tiles_per_lat = tl // tm
    ro = row_off // tm
    split_x = isinstance(x, tuple)
    if split_x:
        assert row_off == 0 and rows is None
        d = x[0].shape[1]
        m_all = x[0].shape[0] + x[1].shape[0]
        in_specs = [pl.BlockSpec((tm, d), lambda i: (jnp.minimum(i, n_ctx_tiles - 1), 0)),
                    pl.BlockSpec((tm, d), lambda i: (jnp.maximum(i - n_ctx_tiles, 0), 0))]
        args = list(x)
    else:
        d = x.shape[1]
        m_all = rows if rows is not None else x.shape[0]
        in_specs = [pl.BlockSpec((tm, d), lambda i: (i + ro, 0))]
        args = [x]

    def mod_row(i):
        gi = i + ro
        return jnp.where(gi < n_ctx_tiles, nlb, (gi - n_ctx_tiles) // tiles_per_lat)

    has_delta = delta is not None
    if has_delta:
        in_specs.append(pl.BlockSpec((tm, d), lambda i: (i + ro, 0)))
        args.append(delta)
    if has_delta or mod_idx is not None:
        in_specs.append(pl.BlockSpec((None, N_MOD, d), lambda i: (mod_row(i), 0, 0)))
        args.append(mod3)
    in_specs.append(pl.BlockSpec((1, d), lambda i: (0, 0)))
    args.append(gain.reshape(1, d))
    out_shape, out_specs = [], []
    if emit_x:
        out_shape.append(jax.ShapeDtypeStruct((m_all, d), F32))
        out_specs.append(pl.BlockSpec((tm, d), lambda i: (i, 0)))
    out_shape.append(jax.ShapeDtypeStruct((m_all, d), out_dtype))
    out_specs.append(pl.BlockSpec((tm, d), lambda i: (i, 0)))
    kern = functools.partial(_resnorm_kernel, has_delta=has_delta, res_scale=res_scale,
                             gate_idx=gate_idx, mod_idx=mod_idx, emit_x=emit_x,
                             ctx_tiles=n_ctx_tiles if split_x else None)
    return pl.pallas_call(
        kern, out_shape=out_shape, grid=(m_all // tm,), in_specs=in_specs, out_specs=out_specs,
        compiler_params=_cparams("arbitrary"), name="resnorm",
    )(*args)


def _mm_kernel(a_ref, w_ref, o_ref):
    o_ref[...] = jnp.dot(a_ref[...], w_ref[...], preferred_element_type=F32).astype(o_ref.dtype)


def _matmul(a, w, *, out_dtype, tm, tn, row_off=0, rows=None, name="matmul"):
    k = a.shape[1]
    n = w.shape[1]
    rows = a.shape[0] if rows is None else rows
    ro = row_off // tm
    return pl.pallas_call(
        _mm_kernel,
        out_shape=jax.ShapeDtypeStruct((rows, n), out_dtype),
        grid=(rows // tm, n // tn),
        in_specs=[pl.BlockSpec((tm, k), lambda i, j: (i + ro, 0)),
                  pl.BlockSpec((k, tn), lambda i, j: (0, j))],
        out_specs=pl.BlockSpec((tm, tn), lambda i, j: (i, j)),
        compiler_params=_cparams("arbitrary", "arbitrary"), name=name,
    )(a, w)


def _ffn_kernel(h_ref, wa_ref, wb_ref, wo_ref, o_ref):
    @pl.when(pl.program_id(1) == 0)
    def _():
        o_ref[...] = jnp.zeros_like(o_ref)

    h = h_ref[...]
    a = jnp.dot(h, wa_ref[...], preferred_element_type=F32)
    b = jnp.dot(h, wb_ref[...], preferred_element_type=F32)
    u = (a * jax.nn.sigmoid(a) * b).astype(BF16)
    o_ref[...] += jnp.dot(u, wo_ref[...], preferred_element_type=F32)


def _ffn(h, w_in, w_out, *, tm, tf):
    m, d = h.shape
    dff = w_out.shape[0]
    nf = dff // tf
    return pl.pallas_call(
        _ffn_kernel,
        out_shape=jax.ShapeDtypeStruct((m, d), F32),
        grid=(m // tm, nf),
        in_specs=[pl.BlockSpec((tm, d), lambda i, j: (i, 0)),
                  pl.BlockSpec((d, tf), lambda i, j: (0, j)),
                  pl.BlockSpec((d, tf), lambda i, j: (0, j + nf)),
                  pl.BlockSpec((tf, d), lambda i, j: (j, 0))],
        out_specs=pl.BlockSpec((tm, d), lambda i, j: (i, 0), pipeline_mode=pl.Buffered(1)),
        compiler_params=_cparams("arbitrary", "arbitrary"), name="ffn",
    )(h, w_in, w_in, w_out)


def _merge_kernel(yr_ref, yh_ref, wr_ref, wh_ref, gr_ref, gh_ref, o_ref):
    pr = jnp.dot(yr_ref[...], wr_ref[...], preferred_element_type=F32)
    ph = jnp.dot(yh_ref[...], wh_ref[...], preferred_element_type=F32)
    o_ref[...] = (jax.nn.sigmoid(gr_ref[...]) * pr
                  + jax.nn.sigmoid(gh_ref[...]) * ph).astype(o_ref.dtype)


def _merge(y_r, y_h, w_br, w_bh, z_rest, col_gr, col_gh, *, tm, tn):
    m, kr = y_r.shape
    kh = y_h.shape[1]
    d = w_br.shape[1]
    cgr, cgh = col_gr // tn, col_gh // tn
    return pl.pallas_call(
        _merge_kernel,
        out_shape=jax.ShapeDtypeStruct((m, d), BF16),
        grid=(m // tm, d // tn),
        in_specs=[pl.BlockSpec((tm, kr), lambda i, j: (i, 0)),
                  pl.BlockSpec((tm, kh), lambda i, j: (i, 0)),
                  pl.BlockSpec((kr, tn), lambda i, j: (0, j)),
                  pl.BlockSpec((kh, tn), lambda i, j: (0, j)),
                  pl.BlockSpec((tm, tn), lambda i, j: (i, j + cgr)),
                  pl.BlockSpec((tm, tn), lambda i, j: (i, j + cgh))],
        out_specs=pl.BlockSpec((tm, tn), lambda i, j: (i, j)),
        compiler_params=_cparams("arbitrary", "arbitrary"), name="merge",
    )(y_r, y_h, w_br, w_bh, z_rest, z_rest)


def _log2(n):
    assert n & (n - 1) == 0
    return n.bit_length() - 1


def _head_ones(n, group):
    sh = _log2(group)
    i = lax.broadcasted_iota(jnp.int32, (n, n), 0) >> sh
    j = lax.broadcasted_iota(jnp.int32, (n, n), 1) >> sh
    return jnp.where(i == j, 1.0, 0.0).astype(BF16)


def _rwkv_prep_kernel(z_ref, zp_ref, zn_ref, mup_ref, mun_ref, w0_ref, w2_ref, a0_ref, a2_ref,
                      g2_ref, kk_ref,
                      r_o, k_o, v_o, kkn_o, af_o, ab_o, lwf_o, lwb_o, g_o,
                      *, tm, mc, tc, tl, dr, wl, al, gl):
    row0 = pl.program_id(0) * tm
    in_ctx = row0 < mc
    pos = jnp.where(in_ctx, row0 % tc, (row0 - mc) % tl)
    seqlen = jnp.where(in_ctx, tc, tl)
    keep_prev = jnp.where(pos == 0, 0.0, 1.0)
    keep_next = jnp.where(pos + tm == seqlen, 0.0, 1.0)
    rows = lax.broadcasted_iota(jnp.int32, (tm, 1), 0)

    def shifted(c0, c1):
        z = z_ref[:, c0:c1]
        zp = jnp.where(rows == 0, keep_prev * zp_ref[7:8, c0:c1], pltpu.roll(z, 1, 0))
        zn = jnp.where(rows == tm - 1, keep_next * zn_ref[0:1, c0:c1], pltpu.roll(z, tm - 1, 0))
        return z + mup_ref[:, c0:c1] * (zp - z) + mun_ref[:, c0:c1] * (zn - z)

    bd = _head_ones(LANES, RWKV_HEAD)
    for j in range(dr // LANES):
        c0 = j * LANES
        r_o[:, c0:c0 + LANES] = shifted(c0, c0 + LANES)
        k = shifted(dr + c0, dr + c0 + LANES)
        k_o[:, c0:c0 + LANES] = k
        v_o[:, c0:c0 + LANES] = shifted(2 * dr + c0, 2 * dr + c0 + LANES)
        kk = k * kk_ref[:, c0:c0 + LANES]
        ss = _group_sum(kk * kk, bd)
        kkn_o[:, c0:c0 + LANES] = kk / jnp.maximum(jnp.sqrt(ss), 1e-12)

    c = 3 * dr
    for d, (lw_o, a_o) in enumerate(((lwf_o, af_o), (lwb_o, ab_o))):
        wd = shifted(c + d * wl, c + (d + 1) * wl)
        ad = shifted(c + 2 * wl + d * al, c + 2 * wl + (d + 1) * al)
        wpre = w0_ref[d:d + 1, :] + _dot(jnp.tanh(wd), w2_ref[d])
        nx = -wpre
        softplus = jnp.maximum(nx, 0.0) + jnp.log1p(jnp.exp(-jnp.abs(nx)))
        w_log = -softplus - 0.5
        lw_o[...] = -jnp.exp(w_log)
        a_o[...] = jax.nn.sigmoid(a0_ref[d:d + 1, :] + _dot(ad, a2_ref[d]))
    gd = shifted(c + 2 * wl + 2 * al, c + 2 * wl + 2 * al + gl)
    g_o[...] = _dot(jax.nn.sigmoid(gd), g2_ref[...])


def _rwkv_prep(z_rest, p, dims, *, tm, zr_cols, dr, wl, al, gl):
    mc, tc, tl, nlb = dims
    m = z_rest.shape[0]
    nblk8 = m // 8
    t8 = tm // 8
    full = lambda shape: pl.BlockSpec(shape, lambda i: (0,) * len(shape))
    outs = [jax.ShapeDtypeStruct((m, dr), F32)] * 9
    kern = functools.partial(_rwkv_prep_kernel, tm=tm, mc=mc, tc=tc, tl=tl, dr=dr, wl=wl, al=al,
                             gl=gl)
    return pl.pallas_call(
        kern, out_shape=outs, grid=(m // tm,),
        in_specs=[pl.BlockSpec((tm, zr_cols), lambda i: (i, 0)),
                  pl.BlockSpec((8, zr_cols), lambda i: (jnp.maximum(i * t8 - 1, 0), 0)),
                  pl.BlockSpec((8, zr_cols), lambda i: (jnp.minimum((i + 1) * t8, nblk8 - 1), 0)),
                  full((1, zr_cols)), full((1, zr_cols)),
                  full((2, dr)), full((2, wl, dr)), full((2, dr)), full((2, al, dr)),
                  full((gl, dr)), full((1, dr))],
        out_specs=[pl.BlockSpec((tm, dr), lambda i: (i, 0))] * 9,
        compiler_params=_cparams("arbitrary"), name="rwkv_prep",
    )(z_rest, z_rest, z_rest, p["mu_prev"], p["mu_next"], p["w0"], p["w2"], p["a0"], p["a2"],
      p["g2"], p["k_k"])


def _stack_heads(x, lo):
    return jnp.concatenate([jnp.where(lo, x, 0.0), jnp.where(lo, 0.0, x)], axis=0)


SUB_BLOCK = 16
assert CHUNK // SUB_BLOCK == 4


def _rwkv_masks(rev):
    c = CHUNK
    n2 = 2 * c
    ti = lax.broadcasted_iota(jnp.int32, (c, c), 0)
    tj = lax.broadcasted_iota(jnp.int32, (c, c), 1)
    cum = jnp.where((tj >= ti) if rev else (tj <= ti), 1.0, 0.0).astype(BF16)
    lo = lax.broadcasted_iota(jnp.int32, (c, LANES), 1) < RWKV_HEAD
    ii = lax.broadcasted_iota(jnp.int32, (n2, n2), 0)
    jj = lax.broadcasted_iota(jnp.int32, (n2, n2), 1)
    im, jm = ii & (c - 1), jj & (c - 1)
    strict = (jm > im) if rev else (jm < im)
    incl = (jm >= im) if rev else (jm <= im)
    eye = jnp.where(ii == jj, 1.0, 0.0)
    sub = (ii >> _log2(SUB_BLOCK)) == (jj >> _log2(SUB_BLOCK))
    return cum, lo, strict, incl, eye, sub


def _rwkv_chunk(r, k, v, kk, a, lw, k_a, s, rev, masks):
    c = CHUNK
    n2 = 2 * c
    cum, lo, strict, incl, eye, sub = masks
    bc = _exact_rowmix(cum, lw)
    yield
    bex = bc - lw
    btot = bc[0:1, :] if rev else bc[c - 1:c, :]
    kd = k * (1.0 + (a - 1.0) * k_a)
    bv = kk * a
    e_neg = jnp.exp(-bc)
    e_tot = jnp.exp(btot - bc)
    at = _stack_heads(-kk * jnp.exp(bex), lo).astype(BF16)
    rt_f = _stack_heads(r * jnp.exp(bc), lo)
    kh = _stack_heads(kd * e_neg, lo).astype(BF16)
    bh = _stack_heads(bv * e_neg, lo).astype(BF16)
    vb_f = _stack_heads(v, lo)
    vb = vb_f.astype(BF16)
    kc = _stack_heads(kd * e_tot, lo).astype(BF16)
    bcc = _stack_heads(bv * e_tot, lo).astype(BF16)

    gram = _dot_nt(jnp.concatenate([at, rt_f.astype(BF16)], axis=0),
                   jnp.concatenate([bh, kh], axis=0))
    yield
    nmat = jnp.where(strict, gram[:n2, :n2], 0.0)
    a_ka = jnp.where(strict, gram[:n2, n2:], 0.0)
    a_rb = jnp.where(incl, gram[n2:, :n2], 0.0)
    a_rk = jnp.where(incl, gram[n2:, n2:], 0.0)
    n_d = jnp.where(sub, nmat, 0.0)
    t_d = eye + n_d
    pw = n_d
    for _ in range(_log2(SUB_BLOCK) - 1):
        pw = _dot(pw, pw)
        yield
        t_d = t_d + _dot(t_d, pw)
    x1 = _dot(a_ka, vb)
    yield
    xm = _dot(t_d, jnp.where(sub, 0.0, nmat))
    yield
    xm2 = _dot(xm, xm)
    yield
    ym = eye + xm + xm2 + _dot(xm, xm2)
    yield
    tm_ = _dot(ym, t_d)
    yield
    aw = _dot(tm_, jnp.concatenate([at, x1.astype(BF16)], axis=1))
    yield
    ah, w = aw[:, :n2], aw[:, n2:]
    rhs = jnp.concatenate(
        [jnp.concatenate([vb, jnp.zeros_like(vb)], axis=1),
         jnp.concatenate([w.astype(BF16), ah.astype(BF16)], axis=1)], axis=0)
    qr = _dot(jnp.concatenate([a_rk, a_rb], axis=1), rhs)
    q = qr[:, :n2]
    rr = rt_f + qr[:, n2:]
    mm = _dot_tn(ah, bcc)
    nn = _dot_tn(jnp.concatenate([vb_f, w], axis=0), jnp.concatenate([kc, bcc], axis=0))
    yield
    o_bd = _dot_nt(rr, s) + q
    o = o_bd[:c, :] + o_bd[c:, :]
    s_new = s * jnp.exp(btot) + _dot(s, mm) + nn
    return o, s_new


def _rwkv_scan_kernel(*refs, has_init, emit_final, nchunk, npairs, aliased):
    it = iter(refs)
    fwd = [next(it) for _ in range(6)]
    bwd = [next(it) for _ in range(6)]
    ka_ref = next(it)
    s0_ref = next(it) if has_init else None
    if aliased:
        next(it), next(it)
    of_ref = next(it)
    ob_ref = next(it)
    sfin_ref = next(it) if emit_final else None
    s_scr = next(it)
    ci = pl.program_id(2)

    @pl.when(ci == 0)
    def _():
        if has_init:
            s_scr[...] = s0_ref[...]
        else:
            s_scr[...] = jnp.zeros_like(s_scr)

    chains, sinks = [], []
    for d, (ins, o_ref) in enumerate(((fwd, of_ref), (bwd, ob_ref))):
        masks = _rwkv_masks(rev=(d == 1))
        for hp in range(npairs):
            sl = slice(hp * LANES, (hp + 1) * LANES)
            r, k, v, kk, a, lw = (x[:, sl] for x in ins)
            chains.append(_rwkv_chunk(r, k, v, kk, a, lw, ka_ref[:, sl], s_scr[d, hp],
                                      rev=(d == 1), masks=masks))
            sinks.append((o_ref, sl, d, hp))
    for (o, s_new), (o_ref, sl, d, hp) in zip(_lockstep(chains), sinks):
        o_ref[:, sl] = o
        s_scr[d, hp] = s_new

    if emit_final:
        @pl.when(ci == nchunk - 1)
        def _():
            sfin_ref[...] = s_scr[...]


def _rwkv_scan(pre, k_a, s0_bd, *, row_off, nb, t, emit_final, npairs, into=None):
    m, dr = pre["r"].shape
    ngroup = dr // (LANES * npairs)
    nchunk = t // CHUNK
    rb = row_off // CHUNK
    has_init = s0_bd is not None
    fmap = lambda b, h, c: (rb + b * nchunk + c, h)
    bmap = lambda b, h, c: (rb + b * nchunk + (nchunk - 1 - c), h)
    blk = (CHUNK, LANES * npairs)
    in_specs = ([pl.BlockSpec(blk, fmap)] * 6 + [pl.BlockSpec(blk, bmap)] * 6
                + [pl.BlockSpec((1, LANES * npairs), lambda b, h, c: (0, h))])
    args = [pre["r"], pre["k"], pre["v"], pre["kk"], pre["a_f"], pre["lw_f"],
            pre["r"], pre["k"], pre["v"], pre["kk"], pre["a_b"], pre["lw_b"], k_a]
    sblk = pl.BlockSpec((None, 2, npairs, LANES, LANES), lambda b, h, c: (b, 0, h, 0, 0))
    if has_init:
        in_specs.append(sblk)
        args.append(s0_bd)
    aliases = {}
    if into is not None:
        aliases = {len(args): 0, len(args) + 1: 1}
        in_specs += [pl.BlockSpec(memory_space=pl.ANY)] * 2
        args += list(into)
    out_shape = [jax.ShapeDtypeStruct((m, dr), F32)] * 2
    out_specs = [pl.BlockSpec(blk, fmap), pl.BlockSpec(blk, bmap)]
    if emit_final:
        out_shape.append(jax.ShapeDtypeStruct((nb, 2, dr // LANES, LANES, LANES), F32))
        out_specs.append(sblk)
    kern = functools.partial(_rwkv_scan_kernel, has_init=has_init, emit_final=emit_final,
                             nchunk=nchunk, npairs=npairs, aliased=into is not None)
    return pl.pallas_call(
        kern, out_shape=out_shape, grid=(nb, ngroup, nchunk), in_specs=in_specs,
        out_specs=out_specs, scratch_shapes=[pltpu.VMEM((2, npairs, LANES, LANES), F32)],
        input_output_aliases=aliases,
        compiler_params=_cparams("arbitrary", "arbitrary", "arbitrary"), name="rwkv_scan",
    )(*args)


def _rwkv_post_kernel(of_ref, ob_ref, r_ref, k_ref, v_ref, af_ref, ab_ref, g_ref,
                      ka_ref, rk_ref, lnw_ref, lnb_ref, y_ref, *, dr):
    bd = _head_ones(LANES, RWKV_HEAD)
    inv = 1.0 / RWKV_HEAD
    for j in range(dr // LANES):
        sl = slice(j * LANES, (j + 1) * LANES)
        o = of_ref[:, sl] + ob_ref[:, sl]
        mean = _group_sum(o, bd) * inv
        oc = o - mean
        var = _group_sum(oc * oc, bd) * inv
        on = oc * lax.rsqrt(var + GN_EPS) * lnw_ref[:, sl] + lnb_ref[:, sl]
        r, k, v = r_ref[:, sl], k_ref[:, sl], v_ref[:, sl]
        k_a, r_k = ka_ref[:, sl], rk_ref[:, sl]
        bonus = 0.0
        for a_ref in (af_ref, ab_ref):
            kd = k * (1.0 + (a_ref[:, sl] - 1.0) * k_a)
            bonus = bonus + _group_sum(r * kd * r_k, bd) * v
        y_ref[:, sl] = ((on + bonus) * g_ref[:, sl]).astype(y_ref.dtype)


def _rwkv_post(o_f, o_b, pre, p, *, tm):
    rows, dr = o_f.shape
    own = pl.BlockSpec((tm, dr), lambda i: (i, 0))
    par = pl.BlockSpec((1, dr), lambda i: (0, 0))
    return pl.pallas_call(
        functools.partial(_rwkv_post_kernel, dr=dr),
        out_shape=jax.ShapeDtypeStruct((rows, dr), BF16),
        grid=(rows // tm,),
        in_specs=[own] * 8 + [par] * 4,
        out_specs=own,
        compiler_params=_cparams("arbitrary"), name="rwkv_post",
    )(o_f, o_b, pre["r"], pre["k"], pre["v"], pre["a_f"], pre["a_b"], pre["g"],
      p["k_a"], p["r_k"], p["ln_w"], p["ln_b"])


HGRN_LEVELS = 6


def _hgrn_mix_matrix(rev):
    c = CHUNK
    t = lax.broadcasted_iota(jnp.int32, (c, c), 0)
    i = lax.broadcasted_iota(jnp.int32, (c, c), 1)
    mats = []
    cum = (i >= t) if rev else (i <= t)
    mats.append(jnp.where(cum, 1.0, 0.0))
    for lvl in range(HGRN_LEVELS):
        half = 1 << lvl
        base = (t >> (lvl + 1)) << (lvl + 1)
        if rev:
            mid = base + half
            ref_ = i >= mid
        else:
            mid = base + half - 1
            ref_ = i <= mid
        mats.append(jnp.where(cum, 1.0, 0.0) - jnp.where(ref_, 1.0, 0.0))
    return jnp.concatenate(mats, axis=0).astype(BF16)


def _hgrn_masks(rev):
    c = CHUNK
    ti = lax.broadcasted_iota(jnp.int32, (c, c), 0)
    tj = lax.broadcasted_iota(jnp.int32, (c, c), 1)
    late, early = (tj, ti) if rev else (ti, tj)
    sels = [((late >> lvl) == (early >> lvl) + 1) & (((early >> lvl) & 1) == 0)
            for lvl in range(HGRN_LEVELS)]
    return _hgrn_mix_matrix(rev), ti == tj, sels


def _hgrn_chunk(q_pre, f_pre, v, lb, st, rev, masks):
    c = CHUNK
    mix, diag, sels = masks
    q = q_pre * jax.nn.sigmoid(q_pre)
    f = lb + (1.0 - lb) * jax.nn.sigmoid(f_pre)
    kf = 1.0 - f
    g = jnp.log(f)
    ex = _exact_rowmix(mix, g)
    yield
    b = ex[:c, :]
    btot = b[0:1, :] if rev else b[c - 1:c, :]
    attn = jnp.where(diag, _dot_nt(q, kf), 0.0)
    for lvl in range(HGRN_LEVELS):
        dq = ex[(lvl + 1) * c:(lvl + 2) * c, :]
        qs = q * jnp.exp(jnp.minimum(dq, 0.0))
        ks = kf * jnp.exp(jnp.minimum(-dq, 0.0))
        attn = attn + jnp.where(sels[lvl], _dot_nt(qs, ks), 0.0)
    yield
    o = _dot_nt(q * jnp.exp(b), st) + _dot(attn, v)
    st_new = st * jnp.exp(btot) + _dot_tn(v, kf * jnp.exp(btot - b))
    return o, st_new


def _hgrn_scan_kernel(*refs, has_init, emit_final, nsteps, nheads, cols, aliased):
    it = iter(refs)
    qf_ref, ff_ref, vf_ref = next(it), next(it), next(it)
    qb_ref, fb_ref, vb_ref = next(it), next(it), next(it)
    lb_ref = next(it)
    s0_ref = next(it) if has_init else None
    if aliased:
        next(it), next(it)
    of_ref, ob_ref = next(it), next(it)
    sfin_ref = next(it) if emit_final else None
    s_scr = next(it)
    in_scr, out_scr = (next(it), next(it)) if cols is not None else (None, None)
    ci = pl.program_id(2)

    @pl.when(ci == 0)
    def _():
        if has_init:
            for d in range(2):
                for h in range(nheads):
                    s_scr[d, h] = s0_ref[d, h].T
        else:
            s_scr[...] = jnp.zeros_like(s_scr)

    ins = ((qf_ref, ff_ref, vf_ref), (qb_ref, fb_ref, vb_ref))
    outs = (of_ref, ob_ref)
    all_masks = [_hgrn_masks(rev=False), _hgrn_masks(rev=True)]

    def advance(load, store):
        chains, sinks = [], []
        for d in range(2):
            for h in range(nheads):
                sl = slice(h * LANES, (h + 1) * LANES)
                qp, fp, v = (load(d, k, sl) for k in range(3))
                chains.append(_hgrn_chunk(qp, fp, v, lb_ref[:, sl], s_scr[d, h], rev=(d == 1),
                                          masks=all_masks[d]))
                sinks.append((d, h, sl))
        for (o, s_new), (d, h, sl) in zip(_lockstep(chains), sinks):
            store(d, sl, o)
            s_scr[d, h] = s_new

    if cols is None:
        def store_rows(d, sl, o):
            outs[d][:, sl] = o
        advance(lambda d, k, sl: ins[d][k][:, sl], store_rows)
    else:
        for d in range(2):
            for k in range(3):
                for j in range(cols):
                    in_scr[d, k, j] = ins[d][k][:, j, :]

        def body(j, carry):
            col = (j, cols - 1 - j)

            def store_col(d, sl, o):
                out_scr[d, col[d], :, sl] = o
            advance(lambda d, k, sl: in_scr[d, k, col[d], :, sl], store_col)
            return carry
        lax.fori_loop(0, cols, body, 0)
        for d in range(2):
            for j in range(cols):
                outs[d][:, j, :] = out_scr[d, j]

    if emit_final:
        @pl.when(ci == nsteps - 1)
        def _():
            for d in range(2):
                for h in range(nheads):
                    sfin_ref[d, h] = s_scr[d, h].T


HGRN_COLS = 8


def _hgrn_scan(zhs, lb, s0, *, row_off, nb, t, dg, emit_final, nheads, column_major,
               into=None):
    m = zhs.shape[0]
    ng = dg // (LANES * nheads)
    has_init = s0 is not None
    lanes = LANES * nheads
    if column_major:
        nrow = t // GRID_W
        assert nrow == CHUNK and m % t == 0 and row_off % t == 0
        nsteps = GRID_W // HGRN_COLS
        seq0 = row_off // t
        zsrc = zhs.reshape(m // t, nrow, GRID_W, 4 * dg)
        blk = (None, nrow, HGRN_COLS, lanes)
        fwd = lambda g: (lambda b, h, c: (b + seq0, 0, c, g * ng + h))
        bwd = lambda g: (lambda b, h, c: (b + seq0, 0, nsteps - 1 - c, g * ng + h))
        out_dims = (m // t, nrow, GRID_W, dg)
        ofwd = lambda b, h, c: (b + seq0, 0, c, h)
        obwd = lambda b, h, c: (b + seq0, 0, nsteps - 1 - c, h)
        cols = HGRN_COLS
    else:
        nsteps = t // CHUNK
        rb = row_off // CHUNK
        zsrc = zhs
        blk = (CHUNK, lanes)
        fwd = lambda g: (lambda b, h, c: (rb + b * nsteps + c, g * ng + h))
        bwd = lambda g: (lambda b, h, c: (rb + b * nsteps + nsteps - 1 - c, g * ng + h))
        out_dims = (m, dg)
        ofwd = lambda b, h, c: (rb + b * nsteps + c, h)
        obwd = lambda b, h, c: (rb + b * nsteps + nsteps - 1 - c, h)
        cols = None

    in_specs = [pl.BlockSpec(blk, fwd(0)), pl.BlockSpec(blk, fwd(1)), pl.BlockSpec(blk, fwd(3)),
                pl.BlockSpec(blk, bwd(0)), pl.BlockSpec(blk, bwd(2)), pl.BlockSpec(blk, bwd(3)),
                pl.BlockSpec((1, lanes), lambda b, h, c: (0, h))]
    args = [zsrc] * 6 + [lb]
    sblk = pl.BlockSpec((None, 2, nheads, LANES, LANES), lambda b, h, c: (b, 0, h, 0, 0))
    if has_init:
        in_specs.append(sblk)
        args.append(s0)
    aliases = {}
    if into is not None:
        aliases = {len(args): 0, len(args) + 1: 1}
        in_specs += [pl.BlockSpec(memory_space=pl.ANY)] * 2
        args += [o.reshape(out_dims) for o in into]
    out_shape = [jax.ShapeDtypeStruct(out_dims, F32)] * 2
    out_specs = [pl.BlockSpec(blk, ofwd), pl.BlockSpec(blk, obwd)]
    if emit_final:
        out_shape.append(jax.ShapeDtypeStruct((nb, 2, dg // LANES, LANES, LANES), F32))
        out_specs.append(sblk)
    kern = functools.partial(_hgrn_scan_kernel, has_init=has_init, emit_final=emit_final,
                             nsteps=nsteps, nheads=nheads, cols=cols, aliased=into is not None)
    scratch = [pltpu.VMEM((2, nheads, LANES, LANES), F32)]
    if column_major:
        scratch += [pltpu.VMEM((2, 3, cols, CHUNK, lanes), F32),
                    pltpu.VMEM((2, cols, CHUNK, lanes), F32)]
    outs = pl.pallas_call(
        kern, out_shape=out_shape, grid=(nb, ng, nsteps), in_specs=in_specs, out_specs=out_specs,
        scratch_shapes=scratch, input_output_aliases=aliases,
        compiler_params=_cparams("arbitrary", "arbitrary", "arbitrary"), name="hgrn_scan",
    )(*args)
    return [outs[0].reshape(m, dg), outs[1].reshape(m, dg)] + list(outs[2:])


def _hgrn_post_kernel(of_ref, ob_ref, gate_ref, gain_ref, y_ref, *, dg):
    gain = gain_ref[...]
    for j in range(dg // LANES):
        sl = slice(j * LANES, (j + 1) * LANES)
        o = of_ref[:, sl] + ob_ref[:, sl]
        on = o * lax.rsqrt(jnp.mean(o * o, axis=-1, keepdims=True) + RMS_EPS) * gain
        gt = gate_ref[:, sl]
        y_ref[:, sl] = (on * (gt * jax.nn.sigmoid(gt))).astype(y_ref.dtype)


def _hgrn_post(o_f, o_b, z_gates, gain, *, gate_col):
    rows_all, dg = o_f.shape
    kern = functools.partial(_hgrn_post_kernel, dg=dg)
    gcb = gate_col // dg
    par = pl.BlockSpec((1, LANES), lambda *_: (0, 0))
    tm = min(256, rows_all)
    ospec = pl.BlockSpec((tm, dg), lambda i: (i, 0))
    return pl.pallas_call(
        kern, out_shape=jax.ShapeDtypeStruct((rows_all, dg), BF16), grid=(rows_all // tm,),
        in_specs=[ospec, ospec, pl.BlockSpec((tm, dg), lambda i: (i, gcb)), par],
        out_specs=ospec,
        compiler_params=_cparams("arbitrary"), name="hgrn_post",
    )(o_f, o_b, z_gates, gain)


def _largest_divisor(n, candidates):
    for cand in candidates:
        if n % cand == 0:
            return cand
    raise ValueError(f"no tile in {candidates} divides {n}")


def _pad_to(x, axis, size):
    pad = size - x.shape[axis]
    if pad == 0:
        return x
    widths = [(0, 0)] * x.ndim
    widths[axis] = (0, pad)
    return jnp.pad(x, widths)


def _pair_state_to_bd(s):
    b, two, h, n, _ = s.shape
    s = s.reshape(b, two, h // 2, 2, n, n)
    z = jnp.zeros_like(s[:, :, :, 0])
    top = jnp.concatenate([s[:, :, :, 0], z], axis=-1)
    bot = jnp.concatenate([z, s[:, :, :, 1]], axis=-1)
    return jnp.concatenate([top, bot], axis=-2)


def _bd_to_pair_state(sbd):
    b, two, hp, n2, _ = sbd.shape
    n = n2 // 2
    s0 = sbd[:, :, :, :n, :n]
    s1 = sbd[:, :, :, n:, n:]
    return jnp.stack([s0, s1], axis=3).reshape(b, two, hp * 2, n, n)


def kernel(x_prompt, x_sample, state_rwkv, state_hgrn, c, c_ctx, w_mod, b_mod, norm_ffn1, norm_mix, norm_ffn2, ffn1_w_in, ffn1_w_out, ffn2_w_in, ffn2_w_out, w_in, rwkv_mu_prev, rwkv_mu_next, rwkv_w0, rwkv_w2, rwkv_a0, rwkv_a2, rwkv_g2, rwkv_k_k, rwkv_k_a, rwkv_r_k, rwkv_ln_w, rwkv_ln_b, hgrn_lb_logits, hgrn_norm, w_branch_rwkv, w_branch_hgrn, w_out, norm_final):
    nbc, tc, d = x_prompt.shape
    nbl, tl, _ = x_sample.shape
    depth = w_mod.shape[0]
    mc, ml = nbc * tc, nbl * tl
    m = mc + ml
    dims = (mc, tc, tl, nbl)
    dr = rwkv_w0.shape[-1]
    dg = w_branch_hgrn.shape[1]
    dff = ffn1_w_out.shape[1]
    wl, al, gl = rwkv_w2.shape[2], rwkv_a2.shape[2], rwkv_g2.shape[1]
    glp = -(-gl // LANES) * LANES
    rw_cols = 3 * dr + 2 * wl + 2 * al + gl
    zr_cols = 3 * dr + 2 * wl + 2 * al + glp
    assert nbl < MOD_ROWS and tl % GRID_W == 0 and tl // GRID_W == CHUNK and tc % CHUNK == 0

    tm_norm = min(256, tc)
    tm_mm = min(512, mc)
    tm_proj = min(1024, mc)
    tn_of = lambda n: _largest_divisor(n, (1024, 512, 256, 128))
    tf = _largest_divisor(dff, (256, 128))

    lb_all = jnp.cumsum(jax.nn.softmax(hgrn_lb_logits.astype(F32), axis=0), axis=0)
    cond = jnp.concatenate([c, c_ctx[None, :], jnp.zeros((MOD_ROWS - nbl - 1, d), F32)], axis=0)

    x = (x_prompt.reshape(mc, d), x_sample.reshape(ml, d))
    new_r, new_h = [], []
    for l in range(depth):
        mod3 = _modulation(cond, w_mod[l], b_mod[l], tn_of(N_MOD * d)).reshape(MOD_ROWS, N_MOD, d)

        h = _resnorm(x, None, mod3, norm_ffn1[l], dims, res_scale=None, gate_idx=None, mod_idx=0,
                     emit_x=False, out_dtype=BF16, tm=tm_norm)[0]
        delta = _ffn(h, ffn1_w_in[l].astype(BF16), ffn1_w_out[l].astype(BF16), tm=tm_proj, tf=tf)

        x, h = _resnorm(x, delta, mod3, norm_mix[l], dims, res_scale=FFN_HALF, gate_idx=2,
                        mod_idx=3, emit_x=True, out_dtype=BF16, tm=tm_norm)
        wi = w_in[l]
        hg0 = rw_cols
        w_r = _pad_to(wi[:, :rw_cols], 1, zr_cols).astype(BF16)
        w_hs = wi[:, hg0:hg0 + 4 * dg].astype(BF16)
        w_gates = wi[:, hg0 + 4 * dg:].astype(BF16)
        col_gate, col_gr, col_gh = 0, dg, dg + d
        z_r = _matmul(h, w_r, out_dtype=F32, tm=tm_proj, tn=tn_of(zr_cols), name="proj_rwkv")
        z_g = _matmul(h, w_gates, out_dtype=F32, tm=tm_proj, tn=tn_of(dg), name="proj_gates")
        zhs = _matmul(h, w_hs, out_dtype=F32, tm=tm_proj, tn=tn_of(dg), name="proj_hgrn")

        p = {
            "mu_prev": _pad_to(rwkv_mu_prev[l][None, :], 1, zr_cols),
            "mu_next": _pad_to(rwkv_mu_next[l][None, :], 1, zr_cols),
            "w0": rwkv_w0[l], "w2": rwkv_w2[l].astype(BF16),
            "a0": rwkv_a0[l], "a2": rwkv_a2[l].astype(BF16),
            "g2": _pad_to(rwkv_g2[l], 0, glp).astype(BF16),
            "k_k": rwkv_k_k[l][None, :], "k_a": rwkv_k_a[l][None, :],
            "r_k": rwkv_r_k[l].reshape(1, dr),
            "ln_w": rwkv_ln_w[l][None, :], "ln_b": rwkv_ln_b[l][None, :],
        }
        names = ("r", "k", "v", "kk", "a_f", "a_b", "lw_f", "lw_b", "g")
        pre = dict(zip(names, _rwkv_prep(z_r, p, dims, tm=min(128, tc), zr_cols=zr_cols, dr=dr,
                                         wl=wl, al=al, gl=glp)))
        npairs = _largest_divisor(dr // LANES, (8, 4, 2, 1))
        of_c, ob_c, sr_c = _rwkv_scan(pre, p["k_a"], None, row_off=0, nb=nbc, t=tc,
                                      emit_final=True, npairs=npairs)
        o_f, o_b = _rwkv_scan(pre, p["k_a"], _pair_state_to_bd(state_rwkv[:, l].astype(F32)),
                              row_off=mc, nb=nbl, t=tl, emit_final=False, npairs=npairs,
                              into=(of_c, ob_c))
        y_r = _rwkv_post(o_f, o_b, pre, p, tm=min(256, tc))
        new_r.append(_bd_to_pair_state(sr_c))

        lb = lb_all[l][None, :]
        gain_h = hgrn_norm[l][None, :]
        nheads = _largest_divisor(dg // LANES, (8, 4, 2, 1))
        nheads_cols = _largest_divisor(dg // LANES, (4, 2, 1))
        hf_c, hb_c, sh_c = _hgrn_scan(zhs, lb, None, row_off=0, nb=nbc, t=tc, dg=dg,
                                      emit_final=True, nheads=nheads, column_major=False)
        h_f, h_b = _hgrn_scan(zhs, lb, state_hgrn[:, l].astype(F32), row_off=mc, nb=nbl, t=tl,
                              dg=dg, emit_final=False, nheads=nheads_cols, column_major=True,
                              into=(hf_c, hb_c))
        y_h = _hgrn_post(h_f, h_b, z_g, gain_h, gate_col=col_gate)
        new_h.append(sh_c)

        merged = _merge(y_r, y_h, w_branch_rwkv[l].astype(BF16), w_branch_hgrn[l].astype(BF16),
                        z_g, col_gr, col_gh, tm=tm_mm, tn=tn_of(dg))
        delta = _matmul(merged, w_out[l].astype(BF16), out_dtype=F32, tm=tm_proj, tn=tn_of(d),
                        name="mix_out")

        x, h = _resnorm(x, delta, mod3, norm_ffn2[l], dims, res_scale=1.0, gate_idx=5, mod_idx=6,
                        emit_x=True, out_dtype=BF16, tm=tm_norm)
        delta = _ffn(h, ffn2_w_in[l].astype(BF16), ffn2_w_out[l].astype(BF16), tm=tm_proj, tf=tf)
        last = l == depth - 1
        if not last:
            x = _resnorm(x, delta, mod3, norm_ffn2[l], dims, res_scale=FFN_HALF, gate_idx=8,
                         mod_idx=None, emit_x=True, out_dtype=F32, tm=tm_norm)[0]

    y_c = _resnorm(x, delta, mod3, norm_final, dims, res_scale=FFN_HALF, gate_idx=8, mod_idx=None,
                   emit_x=False, out_dtype=F32, tm=tm_norm, row_off=0, rows=mc)[0]
    y_l = _resnorm(x, delta, mod3, norm_final, dims, res_scale=FFN_HALF, gate_idx=8, mod_idx=None,
                   emit_x=False, out_dtype=F32, tm=tm_norm, row_off=mc, rows=ml)[0]
    return (y_c.reshape(nbc, tc, d), y_l.reshape(nbl, tl, d),
            jnp.stack(new_r, axis=1), jnp.stack(new_h, axis=1))
```

```python
import functools

import jax
import jax.numpy as jnp
from jax import lax
from jax.experimental import pallas as pl
from jax.experimental.pallas import tpu as pltpu

F32 = jnp.float32
BF16 = jnp.bfloat16

CHUNK = 64
GRID_W = 64
RWKV_HEAD = 64
HGRN_EXPAND = 128
N_MOD = 9
RMS_EPS = 1e-6
GN_EPS = 64e-5
FFN_HALF = 0.5
LANES = 128
MOD_ROWS = 16
VMEM_LIMIT_BYTES = 56 * 1024 * 1024


def _cparams(*sem):
    return pltpu.CompilerParams(dimension_semantics=sem, vmem_limit_bytes=VMEM_LIMIT_BYTES)


def _dot(a, b):
    return jnp.dot(a.astype(BF16), b.astype(BF16), preferred_element_type=F32)


def _dot_nt(a, b):
    return lax.dot_general(a.astype(BF16), b.astype(BF16), (((1,), (1,)), ((), ())),
                           preferred_element_type=F32)


def _dot_tn(a, b):
    return jnp.dot(a.T.astype(BF16), b.astype(BF16), preferred_element_type=F32)


def _split2(x):
    hi = x.astype(BF16)
    return hi, (x - hi.astype(F32)).astype(BF16)


def _split3(x):
    x1 = x.astype(BF16)
    d1 = x - x1.astype(F32)
    x2 = d1.astype(BF16)
    x3 = (d1 - x2.astype(F32)).astype(BF16)
    return x1, x2, x3


def _exact_rowmix(mat3, x):
    return jnp.dot(mat3, jnp.concatenate(_split3(x), axis=0), preferred_element_type=F32)


def _group_sum(x, bd_ones):
    x1, x2 = _split2(x)
    return (jnp.dot(x1, bd_ones, preferred_element_type=F32)
            + jnp.dot(x2, bd_ones, preferred_element_type=F32))


def _lockstep(chains):
    results = [None] * len(chains)
    active = list(enumerate(chains))
    while active:
        still = []
        for idx, chain in active:
            try:
                next(chain)
                still.append((idx, chain))
            except StopIteration as done:
                results[idx] = done.value
        active = still
    return results


def _mod_kernel(c_ref, w_ref, b_ref, o_ref):
    c = c_ref[...]
    s = c * jax.nn.sigmoid(c)
    o_ref[...] = _dot(s, w_ref[...]) + b_ref[...]


def _modulation(cond, w_mod, b_mod, tn):
    d, n = w_mod.shape
    return pl.pallas_call(
        _mod_kernel,
        out_shape=jax.ShapeDtypeStruct((MOD_ROWS, n), F32),
        grid=(n // tn,),
        in_specs=[pl.BlockSpec((MOD_ROWS, d), lambda j: (0, 0)),
                  pl.BlockSpec((d, tn), lambda j: (0, j)),
                  pl.BlockSpec((1, tn), lambda j: (0, j))],
        out_specs=pl.BlockSpec((MOD_ROWS, tn), lambda j: (0, j)),
        compiler_params=_cparams("arbitrary"),
        name="modulation",
    )(cond, w_mod, b_mod.reshape(1, n))


def _resnorm_kernel(*refs, has_delta, res_scale, gate_idx, mod_idx, emit_x, ctx_tiles):
    it = iter(refs)
    x_ref = next(it)
    xl_ref = next(it) if ctx_tiles is not None else None
    delta_ref = next(it) if has_delta else None
    mod_ref = next(it) if (has_delta or mod_idx is not None) else None
    gain_ref = next(it)
    xo_ref = next(it) if emit_x else None
    h_ref = next(it)
    x = x_ref[...]
    if ctx_tiles is not None:
        x = jnp.where(pl.program_id(0) < ctx_tiles, x, xl_ref[...])
    if has_delta:
        g = mod_ref[gate_idx:gate_idx + 1, :]
        x = x + g * (res_scale * delta_ref[...])
    if emit_x:
        xo_ref[...] = x
    y = x * lax.rsqrt(jnp.mean(x * x, axis=-1, keepdims=True) + RMS_EPS)
    y = y * gain_ref[...]
    if mod_idx is not None:
        sh = mod_ref[mod_idx:mod_idx + 1, :]
        sc = mod_ref[mod_idx + 1:mod_idx + 2, :]
        y = y * (1.0 + sc) + sh
    h_ref[...] = y.astype(h_ref.dtype)


def _resnorm(x, delta, mod3, gain, dims, *, res_scale, gate_idx, mod_idx, emit_x, out_dtype,
             tm, row_off=0, rows=None):
    mc, tc, tl, nlb = dims
    n_ctx_tiles = mc // tm
    tiles_per_lat = tl // tm
    ro = row_off // tm
    split_x = isinstance(x, tuple)
    if split_x:
        assert row_off == 0 and rows is None
        d = x[0].shape[1]
        m_all = x[0].shape[0] + x[1].shape[0]
        in_specs = [pl.BlockSpec((tm, d), lambda i: (jnp.minimum(i, n_ctx_tiles - 1), 0)),
                    pl.BlockSpec((tm, d), lambda i: (jnp.maximum(i - n_ctx_tiles, 0), 0))]
        args = list(x)
    else:
        d = x.shape[1]
        m_all = rows if rows is not None else x.shape[0]
        in_specs = [pl.BlockSpec((tm, d), lambda i: (i + ro, 0))]
        args = [x]

    def mod_row(i):
        gi = i + ro
        return jnp.where(gi < n_ctx_tiles, nlb, (gi - n_ctx_tiles) // tiles_per_lat)

    has_delta = delta is not None
    if has_delta:
        in_specs.append(pl.BlockSpec((tm, d), lambda i: (i + ro, 0)))
        args.append(delta)
    if has_delta or mod_idx is not None:
        in_specs.append(pl.BlockSpec((None, N_MOD, d), lambda i: (mod_row(i), 0, 0)))
        args.append(mod3)
    in_specs.append(pl.BlockSpec((1, d), lambda i: (0, 0)))
    args.append(gain.reshape(1, d))
    out_shape, out_specs = [], []
    if emit_x:
        out_shape.append(jax.ShapeDtypeStruct((m_all, d), F32))
        out_specs.append(pl.BlockSpec((tm, d), lambda i: (i, 0)))
    out_shape.append(jax.ShapeDtypeStruct((m_all, d), out_dtype))
    out_specs.append(pl.BlockSpec((tm, d), lambda i: (i, 0)))
    kern = functools.partial(_resnorm_kernel, has_delta=has_delta, res_scale=res_scale,
                             gate_idx=gate_idx, mod_idx=mod_idx, emit_x=emit_x,
                             ctx_tiles=n_ctx_tiles if split_x else None)
    return pl.pallas_call(
        kern, out_shape=out_shape, grid=(m_all // tm,), in_specs=in_specs, out_specs=out_specs,
        compiler_params=_cparams("arbitrary"), name="resnorm",
    )(*args)


def _mm_kernel(a_ref, w_ref, o_ref):
    o_ref[...] = jnp.dot(a_ref[...], w_ref[...], preferred_element_type=F32).astype(o_ref.dtype)


def _matmul(a, w, *, out_dtype, tm, tn, row_off=0, rows=None, name="matmul"):
    k = a.shape[1]
    n = w.shape[1]
    rows = a.shape[0] if rows is None else rows
    ro = row_off // tm
    return pl.pallas_call(
        _mm_kernel,
        out_shape=jax.ShapeDtypeStruct((rows, n), out_dtype),
        grid=(rows // tm, n // tn),
        in_specs=[pl.BlockSpec((tm, k), lambda i, j: (i + ro, 0)),
                  pl.BlockSpec((k, tn), lambda i, j: (0, j))],
        out_specs=pl.BlockSpec((tm, tn), lambda i, j: (i, j)),
        compiler_params=_cparams("arbitrary", "arbitrary"), name=name,
    )(a, w)


def _ffn_kernel(h_ref, wa_ref, wb_ref, wo_ref, o_ref):
    @pl.when(pl.program_id(1) == 0)
    def _():
        o_ref[...] = jnp.zeros_like(o_ref)

    h = h_ref[...]
    a = jnp.dot(h, wa_ref[...], preferred_element_type=F32)
    b = jnp.dot(h, wb_ref[...], preferred_element_type=F32)
    u = (a * jax.nn.sigmoid(a) * b).astype(BF16)
    o_ref[...] += jnp.dot(u, wo_ref[...], preferred_element_type=F32)


def _ffn(h, w_in, w_out, *, tm, tf):
    m, d = h.shape
    dff = w_out.shape[0]
    nf = dff // tf
    return pl.pallas_call(
        _ffn_kernel,
        out_shape=jax.ShapeDtypeStruct((m, d), F32),
        grid=(m // tm, nf),
        in_specs=[pl.BlockSpec((tm, d), lambda i, j: (i, 0)),
                  pl.BlockSpec((d, tf), lambda i, j: (0, j)),
                  pl.BlockSpec((d, tf), lambda i, j: (0, j + nf)),
                  pl.BlockSpec((tf, d), lambda i, j: (j, 0))],
        out_specs=pl.BlockSpec((tm, d), lambda i, j: (i, 0), pipeline_mode=pl.Buffered(1)),
        compiler_params=_cparams("arbitrary", "arbitrary"), name="ffn",
    )(h, w_in, w_in, w_out)


def _merge_kernel(yr_ref, yh_ref, wr_ref, wh_ref, gr_ref, gh_ref, o_ref):
    pr = jnp.dot(yr_ref[...], wr_ref[...], preferred_element_type=F32)
    ph = jnp.dot(yh_ref[...], wh_ref[...], preferred_element_type=F32)
    o_ref[...] = (jax.nn.sigmoid(gr_ref[...]) * pr
                  + jax.nn.sigmoid(gh_ref[...]) * ph).astype(o_ref.dtype)


def _merge(y_r, y_h, w_br, w_bh, z_rest, col_gr, col_gh, *, tm, tn):
    m, kr = y_r.shape
    kh = y_h.shape[1]
    d = w_br.shape[1]
    cgr, cgh = col_gr // tn, col_gh // tn
    return pl.pallas_call(
        _merge_kernel,
        out_shape=jax.ShapeDtypeStruct((m, d), BF16),
        grid=(m // tm, d // tn),
        in_specs=[pl.BlockSpec((tm, kr), lambda i, j: (i, 0)),
                  pl.BlockSpec((tm, kh), lambda i, j: (i, 0)),
                  pl.BlockSpec((kr, tn), lambda i, j: (0, j)),
                  pl.BlockSpec((kh, tn), lambda i, j: (0, j)),
                  pl.BlockSpec((tm, tn), lambda i, j: (i, j + cgr)),
                  pl.BlockSpec((tm, tn), lambda i, j: (i, j + cgh))],
        out_specs=pl.BlockSpec((tm, tn), lambda i, j: (i, j)),
        compiler_params=_cparams("arbitrary", "arbitrary"), name="merge",
    )(y_r, y_h, w_br, w_bh, z_rest, z_rest)


def _log2(n):
    assert n & (n - 1) == 0
    return n.bit_length() - 1


def _head_ones(n, group):
    sh = _log2(group)
    i = lax.broadcasted_iota(jnp.int32, (n, n), 0) >> sh
    j = lax.broadcasted_iota(jnp.int32, (n, n), 1) >> sh
    return jnp.where(i == j, 1.0, 0.0).astype(BF16)


def _rwkv_prep_kernel(z_ref, zp_ref, zn_ref, mup_ref, mun_ref, w0_ref, w2_ref, a0_ref, a2_ref,
                      g2_ref, kk_ref,
                      r_o, k_o, v_o, kkn_o, af_o, ab_o, lwf_o, lwb_o, g_o,
                      *, tm, mc, tc, tl, dr, wl, al, gl):
    row0 = pl.program_id(0) * tm
    in_ctx = row0 < mc
    pos = jnp.where(in_ctx, row0 % tc, (row0 - mc) % tl)
    seqlen = jnp.where(in_ctx, tc, tl)
    keep_prev = jnp.where(pos == 0, 0.0, 1.0)
    keep_next = jnp.where(pos + tm == seqlen, 0.0, 1.0)
    rows = lax.broadcasted_iota(jnp.int32, (tm, 1), 0)

    def shifted(c0, c1):
        z = z_ref[:, c0:c1]
        zp = jnp.where(rows == 0, keep_prev * zp_ref[7:8, c0:c1], pltpu.roll(z, 1, 0))
        zn = jnp.where(rows == tm - 1, keep_next * zn_ref[0:1, c0:c1], pltpu.roll(z, tm - 1, 0))
        return z + mup_ref[:, c0:c1] * (zp - z) + mun_ref[:, c0:c1] * (zn - z)

    bd = _head_ones(LANES, RWKV_HEAD)
    for j in range(dr // LANES):
        c0 = j * LANES
        r_o[:, c0:c0 + LANES] = shifted(c0, c0 + LANES)
        k = shifted(dr + c0, dr + c0 + LANES)
        k_o[:, c0:c0 + LANES] = k
        v_o[:, c0:c0 + LANES] = shifted(2 * dr + c0, 2 * dr + c0 + LANES)
        kk = k * kk_ref[:, c0:c0 + LANES]
        ss = _group_sum(kk * kk, bd)
        kkn_o[:, c0:c0 + LANES] = kk / jnp.maximum(jnp.sqrt(ss), 1e-12)

    c = 3 * dr
    for d, (lw_o, a_o) in enumerate(((lwf_o, af_o), (lwb_o, ab_o))):
        wd = shifted(c + d * wl, c + (d + 1) * wl)
        ad = shifted(c + 2 * wl + d * al, c + 2 * wl + (d + 1) * al)
        wpre = w0_ref[d:d + 1, :] + _dot(jnp.tanh(wd), w2_ref[d])
        nx = -wpre
        softplus = jnp.maximum(nx, 0.0) + jnp.log1p(jnp.exp(-jnp.abs(nx)))
        w_log = -softplus - 0.5
        lw_o[...] = -jnp.exp(w_log)
        a_o[...] = jax.nn.sigmoid(a0_ref[d:d + 1, :] + _dot(ad, a2_ref[d]))
    gd = shifted(c + 2 * wl + 2 * al, c + 2 * wl + 2 * al + gl)
    g_o[...] = _dot(jax.nn.sigmoid(gd), g2_ref[...])


def _rwkv_prep(z_rest, p, dims, *, tm, zr_cols, dr, wl, al, gl):
    mc, tc, tl, nlb = dims
    m = z_rest.shape[0]
    nblk8 = m // 8
    t8 = tm // 8
    full = lambda shape: pl.BlockSpec(shape, lambda i: (0,) * len(shape))
    outs = [jax.ShapeDtypeStruct((m, dr), F32)] * 9
    kern = functools.partial(_rwkv_prep_kernel, tm=tm, mc=mc, tc=tc, tl=tl, dr=dr, wl=wl, al=al,
                             gl=gl)
    return pl.pallas_call(
        kern, out_shape=outs, grid=(m // tm,),
        in_specs=[pl.BlockSpec((tm, zr_cols), lambda i: (i, 0)),
                  pl.BlockSpec((8, zr_cols), lambda i: (jnp.maximum(i * t8 - 1, 0), 0)),
                  pl.BlockSpec((8, zr_cols), lambda i: (jnp.minimum((i + 1) * t8, nblk8 - 1), 0)),
                  full((1, zr_cols)), full((1, zr_cols)),
                  full((2, dr)), full((2, wl, dr)), full((2, dr)), full((2, al, dr)),
                  full((gl, dr)), full((1, dr))],
        out_specs=[pl.BlockSpec((tm, dr), lambda i: (i, 0))] * 9,
        compiler_params=_cparams("arbitrary"), name="rwkv_prep",
    )(z_rest, z_rest, z_rest, p["mu_prev"], p["mu_next"], p["w0"], p["w2"], p["a0"], p["a2"],
      p["g2"], p["k_k"])


def _stack_heads(x, lo):
    return jnp.concatenate([jnp.where(lo, x, 0.0), jnp.where(lo, 0.0, x)], axis=0)


SUB_BLOCK = 16
assert CHUNK // SUB_BLOCK == 4


def _rwkv_masks(rev):
    c = CHUNK
    n2 = 2 * c
    ti = lax.broadcasted_iota(jnp.int32, (c, c), 0)
    tj = lax.broadcasted_iota(jnp.int32, (c, c), 1)
    cum = jnp.where((tj >= ti) if rev else (tj <= ti), 1.0, 0.0).astype(BF16)
    cum = jnp.concatenate([cum, cum, cum], axis=1)
    lo = lax.broadcasted_iota(jnp.int32, (c, LANES), 1) < RWKV_HEAD
    ii = lax.broadcasted_iota(jnp.int32, (n2, n2), 0)
    jj = lax.broadcasted_iota(jnp.int32, (n2, n2), 1)
    im, jm = ii & (c - 1), jj & (c - 1)
    strict = (jm > im) if rev else (jm < im)
    incl = (jm >= im) if rev else (jm <= im)
    eye = jnp.where(ii == jj, 1.0, 0.0)
    sub = (ii >> _log2(SUB_BLOCK)) == (jj >> _log2(SUB_BLOCK))
    return cum, lo, strict, incl, eye, sub


def _rwkv_chunk(r, k, v, kk, a, lw, k_a, s, rev, masks):
    c = CHUNK
    n2 = 2 * c
    cum, lo, strict, incl, eye, sub = masks
    bc = _exact_rowmix(cum, lw)
    yield
    bex = bc - lw
    btot = bc[0:1, :] if rev else bc[c - 1:c, :]
    kd = k * (1.0 + (a - 1.0) * k_a)
    bv = kk * a
    e_neg = jnp.exp(-bc)
    e_tot = jnp.exp(btot - bc)
    at = _stack_heads(-kk * jnp.exp(bex), lo).astype(BF16)
    rt_f = _stack_heads(r * jnp.exp(bc), lo)
    kh = _stack_heads(kd * e_neg, lo).astype(BF16)
    bh = _stack_heads(bv * e_neg, lo).astype(BF16)
    vb_f = _stack_heads(v, lo)
    vb = vb_f.astype(BF16)
    kc = _stack_heads(kd * e_tot, lo).astype(BF16)
    bcc = _stack_heads(bv * e_tot, lo).astype(BF16)

    gram = _dot_nt(jnp.concatenate([at, rt_f.astype(BF16)], axis=0),
                   jnp.concatenate([bh, kh], axis=0))
    yield
    nmat = jnp.where(strict, gram[:n2, :n2], 0.0)
    a_ka = jnp.where(strict, gram[:n2, n2:], 0.0)
    a_rb = jnp.where(incl, gram[n2:, :n2], 0.0)
    a_rk = jnp.where(incl, gram[n2:, n2:], 0.0)
    n_d = jnp.where(sub, nmat, 0.0)
    t_d = eye + n_d
    pw = n_d
    for _ in range(_log2(SUB_BLOCK) - 1):
        pw = _dot(pw, pw)
        yield
        t_d = t_d + _dot(t_d, pw)
    x1 = _dot(a_ka, vb)
    yield
    xm = _dot(t_d, jnp.where(sub, 0.0, nmat))
    yield
    xm2 = _dot(xm, xm)
    yield
    ym = eye + xm + xm2 + _dot(xm, xm2)
    yield
    tm_ = _dot(ym, t_d)
    yield
    aw = _dot(tm_, jnp.concatenate([at, x1.astype(BF16)], axis=1))
    yield
    ah, w = aw[:, :n2], aw[:, n2:]
    rhs = jnp.concatenate(
        [jnp.concatenate([vb, jnp.zeros_like(vb)], axis=1),
         jnp.concatenate([w.astype(BF16), ah.astype(BF16)], axis=1)], axis=0)
    qr = _dot(jnp.concatenate([a_rk, a_rb], axis=1), rhs)
    q = qr[:, :n2]
    rr = rt_f + qr[:, n2:]
    mm = _dot_tn(ah, bcc)
    nn = _dot_tn(jnp.concatenate([vb_f, w], axis=0), jnp.concatenate([kc, bcc], axis=0))
    yield
    o_bd = _dot_nt(rr, s) + q
    o = o_bd[:c, :] + o_bd[c:, :]
    s_new = s * jnp.exp(btot) + _dot(s, mm) + nn
    return o, s_new


def _rwkv_scan_kernel(*refs, has_init, emit_final, nchunk, npairs, aliased):
    it = iter(refs)
    fwd = [next(it) for _ in range(6)]
    bwd = [next(it) for _ in range(6)]
    ka_ref = next(it)
    s0_ref = next(it) if has_init else None
    if aliased:
        next(it), next(it)
    of_ref = next(it)
    ob_ref = next(it)
    sfin_ref = next(it) if emit_final else None
    s_scr = next(it)
    ci = pl.program_id(2)

    @pl.when(ci == 0)
    def _():
        if has_init:
            s_scr[...] = s0_ref[...]
        else:
            s_scr[...] = jnp.zeros_like(s_scr)

    chains, sinks = [], []
    for d, (ins, o_ref) in enumerate(((fwd, of_ref), (bwd, ob_ref))):
        masks = _rwkv_masks(rev=(d == 1))
        for hp in range(npairs):
            sl = slice(hp * LANES, (hp + 1) * LANES)
            r, k, v, kk, a, lw = (x[:, sl] for x in ins)
            chains.append(_rwkv_chunk(r, k, v, kk, a, lw, ka_ref[:, sl], s_scr[d, hp],
                                      rev=(d == 1), masks=masks))
            sinks.append((o_ref, sl, d, hp))
    for (o, s_new), (o_ref, sl, d, hp) in zip(_lockstep(chains), sinks):
        o_ref[:, sl] = o
        s_scr[d, hp] = s_new

    if emit_final:
        @pl.when(ci == nchunk - 1)
        def _():
            sfin_ref[...] = s_scr[...]


def _rwkv_scan(pre, k_a, s0_bd, *, row_off, nb, t, emit_final, npairs, into=None):
    m, dr = pre["r"].shape
    ngroup = dr // (LANES * npairs)
    nchunk = t // CHUNK
    rb = row_off // CHUNK
    has_init = s0_bd is not None
    fmap = lambda b, h, c: (rb + b * nchunk + c, h)
    bmap = lambda b, h, c: (rb + b * nchunk + (nchunk - 1 - c), h)
    blk = (CHUNK, LANES * npairs)
    in_specs = ([pl.BlockSpec(blk, fmap)] * 6 + [pl.BlockSpec(blk, bmap)] * 6
                + [pl.BlockSpec((1, LANES * npairs), lambda b, h, c: (0, h))])
    args = [pre["r"], pre["k"], pre["v"], pre["kk"], pre["a_f"], pre["lw_f"],
            pre["r"], pre["k"], pre["v"], pre["kk"], pre["a_b"], pre["lw_b"], k_a]
    sblk = pl.BlockSpec((None, 2, npairs, LANES, LANES), lambda b, h, c: (b, 0, h, 0, 0))
    if has_init:
        in_specs.append(sblk)
        args.append(s0_bd)
    aliases = {}
    if into is not None:
        aliases = {len(args): 0, len(args) + 1: 1}
        in_specs += [pl.BlockSpec(memory_space=pl.ANY)] * 2
        args += list(into)
    out_shape = [jax.ShapeDtypeStruct((m, dr), F32)] * 2
    out_specs = [pl.BlockSpec(blk, fmap), pl.BlockSpec(blk, bmap)]
    if emit_final:
        out_shape.append(jax.ShapeDtypeStruct((nb, 2, dr // LANES, LANES, LANES), F32))
        out_specs.append(sblk)
    kern = functools.partial(_rwkv_scan_kernel, has_init=has_init, emit_final=emit_final,
                             nchunk=nchunk, npairs=npairs, aliased=into is not None)
    return pl.pallas_call(
        kern, out_shape=out_shape, grid=(nb, ngroup, nchunk), in_specs=in_specs,
        out_specs=out_specs, scratch_shapes=[pltpu.VMEM((2, npairs, LANES, LANES), F32)],
        input_output_aliases=aliases,
        compiler_params=_cparams("arbitrary", "arbitrary", "arbitrary"), name="rwkv_scan",
    )(*args)


def _rwkv_post_kernel(of_ref, ob_ref, r_ref, k_ref, v_ref, af_ref, ab_ref, g_ref,
                      ka_ref, rk_ref, lnw_ref, lnb_ref, y_ref, *, dr):
    bd = _head_ones(LANES, RWKV_HEAD)
    inv = 1.0 / RWKV_HEAD
    for j in range(dr // LANES):
        sl = slice(j * LANES, (j + 1) * LANES)
        o = of_ref[:, sl] + ob_ref[:, sl]
        mean = _group_sum(o, bd) * inv
        oc = o - mean
        var = _group_sum(oc * oc, bd) * inv
        on = oc * lax.rsqrt(var + GN_EPS) * lnw_ref[:, sl] + lnb_ref[:, sl]
        r, k, v = r_ref[:, sl], k_ref[:, sl], v_ref[:, sl]
        k_a, r_k = ka_ref[:, sl], rk_ref[:, sl]
        bonus = 0.0
        for a_ref in (af_ref, ab_ref):
            kd = k * (1.0 + (a_ref[:, sl] - 1.0) * k_a)
            bonus = bonus + _group_sum(r * kd * r_k, bd) * v
        y_ref[:, sl] = ((on + bonus) * g_ref[:, sl]).astype(y_ref.dtype)


def _rwkv_post(o_f, o_b, pre, p, *, tm):
    rows, dr = o_f.shape
    own = pl.BlockSpec((tm, dr), lambda i: (i, 0))
    par = pl.BlockSpec((1, dr), lambda i: (0, 0))
    return pl.pallas_call(
        functools.partial(_rwkv_post_kernel, dr=dr),
        out_shape=jax.ShapeDtypeStruct((rows, dr), BF16),
        grid=(rows // tm,),
        in_specs=[own] * 8 + [par] * 4,
        out_specs=own,
        compiler_params=_cparams("arbitrary"), name="rwkv_post",
    )(o_f, o_b, pre["r"], pre["k"], pre["v"], pre["a_f"], pre["a_b"], pre["g"],
      p["k_a"], p["r_k"], p["ln_w"], p["ln_b"])


HGRN_LEVELS = 6


def _hgrn_mix_matrix(rev):
    c = CHUNK
    t = lax.broadcasted_iota(jnp.int32, (c, c), 0)
    i = lax.broadcasted_iota(jnp.int32, (c, c), 1)
    mats = []
    cum = (i >= t) if rev else (i <= t)
    mats.append(jnp.where(cum, 1.0, 0.0))
    for lvl in range(HGRN_LEVELS):
        half = 1 << lvl
        base = (t >> (lvl + 1)) << (lvl + 1)
        if rev:
            mid = base + half
            ref_ = i >= mid
        else:
            mid = base + half - 1
            ref_ = i <= mid
        mats.append(jnp.where(cum, 1.0, 0.0) - jnp.where(ref_, 1.0, 0.0))
    return jnp.concatenate(mats, axis=0).astype(BF16)


def _hgrn_masks(rev):
    c = CHUNK
    ti = lax.broadcasted_iota(jnp.int32, (c, c), 0)
    tj = lax.broadcasted_iota(jnp.int32, (c, c), 1)
    late, early = (tj, ti) if rev else (ti, tj)
    sels = [((late >> lvl) == (early >> lvl) + 1) & (((early >> lvl) & 1) == 0)
            for lvl in range(HGRN_LEVELS)]
    mix = _hgrn_mix_matrix(rev)
    return jnp.concatenate([mix, mix, mix], axis=1), ti == tj, sels


def _hgrn_chunk(q_pre, f_pre, v, lb, st, rev, masks):
    c = CHUNK
    mix, diag, sels = masks
    q = q_pre * jax.nn.sigmoid(q_pre)
    f = lb + (1.0 - lb) * jax.nn.sigmoid(f_pre)
    kf = 1.0 - f
    g = jnp.log(f)
    ex = _exact_rowmix(mix, g)
    yield
    b = ex[:c, :]
    btot = b[0:1, :] if rev else b[c - 1:c, :]
    attn = jnp.where(diag, _dot_nt(q, kf), 0.0)
    for lvl in range(HGRN_LEVELS):
        dq = ex[(lvl + 1) * c:(lvl + 2) * c, :]
        wgt = jnp.exp(-jnp.abs(dq))
        attn = attn + jnp.where(sels[lvl], _dot_nt(q * wgt, kf * wgt), 0.0)
    yield
    o = _dot_nt(q * jnp.exp(b), st) + _dot(attn, v)
    st_new = st * jnp.exp(btot) + _dot_tn(v, kf * jnp.exp(btot - b))
    return o, st_new


def _hgrn_scan_kernel(*refs, has_init, emit_final, nsteps, nheads, cols, aliased):
    it = iter(refs)
    qf_ref, ff_ref, vf_ref = next(it), next(it), next(it)
    qb_ref, fb_ref, vb_ref = next(it), next(it), next(it)
    lb_ref = next(it)
    s0_ref = next(it) if has_init else None
    if aliased:
        next(it), next(it)
    of_ref, ob_ref = next(it), next(it)
    sfin_ref = next(it) if emit_final else None
    s_scr = next(it)
    in_scr, out_scr = (next(it), next(it)) if cols is not None else (None, None)
    ci = pl.program_id(2)

    @pl.when(ci == 0)
    def _():
        if has_init:
            for d in range(2):
                for h in range(nheads):
                    s_scr[d, h] = s0_ref[d, h].T
        else:
            s_scr[...] = jnp.zeros_like(s_scr)

    ins = ((qf_ref, ff_ref, vf_ref), (qb_ref, fb_ref, vb_ref))
    outs = (of_ref, ob_ref)
    all_masks = [_hgrn_masks(rev=False), _hgrn_masks(rev=True)]

    def advance(load, store):
        chains, sinks = [], []
        for d in range(2):
            for h in range(nheads):
                sl = slice(h * LANES, (h + 1) * LANES)
                qp, fp, v = (load(d, k, sl) for k in range(3))
                chains.append(_hgrn_chunk(qp, fp, v, lb_ref[:, sl], s_scr[d, h], rev=(d == 1),
                                          masks=all_masks[d]))
                sinks.append((d, h, sl))
        for (o, s_new), (d, h, sl) in zip(_lockstep(chains), sinks):
            store(d, sl, o)
            s_scr[d, h] = s_new

    if cols is None:
        def store_rows(d, sl, o):
            outs[d][:, sl] = o
        advance(lambda d, k, sl: ins[d][k][:, sl], store_rows)
    else:
        for d in range(2):
            for k in range(3):
                for j in range(cols):
                    in_scr[d, k, j] = ins[d][k][:, j, :]

        def body(j, carry):
            col = (j, cols - 1 - j)

            def store_col(d, sl, o):
                out_scr[d, col[d], :, sl] = o
            advance(lambda d, k, sl: in_scr[d, k, col[d], :, sl], store_col)
            return carry
        lax.fori_loop(0, cols, body, 0)
        for d in range(2):
            for j in range(cols):
                outs[d][:, j, :] = out_scr[d, j]

    if emit_final:
        @pl.when(ci == nsteps - 1)
        def _():
            for d in range(2):
                for h in range(nheads):
                    sfin_ref[d, h] = s_scr[d, h].T


HGRN_COLS = 8


def _hgrn_scan(zhs, lb, s0, *, row_off, nb, t, dg, emit_final, nheads, column_major,
               into=None):
    m = zhs.shape[0]
    ng = dg // (LANES * nheads)
    has_init = s0 is not None
    lanes = LANES * nheads
    if column_major:
        nrow = t // GRID_W
        assert nrow == CHUNK and m % t == 0 and row_off % t == 0
        nsteps = GRID_W // HGRN_COLS
        seq0 = row_off // t
        zsrc = zhs.reshape(m // t, nrow, GRID_W, 4 * dg)
        blk = (None, nrow, HGRN_COLS, lanes)
        fwd = lambda g: (lambda b, h, c: (b + seq0, 0, c, g * ng + h))
        bwd = lambda g: (lambda b, h, c: (b + seq0, 0, nsteps - 1 - c, g * ng + h))
        out_dims = (m // t, nrow, GRID_W, dg)
        ofwd = lambda b, h, c: (b + seq0, 0, c, h)
        obwd = lambda b, h, c: (b + seq0, 0, nsteps - 1 - c, h)
        cols = HGRN_COLS
    else:
        nsteps = t // CHUNK
        rb = row_off // CHUNK
        zsrc = zhs
        blk = (CHUNK, lanes)
        fwd = lambda g: (lambda b, h, c: (rb + b * nsteps + c, g * ng + h))
        bwd = lambda g: (lambda b, h, c: (rb + b * nsteps + nsteps - 1 - c, g * ng + h))
        out_dims = (m, dg)
        ofwd = lambda b, h, c: (rb + b * nsteps + c, h)
        obwd = lambda b, h, c: (rb + b * nsteps + nsteps - 1 - c, h)
        cols = None

    in_specs = [pl.BlockSpec(blk, fwd(0)), pl.BlockSpec(blk, fwd(1)), pl.BlockSpec(blk, fwd(3)),
                pl.BlockSpec(blk, bwd(0)), pl.BlockSpec(blk, bwd(2)), pl.BlockSpec(blk, bwd(3)),
                pl.BlockSpec((1, lanes), lambda b, h, c: (0, h))]
    args = [zsrc] * 6 + [lb]
    sblk = pl.BlockSpec((None, 2, nheads, LANES, LANES), lambda b, h, c: (b, 0, h, 0, 0))
    if has_init:
        in_specs.append(sblk)
        args.append(s0)
    aliases = {}
    if into is not None:
        aliases = {len(args): 0, len(args) + 1: 1}
        in_specs += [pl.BlockSpec(memory_space=pl.ANY)] * 2
        args += [o.reshape(out_dims) for o in into]
    out_shape = [jax.ShapeDtypeStruct(out_dims, F32)] * 2
    out_specs = [pl.BlockSpec(blk, ofwd), pl.BlockSpec(blk, obwd)]
    if emit_final:
        out_shape.append(jax.ShapeDtypeStruct((nb, 2, dg // LANES, LANES, LANES), F32))
        out_specs.append(sblk)
    kern = functools.partial(_hgrn_scan_kernel, has_init=has_init, emit_final=emit_final,
                             nsteps=nsteps, nheads=nheads, cols=cols, aliased=into is not None)
    scratch = [pltpu.VMEM((2, nheads, LANES, LANES), F32)]
    if column_major:
        scratch += [pltpu.VMEM((2, 3, cols, CHUNK, lanes), F32),
                    pltpu.VMEM((2, cols, CHUNK, lanes), F32)]
    outs = pl.pallas_call(
        kern, out_shape=out_shape, grid=(nb, ng, nsteps), in_specs=in_specs, out_specs=out_specs,
        scratch_shapes=scratch, input_output_aliases=aliases,
        compiler_params=_cparams("arbitrary", "arbitrary", "arbitrary"), name="hgrn_scan",
    )(*args)
    return [outs[0].reshape(m, dg), outs[1].reshape(m, dg)] + list(outs[2:])


def _hgrn_post_kernel(of_ref, ob_ref, gate_ref, gain_ref, y_ref, *, dg):
    gain = gain_ref[...]
    for j in range(dg // LANES):
        sl = slice(j * LANES, (j + 1) * LANES)
        o = of_ref[:, sl] + ob_ref[:, sl]
        on = o * lax.rsqrt(jnp.mean(o * o, axis=-1, keepdims=True) + RMS_EPS) * gain
        gt = gate_ref[:, sl]
        y_ref[:, sl] = (on * (gt * jax.nn.sigmoid(gt))).astype(y_ref.dtype)


def _hgrn_post(o_f, o_b, z_gates, gain, *, gate_col):
    rows_all, dg = o_f.shape
    kern = functools.partial(_hgrn_post_kernel, dg=dg)
    gcb = gate_col // dg
    par = pl.BlockSpec((1, LANES), lambda *_: (0, 0))
    tm = min(256, rows_all)
    ospec = pl.BlockSpec((tm, dg), lambda i: (i, 0))
    return pl.pallas_call(
        kern, out_shape=jax.ShapeDtypeStruct((rows_all, dg), BF16), grid=(rows_all // tm,),
        in_specs=[ospec, ospec, pl.BlockSpec((tm, dg), lambda i: (i, gcb)), par],
        out_specs=ospec,
        compiler_params=_cparams("arbitrary"), name="hgrn_post",
    )(o_f, o_b, z_gates, gain)


def _largest_divisor(n, candidates):
    for cand in candidates:
        if n % cand == 0:
            return cand
    raise ValueError(f"no tile in {candidates} divides {n}")


def _pad_to(x, axis, size):
    pad = size - x.shape[axis]
    if pad == 0:
        return x
    widths = [(0, 0)] * x.ndim
    widths[axis] = (0, pad)
    return jnp.pad(x, widths)


def _pair_state_to_bd(s):
    b, two, h, n, _ = s.shape
    s = s.reshape(b, two, h // 2, 2, n, n)
    z = jnp.zeros_like(s[:, :, :, 0])
    top = jnp.concatenate([s[:, :, :, 0], z], axis=-1)
    bot = jnp.concatenate([z, s[:, :, :, 1]], axis=-1)
    return jnp.concatenate([top, bot], axis=-2)


def _bd_to_pair_state(sbd):
    b, two, hp, n2, _ = sbd.shape
    n = n2 // 2
    s0 = sbd[:, :, :, :n, :n]
    s1 = sbd[:, :, :, n:, n:]
    return jnp.stack([s0, s1], axis=3).reshape(b, two, hp * 2, n, n)


def kernel(x_prompt, x_sample, state_rwkv, state_hgrn, c, c_ctx, w_mod, b_mod, norm_ffn1, norm_mix, norm_ffn2, ffn1_w_in, ffn1_w_out, ffn2_w_in, ffn2_w_out, w_in, rwkv_mu_prev, rwkv_mu_next, rwkv_w0, rwkv_w2, rwkv_a0, rwkv_a2, rwkv_g2, rwkv_k_k, rwkv_k_a, rwkv_r_k, rwkv_ln_w, rwkv_ln_b, hgrn_lb_logits, hgrn_norm, w_branch_rwkv, w_branch_hgrn, w_out, norm_final):
    nbc, tc, d = x_prompt.shape
    nbl, tl, _ = x_sample.shape
    depth = w_mod.shape[0]
    mc, ml = nbc * tc, nbl * tl
    m = mc + ml
    dims = (mc, tc, tl, nbl)
    dr = rwkv_w0.shape[-1]
    dg = w_branch_hgrn.shape[1]
    dff = ffn1_w_out.shape[1]
    wl, al, gl = rwkv_w2.shape[2], rwkv_a2.shape[2], rwkv_g2.shape[1]
    glp = -(-gl // LANES) * LANES
    rw_cols = 3 * dr + 2 * wl + 2 * al + gl
    zr_cols = 3 * dr + 2 * wl + 2 * al + glp
    assert nbl < MOD_ROWS and tl % GRID_W == 0 and tl // GRID_W == CHUNK and tc % CHUNK == 0

    tm_norm = min(256, tc)
    tm_mm = min(512, mc)
    tm_proj = min(1024, mc)
    tn_of = lambda n: _largest_divisor(n, (1024, 512, 256, 128))
    tf = _largest_divisor(dff, (256, 128))

    lb_all = jnp.cumsum(jax.nn.softmax(hgrn_lb_logits.astype(F32), axis=0), axis=0)
    cond = jnp.concatenate([c, c_ctx[None, :], jnp.zeros((MOD_ROWS - nbl - 1, d), F32)], axis=0)

    x = (x_prompt.reshape(mc, d), x_sample.reshape(ml, d))
    new_r, new_h = [], []
    for l in range(depth):
        mod3 = _modulation(cond, w_mod[l], b_mod[l], tn_of(N_MOD * d)).reshape(MOD_ROWS, N_MOD, d)

        h = _resnorm(x, None, mod3, norm_ffn1[l], dims, res_scale=None, gate_idx=None, mod_idx=0,
                     emit_x=False, out_dtype=BF16, tm=tm_norm)[0]
        delta = _ffn(h, ffn1_w_in[l].astype(BF16), ffn1_w_out[l].astype(BF16), tm=tm_proj, tf=tf)

        x, h = _resnorm(x, delta, mod3, norm_mix[l], dims, res_scale=FFN_HALF, gate_idx=2,
                        mod_idx=3, emit_x=True, out_dtype=BF16, tm=tm_norm)
        wi = w_in[l]
        hg0 = rw_cols
        w_r = _pad_to(wi[:, :rw_cols], 1, zr_cols).astype(BF16)
        w_hs = wi[:, hg0:hg0 + 4 * dg].astype(BF16)
        w_gates = wi[:, hg0 + 4 * dg:].astype(BF16)
        col_gate, col_gr, col_gh = 0, dg, dg + d
        z_r = _matmul(h, w_r, out_dtype=F32, tm=tm_proj, tn=tn_of(zr_cols), name="proj_rwkv")
        z_g = _matmul(h, w_gates, out_dtype=F32, tm=tm_proj, tn=tn_of(dg), name="proj_gates")
        zhs = _matmul(h, w_hs, out_dtype=F32, tm=tm_proj, tn=tn_of(dg), name="proj_hgrn")

        p = {
            "mu_prev": _pad_to(rwkv_mu_prev[l][None, :], 1, zr_cols),
            "mu_next": _pad_to(rwkv_mu_next[l][None, :], 1, zr_cols),
            "w0": rwkv_w0[l], "w2": rwkv_w2[l].astype(BF16),
            "a0": rwkv_a0[l], "a2": rwkv_a2[l].astype(BF16),
            "g2": _pad_to(rwkv_g2[l], 0, glp).astype(BF16),
            "k_k": rwkv_k_k[l][None, :], "k_a": rwkv_k_a[l][None, :],
            "r_k": rwkv_r_k[l].reshape(1, dr),
            "ln_w": rwkv_ln_w[l][None, :], "ln_b": rwkv_ln_b[l][None, :],
        }
        names = ("r", "k", "v", "kk", "a_f", "a_b", "lw_f", "lw_b", "g")
        pre = dict(zip(names, _rwkv_prep(z_r, p, dims, tm=min(128, tc), zr_cols=zr_cols, dr=dr,
                                         wl=wl, al=al, gl=glp)))
        npairs = _largest_divisor(dr // LANES, (8, 4, 2, 1))
        of_c, ob_c, sr_c = _rwkv_scan(pre, p["k_a"], None, row_off=0, nb=nbc, t=tc,
                                      emit_final=True, npairs=npairs)
        o_f, o_b = _rwkv_scan(pre, p["k_a"], _pair_state_to_bd(state_rwkv[:, l].astype(F32)),
                              row_off=mc, nb=nbl, t=tl, emit_final=False, npairs=npairs,
                              into=(of_c, ob_c))
        y_r = _rwkv_post(o_f, o_b, pre, p, tm=min(256, tc))
        new_r.append(_bd_to_pair_state(sr_c))

        lb = lb_all[l][None, :]
        gain_h = hgrn_norm[l][None, :]
        nheads = _largest_divisor(dg // LANES, (8, 4, 2, 1))
        nheads_cols = _largest_divisor(dg // LANES, (4, 2, 1))
        hf_c, hb_c, sh_c = _hgrn_scan(zhs, lb, None, row_off=0, nb=nbc, t=tc, dg=dg,
                                      emit_final=True, nheads=nheads, column_major=False)
        h_f, h_b = _hgrn_scan(zhs, lb, state_hgrn[:, l].astype(F32), row_off=mc, nb=nbl, t=tl,
                              dg=dg, emit_final=False, nheads=nheads_cols, column_major=True,
                              into=(hf_c, hb_c))
        y_h = _hgrn_post(h_f, h_b, z_g, gain_h, gate_col=col_gate)
        new_h.append(sh_c)

        merged = _merge(y_r, y_h, w_branch_rwkv[l].astype(BF16), w_branch_hgrn[l].astype(BF16),
                        z_g, col_gr, col_gh, tm=tm_mm, tn=tn_of(dg))
        delta = _matmul(merged, w_out[l].astype(BF16), out_dtype=F32, tm=tm_proj, tn=tn_of(d),
                        name="mix_out")

        x, h = _resnorm(x, delta, mod3, norm_ffn2[l], dims, res_scale=1.0, gate_idx=5, mod_idx=6,
                        emit_x=True, out_dtype=BF16, tm=tm_norm)
        delta = _ffn(h, ffn2_w_in[l].astype(BF16), ffn2_w_out[l].astype(BF16), tm=tm_proj, tf=tf)
        last = l == depth - 1
        if not last:
            x = _resnorm(x, delta, mod3, norm_ffn2[l], dims, res_scale=FFN_HALF, gate_idx=8,
                         mod_idx=None, emit_x=True, out_dtype=F32, tm=tm_norm)[0]

    y_c = _resnorm(x, delta, mod3, norm_final, dims, res_scale=FFN_HALF, gate_idx=8, mod_idx=None,
                   emit_x=False, out_dtype=F32, tm=tm_norm, row_off=0, rows=mc)[0]
    y_l = _resnorm(x, delta, mod3, norm_final, dims, res_scale=FFN_HALF, gate_idx=8, mod_idx=None,
                   emit_x=False, out_dtype=F32, tm=tm_norm, row_off=mc, rows=ml)[0]
    return (y_c.reshape(nbc, tc, d), y_l.reshape(nbl, tl, d),
            jnp.stack(new_r, axis=1), jnp.stack(new_h, axis=1))
```

```python
import functools
import math

import jax
import jax.numpy as jnp
from jax import lax
from jax.experimental import pallas as pl
from jax.experimental.pallas import tpu as pltpu

F32 = jnp.float32
BF16 = jnp.bfloat16

CHUNK = 64
GRID_W = 64
RWKV_HEAD = 64
HGRN_EXPAND = 128
N_MOD = 9
RMS_EPS = 1e-6
GN_EPS = 64e-5
FFN_HALF = 0.5
DECAY_SCALE = math.exp(-0.5)
LANES = 128
MOD_ROWS = 16
VMEM_LIMIT_BYTES = 56 * 1024 * 1024


def _cparams(*sem):
    return pltpu.CompilerParams(dimension_semantics=sem, vmem_limit_bytes=VMEM_LIMIT_BYTES)


def _dot(a, b):
    return jnp.dot(a.astype(BF16), b.astype(BF16), preferred_element_type=F32)


def _dot_nt(a, b):
    return lax.dot_general(a.astype(BF16), b.astype(BF16), (((1,), (1,)), ((), ())),
                           preferred_element_type=F32)


def _dot_tn(a, b):
    return jnp.dot(a.T.astype(BF16), b.astype(BF16), preferred_element_type=F32)


def _split2(x):
    hi = x.astype(BF16)
    return hi, (x - hi.astype(F32)).astype(BF16)


def _split3(x):
    x1 = x.astype(BF16)
    d1 = x - x1.astype(F32)
    x2 = d1.astype(BF16)
    x3 = (d1 - x2.astype(F32)).astype(BF16)
    return x1, x2, x3


def _exact_rowmix(mat3, x):
    return jnp.dot(mat3, jnp.concatenate(_split3(x), axis=0), preferred_element_type=F32)


def _group_sum(x, bd_ones):
    x1, x2 = _split2(x)
    return (jnp.dot(x1, bd_ones, preferred_element_type=F32)
            + jnp.dot(x2, bd_ones, preferred_element_type=F32))


def _lockstep(chains):
    results = [None] * len(chains)
    active = list(enumerate(chains))
    while active:
        still = []
        for idx, chain in active:
            try:
                next(chain)
                still.append((idx, chain))
            except StopIteration as done:
                results[idx] = done.value
        active = still
    return results


def _mod_kernel(c_ref, w_ref, b_ref, o_ref):
    c = c_ref[...]
    s = c * jax.nn.sigmoid(c)
    o_ref[...] = _dot(s, w_ref[...]) + b_ref[...]


def _modulation(cond, w_mod, b_mod, tn):
    d, n = w_mod.shape
    return pl.pallas_call(
        _mod_kernel,
        out_shape=jax.ShapeDtypeStruct((MOD_ROWS, n), F32),
        grid=(n // tn,),
        in_specs=[pl.BlockSpec((MOD_ROWS, d), lambda j: (0, 0)),
                  pl.BlockSpec((d, tn), lambda j: (0, j)),
                  pl.BlockSpec((1, tn), lambda j: (0, j))],
        out_specs=pl.BlockSpec((MOD_ROWS, tn), lambda j: (0, j)),
        compiler_params=_cparams("arbitrary"),
        name="modulation",
    )(cond, w_mod, b_mod.reshape(1, n))


def _resnorm_kernel(*refs, has_delta, res_scale, gate_idx, mod_idx, emit_x, ctx_tiles):
    it = iter(refs)
    x_ref = next(it)
    xl_ref = next(it) if ctx_tiles is not None else None
    delta_ref = next(it) if has_delta else None
    mod_ref = next(it) if (has_delta or mod_idx is not None) else None
    gain_ref = next(it)
    xo_ref = next(it) if emit_x else None
    h_ref = next(it)
    x = x_ref[...]
    if ctx_tiles is not None:
        x = jnp.where(pl.program_id(0) < ctx_tiles, x, xl_ref[...])
    if has_delta:
        g = mod_ref[gate_idx:gate_idx + 1, :]
        x = x + g * (res_scale * delta_ref[...])
    if emit_x:
        xo_ref[...] = x
    y = x * lax.rsqrt(jnp.mean(x * x, axis=-1, keepdims=True) + RMS_EPS)
    y = y * gain_ref[...]
    if mod_idx is not None:
        sh = mod_ref[mod_idx:mod_idx + 1, :]
        sc = mod_ref[mod_idx + 1:mod_idx + 2, :]
        y = y * (1.0 + sc) + sh
    h_ref[...] = y.astype(h_ref.dtype)


def _resnorm(x, delta, mod3, gain, dims, *, res_scale, gate_idx, mod_idx, emit_x, out_dtype,
             tm, row_off=0, rows=None):
    mc, tc, tl, nlb = dims
    n_ctx_tiles = mc // tm
    tiles_per_lat = tl // tm
    ro = row_off // tm
    split_x = isinstance(x, tuple)
    if split_x:
        assert row_off == 0 and rows is None
        d = x[0].shape[1]
        m_all = x[0].shape[0] + x[1].shape[0]
        in_specs = [pl.BlockSpec((tm, d), lambda i: (jnp.minimum(i, n_ctx_tiles - 1), 0)),
                    pl.BlockSpec((tm, d), lambda i: (jnp.maximum(i - n_ctx_tiles, 0), 0))]
        args = list(x)
    else:
        d = x.shape[1]
        m_all = rows if rows is not None else x.shape[0]
        in_specs = [pl.BlockSpec((tm, d), lambda i: (i + ro, 0))]
        args = [x]

    def mod_row(i):
        gi = i + ro
        return jnp.where(gi < n_ctx_tiles, nlb, (gi - n_ctx_tiles) // tiles_per_lat)

    has_delta = delta is not None
    if has_delta:
        in_specs.append(pl.BlockSpec((tm, d), lambda i: (i + ro, 0)))
        args.append(delta)
    if has_delta or mod_idx is not None:
        in_specs.append(pl.BlockSpec((None, N_MOD, d), lambda i: (mod_row(i), 0, 0)))
        args.append(mod3)
    in_specs.append(pl.BlockSpec((1, d), lambda i: (0, 0)))
    args.append(gain.reshape(1, d))
    out_shape, out_specs = [], []
    if emit_x:
        out_shape.append(jax.ShapeDtypeStruct((m_all, d), F32))
        out_specs.append(pl.BlockSpec((tm, d), lambda i: (i, 0)))
    out_shape.append(jax.ShapeDtypeStruct((m_all, d), out_dtype))
    out_specs.append(pl.BlockSpec((tm, d), lambda i: (i, 0)))
    kern = functools.partial(_resnorm_kernel, has_delta=has_delta, res_scale=res_scale,
                             gate_idx=gate_idx, mod_idx=mod_idx, emit_x=emit_x,
                             ctx_tiles=n_ctx_tiles if split_x else None)
    return pl.pallas_call(
        kern, out_shape=out_shape, grid=(m_all // tm,), in_specs=in_specs, out_specs=out_specs,
        compiler_params=_cparams("arbitrary"), name="resnorm",
    )(*args)


def _mm_kernel(a_ref, w_ref, o_ref):
    o_ref[...] = jnp.dot(a_ref[...], w_ref[...], preferred_element_type=F32).astype(o_ref.dtype)


def _matmul(a, w, *, out_dtype, tm, tn, row_off=0, rows=None, name="matmul"):
    k = a.shape[1]
    n = w.shape[1]
    rows = a.shape[0] if rows is None else rows
    ro = row_off // tm
    return pl.pallas_call(
        _mm_kernel,
        out_shape=jax.ShapeDtypeStruct((rows, n), out_dtype),
        grid=(rows // tm, n // tn),
        in_specs=[pl.BlockSpec((tm, k), lambda i, j: (i + ro, 0)),
                  pl.BlockSpec((k, tn), lambda i, j: (0, j))],
        out_specs=pl.BlockSpec((tm, tn), lambda i, j: (i, j)),
        compiler_params=_cparams("arbitrary", "arbitrary"), name=name,
    )(a, w)


def _ffn_kernel(h_ref, wa_ref, wb_ref, wo_ref, o_ref):
    @pl.when(pl.program_id(1) == 0)
    def _():
        o_ref[...] = jnp.zeros_like(o_ref)

    h = h_ref[...]
    a = jnp.dot(h, wa_ref[...], preferred_element_type=F32)
    b = jnp.dot(h, wb_ref[...], preferred_element_type=F32)
    u = (a * jax.nn.sigmoid(a) * b).astype(BF16)
    o_ref[...] += jnp.dot(u, wo_ref[...], preferred_element_type=F32)


def _ffn(h, w_in, w_out, *, tm, tf):
    m, d = h.shape
    dff = w_out.shape[0]
    nf = dff // tf
    return pl.pallas_call(
        _ffn_kernel,
        out_shape=jax.ShapeDtypeStruct((m, d), F32),
        grid=(m // tm, nf),
        in_specs=[pl.BlockSpec((tm, d), lambda i, j: (i, 0)),
                  pl.BlockSpec((d, tf), lambda i, j: (0, j)),
                  pl.BlockSpec((d, tf), lambda i, j: (0, j + nf)),
                  pl.BlockSpec((tf, d), lambda i, j: (j, 0))],
        out_specs=pl.BlockSpec((tm, d), lambda i, j: (i, 0), pipeline_mode=pl.Buffered(1)),
        compiler_params=_cparams("arbitrary", "arbitrary"), name="ffn",
    )(h, w_in, w_in, w_out)


def _merge_kernel(yr_ref, yh_ref, wr_ref, wh_ref, gr_ref, gh_ref, o_ref):
    pr = jnp.dot(yr_ref[...], wr_ref[...], preferred_element_type=F32)
    ph = jnp.dot(yh_ref[...], wh_ref[...], preferred_element_type=F32)
    o_ref[...] = (jax.nn.sigmoid(gr_ref[...]) * pr
                  + jax.nn.sigmoid(gh_ref[...]) * ph).astype(o_ref.dtype)


def _merge(y_r, y_h, w_br, w_bh, z_rest, col_gr, col_gh, *, tm, tn):
    m, kr = y_r.shape
    kh = y_h.shape[1]
    d = w_br.shape[1]
    cgr, cgh = col_gr // tn, col_gh // tn
    return pl.pallas_call(
        _merge_kernel,
        out_shape=jax.ShapeDtypeStruct((m, d), BF16),
        grid=(m // tm, d // tn),
        in_specs=[pl.BlockSpec((tm, kr), lambda i, j: (i, 0)),
                  pl.BlockSpec((tm, kh), lambda i, j: (i, 0)),
                  pl.BlockSpec((kr, tn), lambda i, j: (0, j)),
                  pl.BlockSpec((kh, tn), lambda i, j: (0, j)),
                  pl.BlockSpec((tm, tn), lambda i, j: (i, j + cgr)),
                  pl.BlockSpec((tm, tn), lambda i, j: (i, j + cgh))],
        out_specs=pl.BlockSpec((tm, tn), lambda i, j: (i, j)),
        compiler_params=_cparams("arbitrary", "arbitrary"), name="merge",
    )(y_r, y_h, w_br, w_bh, z_rest, z_rest)


def _log2(n):
    assert n & (n - 1) == 0
    return n.bit_length() - 1


def _head_ones(n, group):
    sh = _log2(group)
    i = lax.broadcasted_iota(jnp.int32, (n, n), 0) >> sh
    j = lax.broadcasted_iota(jnp.int32, (n, n), 1) >> sh
    return jnp.where(i == j, 1.0, 0.0).astype(BF16)


def _rwkv_prep_kernel(z_ref, zp_ref, zn_ref, mup_ref, mun_ref, w0_ref, w2_ref, a0_ref, a2_ref,
                      g2_ref, kk_ref, ka_ref, rk_ref,
                      r_o, k_o, v_o, kkn_o, af_o, ab_o, lwf_o, lwb_o, g_o, bonus_o,
                      *, tm, mc, tc, tl, dr, wl, al, gl):
    row0 = pl.program_id(0) * tm
    in_ctx = row0 < mc
    pos = jnp.where(in_ctx, row0 % tc, (row0 - mc) % tl)
    seqlen = jnp.where(in_ctx, tc, tl)
    keep_prev = jnp.where(pos == 0, 0.0, 1.0)
    keep_next = jnp.where(pos + tm == seqlen, 0.0, 1.0)
    rows = lax.broadcasted_iota(jnp.int32, (tm, 1), 0)

    def shifted(c0, c1):
        z = z_ref[:, c0:c1]
        zp = jnp.where(rows == 0, keep_prev * zp_ref[7:8, c0:c1], pltpu.roll(z, 1, 0))
        zn = jnp.where(rows == tm - 1, keep_next * zn_ref[0:1, c0:c1], pltpu.roll(z, tm - 1, 0))
        return z + mup_ref[:, c0:c1] * (zp - z) + mun_ref[:, c0:c1] * (zn - z)

    bd = _head_ones(LANES, RWKV_HEAD)
    for j in range(dr // LANES):
        c0 = j * LANES
        r_o[:, c0:c0 + LANES] = shifted(c0, c0 + LANES)
        k = shifted(dr + c0, dr + c0 + LANES)
        k_o[:, c0:c0 + LANES] = k
        v_o[:, c0:c0 + LANES] = shifted(2 * dr + c0, 2 * dr + c0 + LANES)
        kk = k * kk_ref[:, c0:c0 + LANES]
        ss = _group_sum(kk * kk, bd)
        kkn_o[:, c0:c0 + LANES] = kk * lax.rsqrt(jnp.maximum(ss, 1e-24))

    c = 3 * dr
    for d, (lw_o, a_o) in enumerate(((lwf_o, af_o), (lwb_o, ab_o))):
        wd = shifted(c + d * wl, c + (d + 1) * wl)
        ad = shifted(c + 2 * wl + d * al, c + 2 * wl + (d + 1) * al)
        wpre = w0_ref[d:d + 1, :] + _dot(jnp.tanh(wd), w2_ref[d])
        lw_o[...] = -DECAY_SCALE * jax.nn.sigmoid(wpre)
        a_o[...] = jax.nn.sigmoid(a0_ref[d:d + 1, :] + _dot(ad, a2_ref[d]))
    gd = shifted(c + 2 * wl + 2 * al, c + 2 * wl + 2 * al + gl)
    g_o[...] = _dot(jax.nn.sigmoid(gd), g2_ref[...])

    for j in range(dr // LANES):
        sl = slice(j * LANES, (j + 1) * LANES)
        r, k, v = r_o[:, sl], k_o[:, sl], v_o[:, sl]
        bonus = 0.0
        for a_o in (af_o, ab_o):
            kd = k * (1.0 + (a_o[:, sl] - 1.0) * ka_ref[:, sl])
            bonus = bonus + _group_sum(r * kd * rk_ref[:, sl], bd) * v
        bonus_o[:, sl] = bonus


def _rwkv_prep(z_rest, p, dims, *, tm, zr_cols, dr, wl, al, gl):
    mc, tc, tl, nlb = dims
    m = z_rest.shape[0]
    nblk8 = m // 8
    t8 = tm // 8
    full = lambda shape: pl.BlockSpec(shape, lambda i: (0,) * len(shape))
    outs = [jax.ShapeDtypeStruct((m, dr), F32)] * 10
    kern = functools.partial(_rwkv_prep_kernel, tm=tm, mc=mc, tc=tc, tl=tl, dr=dr, wl=wl, al=al,
                             gl=gl)
    return pl.pallas_call(
        kern, out_shape=outs, grid=(m // tm,),
        in_specs=[pl.BlockSpec((tm, zr_cols), lambda i: (i, 0)),
                  pl.BlockSpec((8, zr_cols), lambda i: (jnp.maximum(i * t8 - 1, 0), 0)),
                  pl.BlockSpec((8, zr_cols), lambda i: (jnp.minimum((i + 1) * t8, nblk8 - 1), 0)),
                  full((1, zr_cols)), full((1, zr_cols)),
                  full((2, dr)), full((2, wl, dr)), full((2, dr)), full((2, al, dr)),
                  full((gl, dr)), full((1, dr)), full((1, dr)), full((1, dr))],
        out_specs=[pl.BlockSpec((tm, dr), lambda i: (i, 0))] * 10,
        compiler_params=_cparams("arbitrary"), name="rwkv_prep",
    )(z_rest, z_rest, z_rest, p["mu_prev"], p["mu_next"], p["w0"], p["w2"], p["a0"], p["a2"],
      p["g2"], p["k_k"], p["k_a"], p["r_k"])


def _stack_heads(x, lo):
    return jnp.concatenate([jnp.where(lo, x, 0.0), jnp.where(lo, 0.0, x)], axis=0)


SUB_BLOCK = 16
assert CHUNK // SUB_BLOCK == 4


def _rwkv_masks(rev):
    c = CHUNK
    n2 = 2 * c
    ti = lax.broadcasted_iota(jnp.int32, (c, c), 0)
    tj = lax.broadcasted_iota(jnp.int32, (c, c), 1)
    cum = jnp.where((tj >= ti) if rev else (tj <= ti), 1.0, 0.0).astype(BF16)
    cum = jnp.concatenate([cum, cum, cum], axis=1)
    lo = lax.broadcasted_iota(jnp.int32, (c, LANES), 1) < RWKV_HEAD
    ii = lax.broadcasted_iota(jnp.int32, (n2, n2), 0)
    jj = lax.broadcasted_iota(jnp.int32, (n2, n2), 1)
    im, jm = ii & (c - 1), jj & (c - 1)
    strict = (jm > im) if rev else (jm < im)
    incl = (jm >= im) if rev else (jm <= im)
    eye = jnp.where(ii == jj, 1.0, 0.0)
    sub = (ii >> _log2(SUB_BLOCK)) == (jj >> _log2(SUB_BLOCK))
    return cum, lo, strict, incl, eye, sub


def _rwkv_chunk(r, k, v, kk, a, lw, k_a, s, rev, masks):
    c = CHUNK
    n2 = 2 * c
    cum, lo, strict, incl, eye, sub = masks
    bc = _exact_rowmix(cum, lw)
    yield
    bex = bc - lw
    btot = bc[0:1, :] if rev else bc[c - 1:c, :]
    kd = k * (1.0 + (a - 1.0) * k_a)
    bv = kk * a
    e_neg = jnp.exp(-bc)
    e_tot = jnp.exp(btot - bc)
    at = _stack_heads(-kk * jnp.exp(bex), lo).astype(BF16)
    rt_f = _stack_heads(r * jnp.exp(bc), lo)
    kh = _stack_heads(kd * e_neg, lo).astype(BF16)
    bh = _stack_heads(bv * e_neg, lo).astype(BF16)
    vb_f = _stack_heads(v, lo)
    vb = vb_f.astype(BF16)
    kc = _stack_heads(kd * e_tot, lo).astype(BF16)
    bcc = _stack_heads(bv * e_tot, lo).astype(BF16)

    gram = _dot_nt(jnp.concatenate([at, rt_f.astype(BF16)], axis=0),
                   jnp.concatenate([bh, kh], axis=0))
    yield
    nmat = jnp.where(strict, gram[:n2, :n2], 0.0)
    a_ka = jnp.where(strict, gram[:n2, n2:], 0.0)
    a_rb = jnp.where(incl, gram[n2:, :n2], 0.0)
    a_rk = jnp.where(incl, gram[n2:, n2:], 0.0)
    n_d = jnp.where(sub, nmat, 0.0)
    t_d = eye + n_d
    pw = n_d
    for _ in range(_log2(SUB_BLOCK) - 1):
        pw = _dot(pw, pw)
        yield
        t_d = t_d + _dot(t_d, pw)
    x1 = _dot(a_ka, vb)
    yield
    xm = _dot(t_d, jnp.where(sub, 0.0, nmat))
    yield
    xm2 = _dot(xm, xm)
    yield
    ym = eye + xm + xm2 + _dot(xm, xm2)
    yield
    tm_ = _dot(ym, t_d)
    yield
    aw = _dot(tm_, jnp.concatenate([at, x1.astype(BF16)], axis=1))
    yield
    ah, w = aw[:, :n2], aw[:, n2:]
    rhs = jnp.concatenate(
        [jnp.concatenate([vb, jnp.zeros_like(vb)], axis=1),
         jnp.concatenate([w.astype(BF16), ah.astype(BF16)], axis=1)], axis=0)
    qr = _dot(jnp.concatenate([a_rk, a_rb], axis=1), rhs)
    q = qr[:, :n2]
    rr = rt_f + qr[:, n2:]
    mm = _dot_tn(ah, bcc)
    nn = _dot_tn(jnp.concatenate([vb_f, w], axis=0), jnp.concatenate([kc, bcc], axis=0))
    yield
    o_bd = _dot_nt(rr, s) + q
    o = o_bd[:c, :] + o_bd[c:, :]
    s_new = s * jnp.exp(btot) + _dot(s, mm) + nn
    return o, s_new


def _rwkv_scan_kernel(*refs, has_init, emit_final, nchunk, npairs, aliased):
    it = iter(refs)
    fwd = [next(it) for _ in range(6)]
    bwd = [next(it) for _ in range(6)]
    ka_ref = next(it)
    s0_ref = next(it) if has_init else None
    if aliased:
        next(it), next(it)
    of_ref = next(it)
    ob_ref = next(it)
    sfin_ref = next(it) if emit_final else None
    s_scr = next(it)
    ci = pl.program_id(2)

    @pl.when(ci == 0)
    def _():
        if has_init:
            s_scr[...] = s0_ref[...]
        else:
            s_scr[...] = jnp.zeros_like(s_scr)

    chains, sinks = [], []
    for d, (ins, o_ref) in enumerate(((fwd, of_ref), (bwd, ob_ref))):
        masks = _rwkv_masks(rev=(d == 1))
        for hp in range(npairs):
            sl = slice(hp * LANES, (hp + 1) * LANES)
            r, k, v, kk, a, lw = (x[:, sl] for x in ins)
            chains.append(_rwkv_chunk(r, k, v, kk, a, lw, ka_ref[:, sl], s_scr[d, hp],
                                      rev=(d == 1), masks=masks))
            sinks.append((o_ref, sl, d, hp))
    for (o, s_new), (o_ref, sl, d, hp) in zip(_lockstep(chains), sinks):
        o_ref[:, sl] = o
        s_scr[d, hp] = s_new

    if emit_final:
        @pl.when(ci == nchunk - 1)
        def _():
            sfin_ref[...] = s_scr[...]


def _rwkv_scan(pre, k_a, s0_bd, *, row_off, nb, t, emit_final, npairs, into=None):
    m, dr = pre["r"].shape
    ngroup = dr // (LANES * npairs)
    nchunk = t // CHUNK
    rb = row_off // CHUNK
    has_init = s0_bd is not None
    fmap = lambda b, h, c: (rb + b * nchunk + c, h)
    bmap = lambda b, h, c: (rb + b * nchunk + (nchunk - 1 - c), h)
    blk = (CHUNK, LANES * npairs)
    in_specs = ([pl.BlockSpec(blk, fmap)] * 6 + [pl.BlockSpec(blk, bmap)] * 6
                + [pl.BlockSpec((1, LANES * npairs), lambda b, h, c: (0, h))])
    args = [pre["r"], pre["k"], pre["v"], pre["kk"], pre["a_f"], pre["lw_f"],
            pre["r"], pre["k"], pre["v"], pre["kk"], pre["a_b"], pre["lw_b"], k_a]
    sblk = pl.BlockSpec((None, 2, npairs, LANES, LANES), lambda b, h, c: (b, 0, h, 0, 0))
    if has_init:
        in_specs.append(sblk)
        args.append(s0_bd)
    aliases = {}
    if into is not None:
        aliases = {len(args): 0, len(args) + 1: 1}
        in_specs += [pl.BlockSpec(memory_space=pl.ANY)] * 2
        args += list(into)
    out_shape = [jax.ShapeDtypeStruct((m, dr), F32)] * 2
    out_specs = [pl.BlockSpec(blk, fmap), pl.BlockSpec(blk, bmap)]
    if emit_final:
        out_shape.append(jax.ShapeDtypeStruct((nb, 2, dr // LANES, LANES, LANES), F32))
        out_specs.append(sblk)
    kern = functools.partial(_rwkv_scan_kernel, has_init=has_init, emit_final=emit_final,
                             nchunk=nchunk, npairs=npairs, aliased=into is not None)
    return pl.pallas_call(
        kern, out_shape=out_shape, grid=(nb, ngroup, nchunk), in_specs=in_specs,
        out_specs=out_specs, scratch_shapes=[pltpu.VMEM((2, npairs, LANES, LANES), F32)],
        input_output_aliases=aliases,
        compiler_params=_cparams("arbitrary", "arbitrary", "arbitrary"), name="rwkv_scan",
    )(*args)


def _rwkv_post_kernel(of_ref, ob_ref, bonus_ref, g_ref, lnw_ref, lnb_ref, y_ref, *, dr):
    bd = _head_ones(LANES, RWKV_HEAD)
    inv = 1.0 / RWKV_HEAD
    for j in range(dr // LANES):
        sl = slice(j * LANES, (j + 1) * LANES)
        o = of_ref[:, sl] + ob_ref[:, sl]
        mean = _group_sum(o, bd) * inv
        oc = o - mean
        var = _group_sum(oc * oc, bd) * inv
        on = oc * lax.rsqrt(var + GN_EPS) * lnw_ref[:, sl] + lnb_ref[:, sl]
        y_ref[:, sl] = ((on + bonus_ref[:, sl]) * g_ref[:, sl]).astype(y_ref.dtype)


def _rwkv_post(o_f, o_b, pre, p, *, tm):
    rows, dr = o_f.shape
    own = pl.BlockSpec((tm, dr), lambda i: (i, 0))
    par = pl.BlockSpec((1, dr), lambda i: (0, 0))
    return pl.pallas_call(
        functools.partial(_rwkv_post_kernel, dr=dr),
        out_shape=jax.ShapeDtypeStruct((rows, dr), BF16),
        grid=(rows // tm,),
        in_specs=[own] * 4 + [par] * 2,
        out_specs=own,
        compiler_params=_cparams("arbitrary"), name="rwkv_post",
    )(o_f, o_b, pre["bonus"], pre["g"], p["ln_w"], p["ln_b"])


HGRN_LEVELS = 6


def _hgrn_mix_matrix(rev):
    c = CHUNK
    t = lax.broadcasted_iota(jnp.int32, (c, c), 0)
    i = lax.broadcasted_iota(jnp.int32, (c, c), 1)
    mats = []
    cum = (i >= t) if rev else (i <= t)
    mats.append(jnp.where(cum, 1.0, 0.0))
    for lvl in range(HGRN_LEVELS):
        half = 1 << lvl
        base = (t >> (lvl + 1)) << (lvl + 1)
        if rev:
            mid = base + half
            ref_ = i >= mid
        else:
            mid = base + half - 1
            ref_ = i <= mid
        mats.append(jnp.where(cum, 1.0, 0.0) - jnp.where(ref_, 1.0, 0.0))
    return jnp.concatenate(mats, axis=0).astype(BF16)


def _hgrn_masks(rev):
    c = CHUNK
    ti = lax.broadcasted_iota(jnp.int32, (c, c), 0)
    tj = lax.broadcasted_iota(jnp.int32, (c, c), 1)
    late, early = (tj, ti) if rev else (ti, tj)
    sels = [((late >> lvl) == (early >> lvl) + 1) & (((early >> lvl) & 1) == 0)
            for lvl in range(HGRN_LEVELS)]
    mix = _hgrn_mix_matrix(rev)
    return jnp.concatenate([mix, mix, mix], axis=1), ti == tj, sels


def _hgrn_chunk(q_pre, f_pre, v, lb, st, rev, masks):
    c = CHUNK
    mix, diag, sels = masks
    q = q_pre * jax.nn.sigmoid(q_pre)
    f = lb + (1.0 - lb) * jax.nn.sigmoid(f_pre)
    kf = 1.0 - f
    g = jnp.log(f)
    ex = _exact_rowmix(mix, g)
    yield
    b = ex[:c, :]
    btot = b[0:1, :] if rev else b[c - 1:c, :]
    attn = jnp.where(diag, _dot_nt(q, kf), 0.0)
    for lvl in range(HGRN_LEVELS):
        dq = ex[(lvl + 1) * c:(lvl + 2) * c, :]
        wgt = jnp.exp(-jnp.abs(dq))
        attn = attn + jnp.where(sels[lvl], _dot_nt(q * wgt, kf * wgt), 0.0)
    yield
    o = _dot_nt(q * jnp.exp(b), st) + _dot(attn, v)
    st_new = st * jnp.exp(btot) + _dot_tn(v, kf * jnp.exp(btot - b))
    return o, st_new


def _hgrn_scan_kernel(*refs, has_init, emit_final, nsteps, nheads, cols, aliased):
    it = iter(refs)
    qf_ref, ff_ref, vf_ref = next(it), next(it), next(it)
    qb_ref, fb_ref, vb_ref = next(it), next(it), next(it)
    lb_ref = next(it)
    s0_ref = next(it) if has_init else None
    if aliased:
        next(it), next(it)
    of_ref, ob_ref = next(it), next(it)
    sfin_ref = next(it) if emit_final else None
    s_scr = next(it)
    in_scr, out_scr = (next(it), next(it)) if cols is not None else (None, None)
    ci = pl.program_id(2)

    @pl.when(ci == 0)
    def _():
        if has_init:
            for d in range(2):
                for h in range(nheads):
                    s_scr[d, h] = s0_ref[d, h].T
        else:
            s_scr[...] = jnp.zeros_like(s_scr)

    ins = ((qf_ref, ff_ref, vf_ref), (qb_ref, fb_ref, vb_ref))
    outs = (of_ref, ob_ref)
    all_masks = [_hgrn_masks(rev=False), _hgrn_masks(rev=True)]

    def advance(load, store):
        chains, sinks = [], []
        for d in range(2):
            for h in range(nheads):
                sl = slice(h * LANES, (h + 1) * LANES)
                qp, fp, v = (load(d, k, sl) for k in range(3))
                chains.append(_hgrn_chunk(qp, fp, v, lb_ref[:, sl], s_scr[d, h], rev=(d == 1),
                                          masks=all_masks[d]))
                sinks.append((d, h, sl))
        for (o, s_new), (d, h, sl) in zip(_lockstep(chains), sinks):
            store(d, sl, o)
            s_scr[d, h] = s_new

    if cols is None:
        def store_rows(d, sl, o):
            outs[d][:, sl] = o
        advance(lambda d, k, sl: ins[d][k][:, sl], store_rows)
    else:
        for d in range(2):
            for k in range(3):
                in_scr[d, k] = jnp.swapaxes(ins[d][k][...], 0, 1)

        def body(j, carry):
            col = (j, cols - 1 - j)

            def store_col(d, sl, o):
                out_scr[d, col[d], :, sl] = o
            advance(lambda d, k, sl: in_scr[d, k, col[d], :, sl], store_col)
            return carry
        lax.fori_loop(0, cols, body, 0)
        for d in range(2):
            outs[d][...] = jnp.swapaxes(out_scr[d], 0, 1)

    if emit_final:
        @pl.when(ci == nsteps - 1)
        def _():
            for d in range(2):
                for h in range(nheads):
                    sfin_ref[d, h] = s_scr[d, h].T


HGRN_COLS = 8


def _hgrn_scan(zhs, lb, s0, *, row_off, nb, t, dg, emit_final, nheads, column_major,
               into=None):
    m = zhs.shape[0]
    ng = dg // (LANES * nheads)
    has_init = s0 is not None
    lanes = LANES * nheads
    if column_major:
        nrow = t // GRID_W
        assert nrow == CHUNK and m % t == 0 and row_off % t == 0
        nsteps = GRID_W // HGRN_COLS
        seq0 = row_off // t
        zsrc = zhs.reshape(m // t, nrow, GRID_W, 4 * dg)
        blk = (None, nrow, HGRN_COLS, lanes)
        fwd = lambda g: (lambda b, h, c: (b + seq0, 0, c, g * ng + h))
        bwd = lambda g: (lambda b, h, c: (b + seq0, 0, nsteps - 1 - c, g * ng + h))
        out_dims = (m // t, nrow, GRID_W, dg)
        ofwd = lambda b, h, c: (b + seq0, 0, c, h)
        obwd = lambda b, h, c: (b + seq0, 0, nsteps - 1 - c, h)
        cols = HGRN_COLS
    else:
        nsteps = t // CHUNK
        rb = row_off // CHUNK
        zsrc = zhs
        blk = (CHUNK, lanes)
        fwd = lambda g: (lambda b, h, c: (rb + b * nsteps + c, g * ng + h))
        bwd = lambda g: (lambda b, h, c: (rb + b * nsteps + nsteps - 1 - c, g * ng + h))
        out_dims = (m, dg)
        ofwd = lambda b, h, c: (rb + b * nsteps + c, h)
        obwd = lambda b, h, c: (rb + b * nsteps + nsteps - 1 - c, h)
        cols = None

    in_specs = [pl.BlockSpec(blk, fwd(0)), pl.BlockSpec(blk, fwd(1)), pl.BlockSpec(blk, fwd(3)),
                pl.BlockSpec(blk, bwd(0)), pl.BlockSpec(blk, bwd(2)), pl.BlockSpec(blk, bwd(3)),
                pl.BlockSpec((1, lanes), lambda b, h, c: (0, h))]
    args = [zsrc] * 6 + [lb]
    sblk = pl.BlockSpec((None, 2, nheads, LANES, LANES), lambda b, h, c: (b, 0, h, 0, 0))
    if has_init:
        in_specs.append(sblk)
        args.append(s0)
    aliases = {}
    if into is not None:
        aliases = {len(args): 0, len(args) + 1: 1}
        in_specs += [pl.BlockSpec(memory_space=pl.ANY)] * 2
        args += [o.reshape(out_dims) for o in into]
    out_shape = [jax.ShapeDtypeStruct(out_dims, F32)] * 2
    out_specs = [pl.BlockSpec(blk, ofwd), pl.BlockSpec(blk, obwd)]
    if emit_final:
        out_shape.append(jax.ShapeDtypeStruct((nb, 2, dg // LANES, LANES, LANES), F32))
        out_specs.append(sblk)
    kern = functools.partial(_hgrn_scan_kernel, has_init=has_init, emit_final=emit_final,
                             nsteps=nsteps, nheads=nheads, cols=cols, aliased=into is not None)
    scratch = [pltpu.VMEM((2, nheads, LANES, LANES), F32)]
    if column_major:
        scratch += [pltpu.VMEM((2, 3, cols, CHUNK, lanes), F32),
                    pltpu.VMEM((2, cols, CHUNK, lanes), F32)]
    outs = pl.pallas_call(
        kern, out_shape=out_shape, grid=(nb, ng, nsteps), in_specs=in_specs, out_specs=out_specs,
        scratch_shapes=scratch, input_output_aliases=aliases,
        compiler_params=_cparams("arbitrary", "arbitrary", "arbitrary"), name="hgrn_scan",
    )(*args)
    return [outs[0].reshape(m, dg), outs[1].reshape(m, dg)] + list(outs[2:])


def _hgrn_post_kernel(of_ref, ob_ref, gate_ref, gain_ref, y_ref, *, dg):
    gain = gain_ref[...]
    for j in range(dg // LANES):
        sl = slice(j * LANES, (j + 1) * LANES)
        o = of_ref[:, sl] + ob_ref[:, sl]
        on = o * lax.rsqrt(jnp.mean(o * o, axis=-1, keepdims=True) + RMS_EPS) * gain
        gt = gate_ref[:, sl]
        y_ref[:, sl] = (on * (gt * jax.nn.sigmoid(gt))).astype(y_ref.dtype)


def _hgrn_post(o_f, o_b, z_gates, gain, *, gate_col):
    rows_all, dg = o_f.shape
    kern = functools.partial(_hgrn_post_kernel, dg=dg)
    gcb = gate_col // dg
    par = pl.BlockSpec((1, LANES), lambda *_: (0, 0))
    tm = min(256, rows_all)
    ospec = pl.BlockSpec((tm, dg), lambda i: (i, 0))
    return pl.pallas_call(
        kern, out_shape=jax.ShapeDtypeStruct((rows_all, dg), BF16), grid=(rows_all // tm,),
        in_specs=[ospec, ospec, pl.BlockSpec((tm, dg), lambda i: (i, gcb)), par],
        out_specs=ospec,
        compiler_params=_cparams("arbitrary"), name="hgrn_post",
    )(o_f, o_b, z_gates, gain)


def _largest_divisor(n, candidates):
    for cand in candidates:
        if n % cand == 0:
            return cand
    raise ValueError(f"no tile in {candidates} divides {n}")


def _pad_to(x, axis, size):
    pad = size - x.shape[axis]
    if pad == 0:
        return x
    widths = [(0, 0)] * x.ndim
    widths[axis] = (0, pad)
    return jnp.pad(x, widths)


def _pair_state_to_bd(s):
    b, two, h, n, _ = s.shape
    s = s.reshape(b, two, h // 2, 2, n, n)
    z = jnp.zeros_like(s[:, :, :, 0])
    top = jnp.concatenate([s[:, :, :, 0], z], axis=-1)
    bot = jnp.concatenate([z, s[:, :, :, 1]], axis=-1)
    return jnp.concatenate([top, bot], axis=-2)


def _bd_to_pair_state(sbd):
    b, two, hp, n2, _ = sbd.shape
    n = n2 // 2
    s0 = sbd[:, :, :, :n, :n]
    s1 = sbd[:, :, :, n:, n:]
    return jnp.stack([s0, s1], axis=3).reshape(b, two, hp * 2, n, n)


def kernel(x_prompt, x_sample, state_rwkv, state_hgrn, c, c_ctx, w_mod, b_mod, norm_ffn1, norm_mix, norm_ffn2, ffn1_w_in, ffn1_w_out, ffn2_w_in, ffn2_w_out, w_in, rwkv_mu_prev, rwkv_mu_next, rwkv_w0, rwkv_w2, rwkv_a0, rwkv_a2, rwkv_g2, rwkv_k_k, rwkv_k_a, rwkv_r_k, rwkv_ln_w, rwkv_ln_b, hgrn_lb_logits, hgrn_norm, w_branch_rwkv, w_branch_hgrn, w_out, norm_final):
    nbc, tc, d = x_prompt.shape
    nbl, tl, _ = x_sample.shape
    depth = w_mod.shape[0]
    mc, ml = nbc * tc, nbl * tl
    m = mc + ml
    dims = (mc, tc, tl, nbl)
    dr = rwkv_w0.shape[-1]
    dg = w_branch_hgrn.shape[1]
    dff = ffn1_w_out.shape[1]
    wl, al, gl = rwkv_w2.shape[2], rwkv_a2.shape[2], rwkv_g2.shape[1]
    glp = -(-gl // LANES) * LANES
    rw_cols = 3 * dr + 2 * wl + 2 * al + gl
    zr_cols = 3 * dr + 2 * wl + 2 * al + glp
    assert nbl < MOD_ROWS and tl % GRID_W == 0 and tl // GRID_W == CHUNK and tc % CHUNK == 0

    tm_norm = min(256, tc)
    tm_mm = min(512, mc)
    tm_proj = min(1024, mc)
    tn_of = lambda n: _largest_divisor(n, (1024, 512, 256, 128))
    tf = _largest_divisor(dff, (256, 128))

    lb_all = jnp.cumsum(jax.nn.softmax(hgrn_lb_logits.astype(F32), axis=0), axis=0)
    cond = jnp.concatenate([c, c_ctx[None, :], jnp.zeros((MOD_ROWS - nbl - 1, d), F32)], axis=0)

    x = (x_prompt.reshape(mc, d), x_sample.reshape(ml, d))
    new_r, new_h = [], []
    for l in range(depth):
        mod3 = _modulation(cond, w_mod[l], b_mod[l], tn_of(N_MOD * d)).reshape(MOD_ROWS, N_MOD, d)

        h = _resnorm(x, None, mod3, norm_ffn1[l], dims, res_scale=None, gate_idx=None, mod_idx=0,
                     emit_x=False, out_dtype=BF16, tm=tm_norm)[0]
        delta = _ffn(h, ffn1_w_in[l].astype(BF16), ffn1_w_out[l].astype(BF16), tm=tm_proj, tf=tf)

        x, h = _resnorm(x, delta, mod3, norm_mix[l], dims, res_scale=FFN_HALF, gate_idx=2,
                        mod_idx=3, emit_x=True, out_dtype=BF16, tm=tm_norm)
        wi = w_in[l]
        hg0 = rw_cols
        w_r = _pad_to(wi[:, :rw_cols], 1, zr_cols).astype(BF16)
        w_hs = wi[:, hg0:hg0 + 4 * dg].astype(BF16)
        w_gates = wi[:, hg0 + 4 * dg:].astype(BF16)
        col_gate, col_gr, col_gh = 0, dg, dg + d
        z_r = _matmul(h, w_r, out_dtype=F32, tm=tm_proj, tn=tn_of(zr_cols), name="proj_rwkv")
        z_g = _matmul(h, w_gates, out_dtype=F32, tm=tm_proj, tn=tn_of(dg), name="proj_gates")
        zhs = _matmul(h, w_hs, out_dtype=F32, tm=tm_proj, tn=tn_of(dg), name="proj_hgrn")

        p = {
            "mu_prev": _pad_to(rwkv_mu_prev[l][None, :], 1, zr_cols),
            "mu_next": _pad_to(rwkv_mu_next[l][None, :], 1, zr_cols),
            "w0": rwkv_w0[l], "w2": rwkv_w2[l].astype(BF16),
            "a0": rwkv_a0[l], "a2": rwkv_a2[l].astype(BF16),
            "g2": _pad_to(rwkv_g2[l], 0, glp).astype(BF16),
            "k_k": rwkv_k_k[l][None, :], "k_a": rwkv_k_a[l][None, :],
            "r_k": rwkv_r_k[l].reshape(1, dr),
            "ln_w": rwkv_ln_w[l][None, :], "ln_b": rwkv_ln_b[l][None, :],
        }
        names = ("r", "k", "v", "kk", "a_f", "a_b", "lw_f", "lw_b", "g", "bonus")
        pre = dict(zip(names, _rwkv_prep(z_r, p, dims, tm=min(128, tc), zr_cols=zr_cols, dr=dr,
                                         wl=wl, al=al, gl=glp)))
        npairs = _largest_divisor(dr // LANES, (16, 8, 4, 2, 1))
        of_c, ob_c, sr_c = _rwkv_scan(pre, p["k_a"], None, row_off=0, nb=nbc, t=tc,
                                      emit_final=True, npairs=npairs)
        o_f, o_b = _rwkv_scan(pre, p["k_a"], _pair_state_to_bd(state_rwkv[:, l].astype(F32)),
                              row_off=mc, nb=nbl, t=tl, emit_final=False, npairs=npairs,
                              into=(of_c, ob_c))
        y_r = _rwkv_post(o_f, o_b, pre, p, tm=min(256, tc))
        new_r.append(_bd_to_pair_state(sr_c))

        lb = lb_all[l][None, :]
        gain_h = hgrn_norm[l][None, :]
        nheads = _largest_divisor(dg // LANES, (8, 4, 2, 1))
        nheads_cols = _largest_divisor(dg // LANES, (4, 2, 1))
        hf_c, hb_c, sh_c = _hgrn_scan(zhs, lb, None, row_off=0, nb=nbc, t=tc, dg=dg,
                                      emit_final=True, nheads=nheads, column_major=False)
        h_f, h_b = _hgrn_scan(zhs, lb, state_hgrn[:, l].astype(F32), row_off=mc, nb=nbl, t=tl,
                              dg=dg, emit_final=False, nheads=nheads_cols, column_major=True,
                              into=(hf_c, hb_c))
        y_h = _hgrn_post(h_f, h_b, z_g, gain_h, gate_col=col_gate)
        new_h.append(sh_c)

        merged = _merge(y_r, y_h, w_branch_rwkv[l].astype(BF16), w_branch_hgrn[l].astype(BF16),
                        z_g, col_gr, col_gh, tm=tm_mm, tn=tn_of(dg))
        delta = _matmul(merged, w_out[l].astype(BF16), out_dtype=F32, tm=tm_proj, tn=tn_of(d),
                        name="mix_out")

        x, h = _resnorm(x, delta, mod3, norm_ffn2[l], dims, res_scale=1.0, gate_idx=5, mod_idx=6,
                        emit_x=True, out_dtype=BF16, tm=tm_norm)
        delta = _ffn(h, ffn2_w_in[l].astype(BF16), ffn2_w_out[l].astype(BF16), tm=tm_proj, tf=tf)
        last = l == depth - 1
        if not last:
            x = _resnorm(x, delta, mod3, norm_ffn2[l], dims, res_scale=FFN_HALF, gate_idx=8,
                         mod_idx=None, emit_x=True, out_dtype=F32, tm=tm_norm)[0]

    y_c = _resnorm(x, delta, mod3, norm_final, dims, res_scale=FFN_HALF, gate_idx=8, mod_idx=None,
                   emit_x=False, out_dtype=F32, tm=tm_norm, row_off=0, rows=mc)[0]
    y_l = _resnorm(x, delta, mod3, norm_final, dims, res_scale=FFN_HALF, gate_idx=8, mod_idx=None,
                   emit_x=False, out_dtype=F32, tm=tm_norm, row_off=mc, rows=ml)[0]
    return (y_c.reshape(nbc, tc, d), y_l.reshape(nbl, tl, d),
            jnp.stack(new_r, axis=1), jnp.stack(new_h, axis=1))
```

```python
import functools
import math

import jax
import jax.numpy as jnp
from jax import lax
from jax.experimental import pallas as pl
from jax.experimental.pallas import tpu as pltpu

F32 = jnp.float32
BF16 = jnp.bfloat16

CHUNK = 64
GRID_W = 64
RWKV_HEAD = 64
HGRN_EXPAND = 128
N_MOD = 9
RMS_EPS = 1e-6
GN_EPS = 64e-5
FFN_HALF = 0.5
DECAY_SCALE = math.exp(-0.5)
LANES = 128
MOD_ROWS = 16
VMEM_LIMIT_BYTES = 56 * 1024 * 1024


def _cparams(*sem):
    return pltpu.CompilerParams(dimension_semantics=sem, vmem_limit_bytes=VMEM_LIMIT_BYTES)


def _dot(a, b):
    return jnp.dot(a.astype(BF16), b.astype(BF16), preferred_element_type=F32)


def _dot_nt(a, b):
    return lax.dot_general(a.astype(BF16), b.astype(BF16), (((1,), (1,)), ((), ())),
                           preferred_element_type=F32)


def _dot_tn(a, b):
    return jnp.dot(a.T.astype(BF16), b.astype(BF16), preferred_element_type=F32)


def _split2(x):
    hi = x.astype(BF16)
    return hi, (x - hi.astype(F32)).astype(BF16)


def _split3(x):
    x1 = x.astype(BF16)
    d1 = x - x1.astype(F32)
    x2 = d1.astype(BF16)
    x3 = (d1 - x2.astype(F32)).astype(BF16)
    return x1, x2, x3


def _exact_rowmix(mat3, x):
    return jnp.dot(mat3, jnp.concatenate(_split3(x), axis=0), preferred_element_type=F32)


def _group_sum(x, bd_ones):
    x1, x2 = _split2(x)
    return (jnp.dot(x1, bd_ones, preferred_element_type=F32)
            + jnp.dot(x2, bd_ones, preferred_element_type=F32))


def _lockstep(chains):
    results = [None] * len(chains)
    active = list(enumerate(chains))
    while active:
        still = []
        for idx, chain in active:
            try:
                next(chain)
                still.append((idx, chain))
            except StopIteration as done:
                results[idx] = done.value
        active = still
    return results


def _mod_kernel(c_ref, w_ref, b_ref, o_ref):
    c = c_ref[...]
    s = c * jax.nn.sigmoid(c)
    o_ref[...] = _dot(s, w_ref[...]) + b_ref[...]


def _modulation(cond, w_mod, b_mod, tn):
    d, n = w_mod.shape
    return pl.pallas_call(
        _mod_kernel,
        out_shape=jax.ShapeDtypeStruct((MOD_ROWS, n), F32),
        grid=(n // tn,),
        in_specs=[pl.BlockSpec((MOD_ROWS, d), lambda j: (0, 0)),
                  pl.BlockSpec((d, tn), lambda j: (0, j)),
                  pl.BlockSpec((1, tn), lambda j: (0, j))],
        out_specs=pl.BlockSpec((MOD_ROWS, tn), lambda j: (0, j)),
        compiler_params=_cparams("arbitrary"),
        name="modulation",
    )(cond, w_mod, b_mod.reshape(1, n))


def _resnorm_kernel(*refs, has_delta, res_scale, gate_idx, mod_idx, emit_x, ctx_tiles):
    it = iter(refs)
    x_ref = next(it)
    xl_ref = next(it) if ctx_tiles is not None else None
    delta_ref = next(it) if has_delta else None
    mod_ref = next(it) if (has_delta or mod_idx is not None) else None
    gain_ref = next(it)
    xo_ref = next(it) if emit_x else None
    h_ref = next(it)
    x = x_ref[...]
    if ctx_tiles is not None:
        x = jnp.where(pl.program_id(0) < ctx_tiles, x, xl_ref[...])
    if has_delta:
        g = mod_ref[gate_idx:gate_idx + 1, :]
        x = x + g * (res_scale * delta_ref[...])
    if emit_x:
        xo_ref[...] = x
    y = x * lax.rsqrt(jnp.mean(x * x, axis=-1, keepdims=True) + RMS_EPS)
    y = y * gain_ref[...]
    if mod_idx is not None:
        sh = mod_ref[mod_idx:mod_idx + 1, :]
        sc = mod_ref[mod_idx + 1:mod_idx + 2, :]
        y = y * (1.0 + sc) + sh
    h_ref[...] = y.astype(h_ref.dtype)


def _resnorm(x, delta, mod3, gain, dims, *, res_scale, gate_idx, mod_idx, emit_x, out_dtype,
             tm, row_off=0, rows=None):
    mc, tc, tl, nlb = dims
    n_ctx_tiles = mc // tm
    tiles_per_lat = tl // tm
    ro = row_off // tm
    split_x = isinstance(x, tuple)
    if split_x:
        assert row_off == 0 and rows is None
        d = x[0].shape[1]
        m_all = x[0].shape[0] + x[1].shape[0]
        in_specs = [pl.BlockSpec((tm, d), lambda i: (jnp.minimum(i, n_ctx_tiles - 1), 0)),
                    pl.BlockSpec((tm, d), lambda i: (jnp.maximum(i - n_ctx_tiles, 0), 0))]
        args = list(x)
    else:
        d = x.shape[1]
        m_all = rows if rows is not None else x.shape[0]
        in_specs = [pl.BlockSpec((tm, d), lambda i: (i + ro, 0))]
        args = [x]

    def mod_row(i):
        gi = i + ro
        return jnp.where(gi < n_ctx_tiles, nlb, (gi - n_ctx_tiles) // tiles_per_lat)

    has_delta = delta is not None
    if has_delta:
        in_specs.append(pl.BlockSpec((tm, d), lambda i: (i + ro, 0)))
        args.append(delta)
    if has_delta or mod_idx is not None:
        in_specs.append(pl.BlockSpec((None, N_MOD, d), lambda i: (mod_row(i), 0, 0)))
        args.append(mod3)
    in_specs.append(pl.BlockSpec((1, d), lambda i: (0, 0)))
    args.append(gain.reshape(1, d))
    out_shape, out_specs = [], []
    if emit_x:
        out_shape.append(jax.ShapeDtypeStruct((m_all, d), F32))
        out_specs.append(pl.BlockSpec((tm, d), lambda i: (i, 0)))
    out_shape.append(jax.ShapeDtypeStruct((m_all, d), out_dtype))
    out_specs.append(pl.BlockSpec((tm, d), lambda i: (i, 0)))
    kern = functools.partial(_resnorm_kernel, has_delta=has_delta, res_scale=res_scale,
                             gate_idx=gate_idx, mod_idx=mod_idx, emit_x=emit_x,
                             ctx_tiles=n_ctx_tiles if split_x else None)
    return pl.pallas_call(
        kern, out_shape=out_shape, grid=(m_all // tm,), in_specs=in_specs, out_specs=out_specs,
        compiler_params=_cparams("arbitrary"), name="resnorm",
    )(*args)


def _mm_kernel(a_ref, w_ref, o_ref):
    o_ref[...] = jnp.dot(a_ref[...], w_ref[...], preferred_element_type=F32).astype(o_ref.dtype)


def _matmul(a, w, *, out_dtype, tm, tn, row_off=0, rows=None, name="matmul"):
    k = a.shape[1]
    n = w.shape[1]
    rows = a.shape[0] if rows is None else rows
    ro = row_off // tm
    return pl.pallas_call(
        _mm_kernel,
        out_shape=jax.ShapeDtypeStruct((rows, n), out_dtype),
        grid=(rows // tm, n // tn),
        in_specs=[pl.BlockSpec((tm, k), lambda i, j: (i + ro, 0)),
                  pl.BlockSpec((k, tn), lambda i, j: (0, j))],
        out_specs=pl.BlockSpec((tm, tn), lambda i, j: (i, j)),
        compiler_params=_cparams("arbitrary", "arbitrary"), name=name,
    )(a, w)


def _ffn_kernel(h_ref, wa_ref, wb_ref, wo_ref, o_ref):
    @pl.when(pl.program_id(1) == 0)
    def _():
        o_ref[...] = jnp.zeros_like(o_ref)

    h = h_ref[...]
    a = jnp.dot(h, wa_ref[...], preferred_element_type=F32)
    b = jnp.dot(h, wb_ref[...], preferred_element_type=F32)
    u = (a * jax.nn.sigmoid(a) * b).astype(BF16)
    o_ref[...] += jnp.dot(u, wo_ref[...], preferred_element_type=F32)


def _ffn(h, w_in, w_out, *, tm, tf):
    m, d = h.shape
    dff = w_out.shape[0]
    nf = dff // tf
    return pl.pallas_call(
        _ffn_kernel,
        out_shape=jax.ShapeDtypeStruct((m, d), F32),
        grid=(m // tm, nf),
        in_specs=[pl.BlockSpec((tm, d), lambda i, j: (i, 0)),
                  pl.BlockSpec((d, tf), lambda i, j: (0, j)),
                  pl.BlockSpec((d, tf), lambda i, j: (0, j + nf)),
                  pl.BlockSpec((tf, d), lambda i, j: (j, 0))],
        out_specs=pl.BlockSpec((tm, d), lambda i, j: (i, 0), pipeline_mode=pl.Buffered(1)),
        compiler_params=_cparams("arbitrary", "arbitrary"), name="ffn",
    )(h, w_in, w_in, w_out)


def _merge_kernel(yr_ref, yh_ref, wr_ref, wh_ref, gr_ref, gh_ref, o_ref):
    pr = jnp.dot(yr_ref[...], wr_ref[...], preferred_element_type=F32)
    ph = jnp.dot(yh_ref[...], wh_ref[...], preferred_element_type=F32)
    o_ref[...] = (jax.nn.sigmoid(gr_ref[...]) * pr
                  + jax.nn.sigmoid(gh_ref[...]) * ph).astype(o_ref.dtype)


def _merge(y_r, y_h, w_br, w_bh, z_rest, col_gr, col_gh, *, tm, tn):
    m, kr = y_r.shape
    kh = y_h.shape[1]
    d = w_br.shape[1]
    cgr, cgh = col_gr // tn, col_gh // tn
    return pl.pallas_call(
        _merge_kernel,
        out_shape=jax.ShapeDtypeStruct((m, d), BF16),
        grid=(m // tm, d // tn),
        in_specs=[pl.BlockSpec((tm, kr), lambda i, j: (i, 0), pipeline_mode=pl.Buffered(1)),
                  pl.BlockSpec((tm, kh), lambda i, j: (i, 0), pipeline_mode=pl.Buffered(1)),
                  pl.BlockSpec((kr, tn), lambda i, j: (0, j)),
                  pl.BlockSpec((kh, tn), lambda i, j: (0, j)),
                  pl.BlockSpec((tm, tn), lambda i, j: (i, j + cgr)),
                  pl.BlockSpec((tm, tn), lambda i, j: (i, j + cgh))],
        out_specs=pl.BlockSpec((tm, tn), lambda i, j: (i, j)),
        compiler_params=_cparams("arbitrary", "arbitrary"), name="merge",
    )(y_r, y_h, w_br, w_bh, z_rest, z_rest)


def _log2(n):
    assert n & (n - 1) == 0
    return n.bit_length() - 1


def _head_ones(n, group):
    sh = _log2(group)
    i = lax.broadcasted_iota(jnp.int32, (n, n), 0) >> sh
    j = lax.broadcasted_iota(jnp.int32, (n, n), 1) >> sh
    return jnp.where(i == j, 1.0, 0.0).astype(BF16)


def _rwkv_prep_kernel(z_ref, zp_ref, zn_ref, mup_ref, mun_ref, w0_ref, w2_ref, a0_ref, a2_ref,
                      g2_ref, kk_ref, ka_ref, rk_ref,
                      r_o, k_o, v_o, kkn_o, af_o, ab_o, lwf_o, lwb_o, g_o, bonus_o,
                      *, tm, mc, tc, tl, dr, wl, al, gl):
    row0 = pl.program_id(0) * tm
    in_ctx = row0 < mc
    pos = jnp.where(in_ctx, row0 % tc, (row0 - mc) % tl)
    seqlen = jnp.where(in_ctx, tc, tl)
    keep_prev = jnp.where(pos == 0, 0.0, 1.0)
    keep_next = jnp.where(pos + tm == seqlen, 0.0, 1.0)
    rows = lax.broadcasted_iota(jnp.int32, (tm, 1), 0)

    def shifted(c0, c1):
        z = z_ref[:, c0:c1]
        zp = jnp.where(rows == 0, keep_prev * zp_ref[7:8, c0:c1], pltpu.roll(z, 1, 0))
        zn = jnp.where(rows == tm - 1, keep_next * zn_ref[0:1, c0:c1], pltpu.roll(z, tm - 1, 0))
        return z + mup_ref[:, c0:c1] * (zp - z) + mun_ref[:, c0:c1] * (zn - z)

    bd = _head_ones(LANES, RWKV_HEAD)
    for j in range(dr // LANES):
        c0 = j * LANES
        r_o[:, c0:c0 + LANES] = shifted(c0, c0 + LANES)
        k = shifted(dr + c0, dr + c0 + LANES)
        k_o[:, c0:c0 + LANES] = k
        v_o[:, c0:c0 + LANES] = shifted(2 * dr + c0, 2 * dr + c0 + LANES)
        kk = k * kk_ref[:, c0:c0 + LANES]
        ss = _group_sum(kk * kk, bd)
        kkn_o[:, c0:c0 + LANES] = kk * lax.rsqrt(jnp.maximum(ss, 1e-24))

    c = 3 * dr
    for d, (lw_o, a_o) in enumerate(((lwf_o, af_o), (lwb_o, ab_o))):
        wd = shifted(c + d * wl, c + (d + 1) * wl)
        ad = shifted(c + 2 * wl + d * al, c + 2 * wl + (d + 1) * al)
        wpre = w0_ref[d:d + 1, :] + _dot(jnp.tanh(wd), w2_ref[d])
        lw_o[...] = -DECAY_SCALE * jax.nn.sigmoid(wpre)
        a_o[...] = jax.nn.sigmoid(a0_ref[d:d + 1, :] + _dot(ad, a2_ref[d]))
    gd = shifted(c + 2 * wl + 2 * al, c + 2 * wl + 2 * al + gl)
    g_o[...] = _dot(jax.nn.sigmoid(gd), g2_ref[...])

    for j in range(dr // LANES):
        sl = slice(j * LANES, (j + 1) * LANES)
        r, k, v = r_o[:, sl], k_o[:, sl], v_o[:, sl]
        bonus = 0.0
        for a_o in (af_o, ab_o):
            kd = k * (1.0 + (a_o[:, sl] - 1.0) * ka_ref[:, sl])
            bonus = bonus + _group_sum(r * kd * rk_ref[:, sl], bd) * v
        bonus_o[:, sl] = bonus


def _rwkv_prep(z_rest, p, dims, *, tm, zr_cols, dr, wl, al, gl):
    mc, tc, tl, nlb = dims
    m = z_rest.shape[0]
    nblk8 = m // 8
    t8 = tm // 8
    full = lambda shape: pl.BlockSpec(shape, lambda i: (0,) * len(shape))
    outs = [jax.ShapeDtypeStruct((m, dr), F32)] * 10
    kern = functools.partial(_rwkv_prep_kernel, tm=tm, mc=mc, tc=tc, tl=tl, dr=dr, wl=wl, al=al,
                             gl=gl)
    return pl.pallas_call(
        kern, out_shape=outs, grid=(m // tm,),
        in_specs=[pl.BlockSpec((tm, zr_cols), lambda i: (i, 0)),
                  pl.BlockSpec((8, zr_cols), lambda i: (jnp.maximum(i * t8 - 1, 0), 0)),
                  pl.BlockSpec((8, zr_cols), lambda i: (jnp.minimum((i + 1) * t8, nblk8 - 1), 0)),
                  full((1, zr_cols)), full((1, zr_cols)),
                  full((2, dr)), full((2, wl, dr)), full((2, dr)), full((2, al, dr)),
                  full((gl, dr)), full((1, dr)), full((1, dr)), full((1, dr))],
        out_specs=[pl.BlockSpec((tm, dr), lambda i: (i, 0))] * 10,
        compiler_params=_cparams("arbitrary"), name="rwkv_prep",
    )(z_rest, z_rest, z_rest, p["mu_prev"], p["mu_next"], p["w0"], p["w2"], p["a0"], p["a2"],
      p["g2"], p["k_k"], p["k_a"], p["r_k"])


def _stack_heads(x, lo):
    return jnp.concatenate([jnp.where(lo, x, 0.0), jnp.where(lo, 0.0, x)], axis=0)


SUB_BLOCK = 16
assert CHUNK // SUB_BLOCK == 4


def _rwkv_masks(rev):
    c = CHUNK
    n2 = 2 * c
    ti = lax.broadcasted_iota(jnp.int32, (c, c), 0)
    tj = lax.broadcasted_iota(jnp.int32, (c, c), 1)
    cum = jnp.where((tj >= ti) if rev else (tj <= ti), 1.0, 0.0).astype(BF16)
    cum = jnp.concatenate([cum, cum, cum], axis=1)
    lo = lax.broadcasted_iota(jnp.int32, (c, LANES), 1) < RWKV_HEAD
    ii = lax.broadcasted_iota(jnp.int32, (n2, n2), 0)
    jj = lax.broadcasted_iota(jnp.int32, (n2, n2), 1)
    im, jm = ii & (c - 1), jj & (c - 1)
    strict = (jm > im) if rev else (jm < im)
    incl = (jm >= im) if rev else (jm <= im)
    eye = jnp.where(ii == jj, 1.0, 0.0)
    sub = (ii >> _log2(SUB_BLOCK)) == (jj >> _log2(SUB_BLOCK))
    return cum, lo, strict, incl, eye, sub


def _rwkv_chunk(r, k, v, kk, a, lw, k_a, s, rev, masks):
    c = CHUNK
    n2 = 2 * c
    cum, lo, strict, incl, eye, sub = masks
    bc = _exact_rowmix(cum, lw)
    yield
    bex = bc - lw
    btot = bc[0:1, :] if rev else bc[c - 1:c, :]
    kd = k * (1.0 + (a - 1.0) * k_a)
    bv = kk * a
    e_neg = jnp.exp(-bc)
    e_tot = jnp.exp(btot - bc)
    at = _stack_heads(-kk * jnp.exp(bex), lo).astype(BF16)
    rt_f = _stack_heads(r * jnp.exp(bc), lo)
    kh = _stack_heads(kd * e_neg, lo).astype(BF16)
    bh = _stack_heads(bv * e_neg, lo).astype(BF16)
    vb_f = _stack_heads(v, lo)
    vb = vb_f.astype(BF16)
    kc = _stack_heads(kd * e_tot, lo).astype(BF16)
    bcc = _stack_heads(bv * e_tot, lo).astype(BF16)

    gram = _dot_nt(jnp.concatenate([at, rt_f.astype(BF16)], axis=0),
                   jnp.concatenate([bh, kh], axis=0))
    yield
    nmat = jnp.where(strict, gram[:n2, :n2], 0.0)
    a_ka = jnp.where(strict, gram[:n2, n2:], 0.0)
    a_rb = jnp.where(incl, gram[n2:, :n2], 0.0)
    a_rk = jnp.where(incl, gram[n2:, n2:], 0.0)
    n_d = jnp.where(sub, nmat, 0.0)
    t_d = eye + n_d
    pw = n_d
    for _ in range(_log2(SUB_BLOCK) - 1):
        pw = _dot(pw, pw)
        yield
        t_d = t_d + _dot(t_d, pw)
    x1 = _dot(a_ka, vb)
    yield
    xm = _dot(t_d, jnp.where(sub, 0.0, nmat))
    yield
    xm2 = _dot(xm, xm)
    yield
    ym = eye + xm + xm2 + _dot(xm, xm2)
    yield
    tm_ = _dot(ym, t_d)
    yield
    aw = _dot(tm_, jnp.concatenate([at, x1.astype(BF16)], axis=1))
    yield
    ah, w = aw[:, :n2], aw[:, n2:]
    rhs = jnp.concatenate(
        [jnp.concatenate([vb, jnp.zeros_like(vb)], axis=1),
         jnp.concatenate([w.astype(BF16), ah.astype(BF16)], axis=1)], axis=0)
    qr = _dot(jnp.concatenate([a_rk, a_rb], axis=1), rhs)
    q = qr[:, :n2]
    rr = rt_f + qr[:, n2:]
    mm = _dot_tn(ah, bcc)
    nn = _dot_tn(jnp.concatenate([vb_f, w], axis=0), jnp.concatenate([kc, bcc], axis=0))
    yield
    o_bd = _dot_nt(rr, s) + q
    o = o_bd[:c, :] + o_bd[c:, :]
    s_new = s * jnp.exp(btot) + _dot(s, mm) + nn
    return o, s_new


def _rwkv_scan_kernel(*refs, has_init, emit_final, nchunk, npairs, aliased):
    it = iter(refs)
    fwd = [next(it) for _ in range(6)]
    bwd = [next(it) for _ in range(6)]
    ka_ref = next(it)
    s0_ref = next(it) if has_init else None
    if aliased:
        next(it), next(it)
    of_ref = next(it)
    ob_ref = next(it)
    sfin_ref = next(it) if emit_final else None
    s_scr = next(it)
    ci = pl.program_id(2)

    @pl.when(ci == 0)
    def _():
        if has_init:
            s_scr[...] = s0_ref[...]
        else:
            s_scr[...] = jnp.zeros_like(s_scr)

    chains, sinks = [], []
    for d, (ins, o_ref) in enumerate(((fwd, of_ref), (bwd, ob_ref))):
        masks = _rwkv_masks(rev=(d == 1))
        for hp in range(npairs):
            sl = slice(hp * LANES, (hp + 1) * LANES)
            r, k, v, kk, a, lw = (x[:, sl] for x in ins)
            chains.append(_rwkv_chunk(r, k, v, kk, a, lw, ka_ref[:, sl], s_scr[d, hp],
                                      rev=(d == 1), masks=masks))
            sinks.append((o_ref, sl, d, hp))
    for (o, s_new), (o_ref, sl, d, hp) in zip(_lockstep(chains), sinks):
        o_ref[:, sl] = o
        s_scr[d, hp] = s_new

    if emit_final:
        @pl.when(ci == nchunk - 1)
        def _():
            sfin_ref[...] = s_scr[...]


def _rwkv_scan(pre, k_a, s0_bd, *, row_off, nb, t, emit_final, npairs, into=None):
    m, dr = pre["r"].shape
    ngroup = dr // (LANES * npairs)
    nchunk = t // CHUNK
    rb = row_off // CHUNK
    has_init = s0_bd is not None
    fmap = lambda b, h, c: (rb + b * nchunk + c, h)
    bmap = lambda b, h, c: (rb + b * nchunk + (nchunk - 1 - c), h)
    blk = (CHUNK, LANES * npairs)
    in_specs = ([pl.BlockSpec(blk, fmap)] * 6 + [pl.BlockSpec(blk, bmap)] * 6
                + [pl.BlockSpec((1, LANES * npairs), lambda b, h, c: (0, h))])
    args = [pre["r"], pre["k"], pre["v"], pre["kk"], pre["a_f"], pre["lw_f"],
            pre["r"], pre["k"], pre["v"], pre["kk"], pre["a_b"], pre["lw_b"], k_a]
    sblk = pl.BlockSpec((None, 2, npairs, LANES, LANES), lambda b, h, c: (b, 0, h, 0, 0))
    if has_init:
        in_specs.append(sblk)
        args.append(s0_bd)
    aliases = {}
    if into is not None:
        aliases = {len(args): 0, len(args) + 1: 1}
        in_specs += [pl.BlockSpec(memory_space=pl.ANY)] * 2
        args += list(into)
    out_shape = [jax.ShapeDtypeStruct((m, dr), F32)] * 2
    out_specs = [pl.BlockSpec(blk, fmap), pl.BlockSpec(blk, bmap)]
    if emit_final:
        out_shape.append(jax.ShapeDtypeStruct((nb, 2, dr // LANES, LANES, LANES), F32))
        out_specs.append(sblk)
    kern = functools.partial(_rwkv_scan_kernel, has_init=has_init, emit_final=emit_final,
                             nchunk=nchunk, npairs=npairs, aliased=into is not None)
    return pl.pallas_call(
        kern, out_shape=out_shape, grid=(nb, ngroup, nchunk), in_specs=in_specs,
        out_specs=out_specs, scratch_shapes=[pltpu.VMEM((2, npairs, LANES, LANES), F32)],
        input_output_aliases=aliases,
        compiler_params=_cparams("arbitrary", "arbitrary", "arbitrary"), name="rwkv_scan",
    )(*args)


def _rwkv_post_kernel(of_ref, ob_ref, bonus_ref, g_ref, lnw_ref, lnb_ref, y_ref, *, dr):
    bd = _head_ones(LANES, RWKV_HEAD)
    inv = 1.0 / RWKV_HEAD
    for j in range(dr // LANES):
        sl = slice(j * LANES, (j + 1) * LANES)
        o = of_ref[:, sl] + ob_ref[:, sl]
        mean = _group_sum(o, bd) * inv
        oc = o - mean
        var = _group_sum(oc * oc, bd) * inv
        on = oc * lax.rsqrt(var + GN_EPS) * lnw_ref[:, sl] + lnb_ref[:, sl]
        y_ref[:, sl] = ((on + bonus_ref[:, sl]) * g_ref[:, sl]).astype(y_ref.dtype)


def _rwkv_post(o_f, o_b, pre, p, *, tm):
    rows, dr = o_f.shape
    own = pl.BlockSpec((tm, dr), lambda i: (i, 0))
    par = pl.BlockSpec((1, dr), lambda i: (0, 0))
    return pl.pallas_call(
        functools.partial(_rwkv_post_kernel, dr=dr),
        out_shape=jax.ShapeDtypeStruct((rows, dr), BF16),
        grid=(rows // tm,),
        in_specs=[own] * 4 + [par] * 2,
        out_specs=own,
        compiler_params=_cparams("arbitrary"), name="rwkv_post",
    )(o_f, o_b, pre["bonus"], pre["g"], p["ln_w"], p["ln_b"])


HGRN_LEVELS = 6


SUBLANES = 8


def _boundary_rows(b, lvl, rev, sub):
    c = CHUNK
    half = 1 << lvl
    blk = 2 * half
    src = (lambda base: base + half) if rev else (lambda base: base + half - 1)

    def rows(starts, nrows):
        return jnp.concatenate(
            [jnp.broadcast_to(b[src(s):src(s) + 1, :], (nrows, LANES)) for s in starts], axis=0)

    if blk >= SUBLANES:
        return rows(range(0, c, blk), blk)
    out = None
    for k in range(SUBLANES // blk):
        arr = rows(range(k * blk, c, SUBLANES), SUBLANES)
        out = arr if out is None else jnp.where(sub >= k * blk, arr, out)
    return out


def _hgrn_masks(rev):
    c = CHUNK
    ti = lax.broadcasted_iota(jnp.int32, (c, c), 0)
    tj = lax.broadcasted_iota(jnp.int32, (c, c), 1)
    late, early = (tj, ti) if rev else (ti, tj)
    sels = [((late >> lvl) == (early >> lvl) + 1) & (((early >> lvl) & 1) == 0)
            for lvl in range(HGRN_LEVELS)]
    cum = jnp.where((tj >= ti) if rev else (tj <= ti), 1.0, 0.0).astype(BF16)
    sub = lax.broadcasted_iota(jnp.int32, (c, LANES), 0) & (SUBLANES - 1)
    return jnp.concatenate([cum, cum, cum], axis=1), ti == tj, sels, sub


def _hgrn_chunk(q_pre, f_pre, v, lb, st, rev, masks):
    c = CHUNK
    cum3, diag, sels, sub = masks
    q = q_pre * jax.nn.sigmoid(q_pre)
    f = lb + (1.0 - lb) * jax.nn.sigmoid(f_pre)
    kf = 1.0 - f
    b = _exact_rowmix(cum3, jnp.log(f))
    yield
    btot = b[0:1, :] if rev else b[c - 1:c, :]
    q_b, kf_b = q.astype(BF16), kf.astype(BF16)
    attn = jnp.where(diag, _dot_nt(q_b, kf_b), 0.0)
    for lvl in range(HGRN_LEVELS):
        dq = b - _boundary_rows(b, lvl, rev, sub)
        wgt = jnp.exp(-jnp.abs(dq)).astype(BF16)
        attn = jnp.where(sels[lvl], _dot_nt(q_b * wgt, kf_b * wgt), attn)
    yield
    o = _dot_nt(q * jnp.exp(b), st) + _dot(attn, v)
    st_new = st * jnp.exp(btot) + _dot_tn(v, kf * jnp.exp(btot - b))
    return o, st_new


def _hgrn_scan_kernel(*refs, has_init, emit_final, nsteps, nheads, cols, aliased):
    it = iter(refs)
    qf_ref, ff_ref, vf_ref = next(it), next(it), next(it)
    qb_ref, fb_ref, vb_ref = next(it), next(it), next(it)
    lb_ref = next(it)
    s0_ref = next(it) if has_init else None
    if aliased:
        next(it), next(it)
    of_ref, ob_ref = next(it), next(it)
    sfin_ref = next(it) if emit_final else None
    s_scr = next(it)
    in_scr, out_scr = (next(it), next(it)) if cols is not None else (None, None)
    ci = pl.program_id(2)

    @pl.when(ci == 0)
    def _():
        if has_init:
            for d in range(2):
                for h in range(nheads):
                    s_scr[d, h] = s0_ref[d, h].T
        else:
            s_scr[...] = jnp.zeros_like(s_scr)

    ins = ((qf_ref, ff_ref, vf_ref), (qb_ref, fb_ref, vb_ref))
    outs = (of_ref, ob_ref)
    all_masks = [_hgrn_masks(rev=False), _hgrn_masks(rev=True)]

    def advance(load, store):
        chains, sinks = [], []
        for d in range(2):
            for h in range(nheads):
                sl = slice(h * LANES, (h + 1) * LANES)
                qp, fp, v = (load(d, k, sl) for k in range(3))
                chains.append(_hgrn_chunk(qp, fp, v, lb_ref[:, sl], s_scr[d, h], rev=(d == 1),
                                          masks=all_masks[d]))
                sinks.append((d, h, sl))
        for (o, s_new), (d, h, sl) in zip(_lockstep(chains), sinks):
            store(d, sl, o)
            s_scr[d, h] = s_new

    if cols is None:
        def store_rows(d, sl, o):
            outs[d][:, sl] = o
        advance(lambda d, k, sl: ins[d][k][:, sl], store_rows)
    else:
        for d in range(2):
            for k in range(3):
                in_scr[d, k] = jnp.swapaxes(ins[d][k][...], 0, 1)

        def body(j, carry):
            col = (j, cols - 1 - j)

            def store_col(d, sl, o):
                out_scr[d, col[d], :, sl] = o
            advance(lambda d, k, sl: in_scr[d, k, col[d], :, sl], store_col)
            return carry
        lax.fori_loop(0, cols, body, 0)
        for d in range(2):
            outs[d][...] = jnp.swapaxes(out_scr[d], 0, 1)

    if emit_final:
        @pl.when(ci == nsteps - 1)
        def _():
            for d in range(2):
                for h in range(nheads):
                    sfin_ref[d, h] = s_scr[d, h].T


HGRN_COLS = 8


def _hgrn_scan(zhs, lb, s0, *, row_off, nb, t, dg, emit_final, nheads, column_major,
               into=None):
    m = zhs.shape[0]
    ng = dg // (LANES * nheads)
    has_init = s0 is not None
    lanes = LANES * nheads
    if column_major:
        nrow = t // GRID_W
        assert nrow == CHUNK and m % t == 0 and row_off % t == 0
        nsteps = GRID_W // HGRN_COLS
        seq0 = row_off // t
        zsrc = zhs.reshape(m // t, nrow, GRID_W, 4 * dg)
        blk = (None, nrow, HGRN_COLS, lanes)
        fwd = lambda g: (lambda b, h, c: (b + seq0, 0, c, g * ng + h))
        bwd = lambda g: (lambda b, h, c: (b + seq0, 0, nsteps - 1 - c, g * ng + h))
        out_dims = (m // t, nrow, GRID_W, dg)
        ofwd = lambda b, h, c: (b + seq0, 0, c, h)
        obwd = lambda b, h, c: (b + seq0, 0, nsteps - 1 - c, h)
        cols = HGRN_COLS
    else:
        nsteps = t // CHUNK
        rb = row_off // CHUNK
        zsrc = zhs
        blk = (CHUNK, lanes)
        fwd = lambda g: (lambda b, h, c: (rb + b * nsteps + c, g * ng + h))
        bwd = lambda g: (lambda b, h, c: (rb + b * nsteps + nsteps - 1 - c, g * ng + h))
        out_dims = (m, dg)
        ofwd = lambda b, h, c: (rb + b * nsteps + c, h)
        obwd = lambda b, h, c: (rb + b * nsteps + nsteps - 1 - c, h)
        cols = None

    in_specs = [pl.BlockSpec(blk, fwd(0)), pl.BlockSpec(blk, fwd(1)), pl.BlockSpec(blk, fwd(3)),
                pl.BlockSpec(blk, bwd(0)), pl.BlockSpec(blk, bwd(2)), pl.BlockSpec(blk, bwd(3)),
                pl.BlockSpec((1, lanes), lambda b, h, c: (0, h))]
    args = [zsrc] * 6 + [lb]
    sblk = pl.BlockSpec((None, 2, nheads, LANES, LANES), lambda b, h, c: (b, 0, h, 0, 0))
    if has_init:
        in_specs.append(sblk)
        args.append(s0)
    aliases = {}
    if into is not None:
        aliases = {len(args): 0, len(args) + 1: 1}
        in_specs += [pl.BlockSpec(memory_space=pl.ANY)] * 2
        args += [o.reshape(out_dims) for o in into]
    out_shape = [jax.ShapeDtypeStruct(out_dims, F32)] * 2
    out_specs = [pl.BlockSpec(blk, ofwd), pl.BlockSpec(blk, obwd)]
    if emit_final:
        out_shape.append(jax.ShapeDtypeStruct((nb, 2, dg // LANES, LANES, LANES), F32))
        out_specs.append(sblk)
    kern = functools.partial(_hgrn_scan_kernel, has_init=has_init, emit_final=emit_final,
                             nsteps=nsteps, nheads=nheads, cols=cols, aliased=into is not None)
    scratch = [pltpu.VMEM((2, nheads, LANES, LANES), F32)]
    if column_major:
        scratch += [pltpu.VMEM((2, 3, cols, CHUNK, lanes), F32),
                    pltpu.VMEM((2, cols, CHUNK, lanes), F32)]
    outs = pl.pallas_call(
        kern, out_shape=out_shape, grid=(nb, ng, nsteps), in_specs=in_specs, out_specs=out_specs,
        scratch_shapes=scratch, input_output_aliases=aliases,
        compiler_params=_cparams("arbitrary", "arbitrary", "arbitrary"), name="hgrn_scan",
    )(*args)
    return [outs[0].reshape(m, dg), outs[1].reshape(m, dg)] + list(outs[2:])


def _hgrn_post_kernel(of_ref, ob_ref, gate_ref, gain_ref, y_ref, *, dg):
    gain = gain_ref[...]
    for j in range(dg // LANES):
        sl = slice(j * LANES, (j + 1) * LANES)
        o = of_ref[:, sl] + ob_ref[:, sl]
        on = o * lax.rsqrt(jnp.mean(o * o, axis=-1, keepdims=True) + RMS_EPS) * gain
        gt = gate_ref[:, sl]
        y_ref[:, sl] = (on * (gt * jax.nn.sigmoid(gt))).astype(y_ref.dtype)


def _hgrn_post(o_f, o_b, z_gates, gain, *, gate_col):
    rows_all, dg = o_f.shape
    kern = functools.partial(_hgrn_post_kernel, dg=dg)
    gcb = gate_col // dg
    par = pl.BlockSpec((1, LANES), lambda *_: (0, 0))
    tm = min(256, rows_all)
    ospec = pl.BlockSpec((tm, dg), lambda i: (i, 0))
    return pl.pallas_call(
        kern, out_shape=jax.ShapeDtypeStruct((rows_all, dg), BF16), grid=(rows_all // tm,),
        in_specs=[ospec, ospec, pl.BlockSpec((tm, dg), lambda i: (i, gcb)), par],
        out_specs=ospec,
        compiler_params=_cparams("arbitrary"), name="hgrn_post",
    )(o_f, o_b, z_gates, gain)


def _largest_divisor(n, candidates):
    for cand in candidates:
        if n % cand == 0:
            return cand
    raise ValueError(f"no tile in {candidates} divides {n}")


def _pad_to(x, axis, size):
    pad = size - x.shape[axis]
    if pad == 0:
        return x
    widths = [(0, 0)] * x.ndim
    widths[axis] = (0, pad)
    return jnp.pad(x, widths)


def _pair_state_to_bd(s):
    b, two, h, n, _ = s.shape
    s = s.reshape(b, two, h // 2, 2, n, n)
    z = jnp.zeros_like(s[:, :, :, 0])
    top = jnp.concatenate([s[:, :, :, 0], z], axis=-1)
    bot = jnp.concatenate([z, s[:, :, :, 1]], axis=-1)
    return jnp.concatenate([top, bot], axis=-2)


def _bd_to_pair_state(sbd):
    b, two, hp, n2, _ = sbd.shape
    n = n2 // 2
    s0 = sbd[:, :, :, :n, :n]
    s1 = sbd[:, :, :, n:, n:]
    return jnp.stack([s0, s1], axis=3).reshape(b, two, hp * 2, n, n)


def kernel(x_prompt, x_sample, state_rwkv, state_hgrn, c, c_ctx, w_mod, b_mod, norm_ffn1, norm_mix, norm_ffn2, ffn1_w_in, ffn1_w_out, ffn2_w_in, ffn2_w_out, w_in, rwkv_mu_prev, rwkv_mu_next, rwkv_w0, rwkv_w2, rwkv_a0, rwkv_a2, rwkv_g2, rwkv_k_k, rwkv_k_a, rwkv_r_k, rwkv_ln_w, rwkv_ln_b, hgrn_lb_logits, hgrn_norm, w_branch_rwkv, w_branch_hgrn, w_out, norm_final):
    nbc, tc, d = x_prompt.shape
    nbl, tl, _ = x_sample.shape
    depth = w_mod.shape[0]
    mc, ml = nbc * tc, nbl * tl
    m = mc + ml
    dims = (mc, tc, tl, nbl)
    dr = rwkv_w0.shape[-1]
    dg = w_branch_hgrn.shape[1]
    dff = ffn1_w_out.shape[1]
    wl, al, gl = rwkv_w2.shape[2], rwkv_a2.shape[2], rwkv_g2.shape[1]
    glp = -(-gl // LANES) * LANES
    rw_cols = 3 * dr + 2 * wl + 2 * al + gl
    zr_cols = 3 * dr + 2 * wl + 2 * al + glp
    assert nbl < MOD_ROWS and tl % GRID_W == 0 and tl // GRID_W == CHUNK and tc % CHUNK == 0

    tm_norm = min(256, tc)
    tm_proj = min(1024, mc)
    tn_of = lambda n: _largest_divisor(n, (1024, 512, 256, 128))
    tf = _largest_divisor(dff, (256, 128))

    lb_all = jnp.cumsum(jax.nn.softmax(hgrn_lb_logits.astype(F32), axis=0), axis=0)
    cond = jnp.concatenate([c, c_ctx[None, :], jnp.zeros((MOD_ROWS - nbl - 1, d), F32)], axis=0)

    x = (x_prompt.reshape(mc, d), x_sample.reshape(ml, d))
    new_r, new_h = [], []
    for l in range(depth):
        mod3 = _modulation(cond, w_mod[l], b_mod[l], tn_of(N_MOD * d)).reshape(MOD_ROWS, N_MOD, d)

        h = _resnorm(x, None, mod3, norm_ffn1[l], dims, res_scale=None, gate_idx=None, mod_idx=0,
                     emit_x=False, out_dtype=BF16, tm=tm_norm)[0]
        delta = _ffn(h, ffn1_w_in[l].astype(BF16), ffn1_w_out[l].astype(BF16), tm=tm_proj, tf=tf)

        x, h = _resnorm(x, delta, mod3, norm_mix[l], dims, res_scale=FFN_HALF, gate_idx=2,
                        mod_idx=3, emit_x=True, out_dtype=BF16, tm=tm_norm)
        wi = w_in[l]
        hg0 = rw_cols
        w_r = _pad_to(wi[:, :rw_cols], 1, zr_cols).astype(BF16)
        w_hs = wi[:, hg0:hg0 + 4 * dg].astype(BF16)
        w_gates = wi[:, hg0 + 4 * dg:].astype(BF16)
        col_gate, col_gr, col_gh = 0, dg, dg + d
        z_r = _matmul(h, w_r, out_dtype=F32, tm=tm_proj, tn=tn_of(zr_cols), name="proj_rwkv")
        z_g = _matmul(h, w_gates, out_dtype=F32, tm=tm_proj, tn=tn_of(dg), name="proj_gates")
        zhs = _matmul(h, w_hs, out_dtype=F32, tm=tm_proj, tn=tn_of(dg), name="proj_hgrn")

        p = {
            "mu_prev": _pad_to(rwkv_mu_prev[l][None, :], 1, zr_cols),
            "mu_next": _pad_to(rwkv_mu_next[l][None, :], 1, zr_cols),
            "w0": rwkv_w0[l], "w2": rwkv_w2[l].astype(BF16),
            "a0": rwkv_a0[l], "a2": rwkv_a2[l].astype(BF16),
            "g2": _pad_to(rwkv_g2[l], 0, glp).astype(BF16),
            "k_k": rwkv_k_k[l][None, :], "k_a": rwkv_k_a[l][None, :],
            "r_k": rwkv_r_k[l].reshape(1, dr),
            "ln_w": rwkv_ln_w[l][None, :], "ln_b": rwkv_ln_b[l][None, :],
        }
        names = ("r", "k", "v", "kk", "a_f", "a_b", "lw_f", "lw_b", "g", "bonus")
        pre = dict(zip(names, _rwkv_prep(z_r, p, dims, tm=min(128, tc), zr_cols=zr_cols, dr=dr,
                                         wl=wl, al=al, gl=glp)))
        npairs = _largest_divisor(dr // LANES, (16, 8, 4, 2, 1))
        of_c, ob_c, sr_c = _rwkv_scan(pre, p["k_a"], None, row_off=0, nb=nbc, t=tc,
                                      emit_final=True, npairs=npairs)
        o_f, o_b = _rwkv_scan(pre, p["k_a"], _pair_state_to_bd(state_rwkv[:, l].astype(F32)),
                              row_off=mc, nb=nbl, t=tl, emit_final=False, npairs=npairs,
                              into=(of_c, ob_c))
        y_r = _rwkv_post(o_f, o_b, pre, p, tm=min(256, tc))
        new_r.append(_bd_to_pair_state(sr_c))

        lb = lb_all[l][None, :]
        gain_h = hgrn_norm[l][None, :]
        nheads = _largest_divisor(dg // LANES, (8, 4, 2, 1))
        nheads_cols = _largest_divisor(dg // LANES, (4, 2, 1))
        hf_c, hb_c, sh_c = _hgrn_scan(zhs, lb, None, row_off=0, nb=nbc, t=tc, dg=dg,
                                      emit_final=True, nheads=nheads, column_major=False)
        h_f, h_b = _hgrn_scan(zhs, lb, state_hgrn[:, l].astype(F32), row_off=mc, nb=nbl, t=tl,
                              dg=dg, emit_final=False, nheads=nheads_cols, column_major=True,
                              into=(hf_c, hb_c))
        y_h = _hgrn_post(h_f, h_b, z_g, gain_h, gate_col=col_gate)
        new_h.append(sh_c)

        merged = _merge(y_r, y_h, w_branch_rwkv[l].astype(BF16), w_branch_hgrn[l].astype(BF16),
                        z_g, col_gr, col_gh, tm=tm_proj, tn=tn_of(dg))
        delta = _matmul(merged, w_out[l].astype(BF16), out_dtype=F32, tm=tm_proj, tn=tn_of(d),
                        name="mix_out")

        x, h = _resnorm(x, delta, mod3, norm_ffn2[l], dims, res_scale=1.0, gate_idx=5, mod_idx=6,
                        emit_x=True, out_dtype=BF16, tm=tm_norm)
        delta = _ffn(h, ffn2_w_in[l].astype(BF16), ffn2_w_out[l].astype(BF16), tm=tm_proj, tf=tf)
        last = l == depth - 1
        if not last:
            x = _resnorm(x, delta, mod3, norm_ffn2[l], dims, res_scale=FFN_HALF, gate_idx=8,
                         mod_idx=None, emit_x=True, out_dtype=F32, tm=tm_norm)[0]

    y_c = _resnorm(x, delta, mod3, norm_final, dims, res_scale=FFN_HALF, gate_idx=8, mod_idx=None,
                   emit_x=False, out_dtype=F32, tm=tm_norm, row_off=0, rows=mc)[0]
    y_l = _resnorm(x, delta, mod3, norm_final, dims, res_scale=FFN_HALF, gate_idx=8, mod_idx=None,
                   emit_x=False, out_dtype=F32, tm=tm_norm, row_off=mc, rows=ml)[0]
    return (y_c.reshape(nbc, tc, d), y_l.reshape(nbl, tl, d),
            jnp.stack(new_r, axis=1), jnp.stack(new_h, axis=1))
```

```python
import functools
import math

import jax
import jax.numpy as jnp
from jax import lax
from jax.experimental import pallas as pl
from jax.experimental.pallas import tpu as pltpu

F32 = jnp.float32
BF16 = jnp.bfloat16

CHUNK = 64
GRID_W = 64
RWKV_HEAD = 64
HGRN_EXPAND = 128
N_MOD = 9
RMS_EPS = 1e-6
GN_EPS = 64e-5
FFN_HALF = 0.5
DECAY_SCALE = math.exp(-0.5)
LANES = 128
MOD_ROWS = 16
VMEM_LIMIT_BYTES = 56 * 1024 * 1024


def _cparams(*sem):
    return pltpu.CompilerParams(dimension_semantics=sem, vmem_limit_bytes=VMEM_LIMIT_BYTES)


def _dot(a, b):
    return jnp.dot(a.astype(BF16), b.astype(BF16), preferred_element_type=F32)


def _dot_nt(a, b):
    return lax.dot_general(a.astype(BF16), b.astype(BF16), (((1,), (1,)), ((), ())),
                           preferred_element_type=F32)


def _dot_tn(a, b):
    return jnp.dot(a.T.astype(BF16), b.astype(BF16), preferred_element_type=F32)


def _split2(x):
    hi = x.astype(BF16)
    return hi, (x - hi.astype(F32)).astype(BF16)


def _split3(x):
    x1 = x.astype(BF16)
    d1 = x - x1.astype(F32)
    x2 = d1.astype(BF16)
    x3 = (d1 - x2.astype(F32)).astype(BF16)
    return x1, x2, x3


def _exact_rowmix(mat3, x):
    return jnp.dot(mat3, jnp.concatenate(_split3(x), axis=0), preferred_element_type=F32)


def _group_sum(x, bd_ones):
    x1, x2 = _split2(x)
    return (jnp.dot(x1, bd_ones, preferred_element_type=F32)
            + jnp.dot(x2, bd_ones, preferred_element_type=F32))


def _lockstep(chains):
    results = [None] * len(chains)
    active = list(enumerate(chains))
    while active:
        still = []
        for idx, chain in active:
            try:
                next(chain)
                still.append((idx, chain))
            except StopIteration as done:
                results[idx] = done.value
        active = still
    return results


def _mod_kernel(c_ref, w_ref, b_ref, o_ref):
    c = c_ref[...]
    s = c * jax.nn.sigmoid(c)
    o_ref[...] = _dot(s, w_ref[...]) + b_ref[...]


def _modulation(cond, w_mod, b_mod, tn):
    d, n = w_mod.shape
    return pl.pallas_call(
        _mod_kernel,
        out_shape=jax.ShapeDtypeStruct((MOD_ROWS, n), F32),
        grid=(n // tn,),
        in_specs=[pl.BlockSpec((MOD_ROWS, d), lambda j: (0, 0)),
                  pl.BlockSpec((d, tn), lambda j: (0, j)),
                  pl.BlockSpec((1, tn), lambda j: (0, j))],
        out_specs=pl.BlockSpec((MOD_ROWS, tn), lambda j: (0, j)),
        compiler_params=_cparams("arbitrary"),
        name="modulation",
    )(cond, w_mod, b_mod.reshape(1, n))


def _resnorm_kernel(*refs, has_delta, res_scale, gate_idx, mod_idx, emit_x, ctx_tiles):
    it = iter(refs)
    x_ref = next(it)
    xl_ref = next(it) if ctx_tiles is not None else None
    delta_ref = next(it) if has_delta else None
    mod_ref = next(it) if (has_delta or mod_idx is not None) else None
    gain_ref = next(it)
    xo_ref = next(it) if emit_x else None
    h_ref = next(it)
    x = x_ref[...]
    if ctx_tiles is not None:
        x = jnp.where(pl.program_id(0) < ctx_tiles, x, xl_ref[...])
    if has_delta:
        g = mod_ref[gate_idx:gate_idx + 1, :]
        x = x + g * (res_scale * delta_ref[...])
    if emit_x:
        xo_ref[...] = x
    y = x * lax.rsqrt(jnp.mean(x * x, axis=-1, keepdims=True) + RMS_EPS)
    y = y * gain_ref[...]
    if mod_idx is not None:
        sh = mod_ref[mod_idx:mod_idx + 1, :]
        sc = mod_ref[mod_idx + 1:mod_idx + 2, :]
        y = y * (1.0 + sc) + sh
    h_ref[...] = y.astype(h_ref.dtype)


def _resnorm(x, delta, mod3, gain, dims, *, res_scale, gate_idx, mod_idx, emit_x, out_dtype,
             tm, row_off=0, rows=None):
    mc, tc, tl, nlb = dims
    n_ctx_tiles = mc // tm
    tiles_per_lat = tl // tm
    ro = row_off // tm
    split_x = isinstance(x, tuple)
    if split_x:
        assert row_off == 0 and rows is None
        d = x[0].shape[1]
        m_all = x[0].shape[0] + x[1].shape[0]
        in_specs = [pl.BlockSpec((tm, d), lambda i: (jnp.minimum(i, n_ctx_tiles - 1), 0)),
                    pl.BlockSpec((tm, d), lambda i: (jnp.maximum(i - n_ctx_tiles, 0), 0))]
        args = list(x)
    else:
        d = x.shape[1]
        m_all = rows if rows is not None else x.shape[0]
        in_specs = [pl.BlockSpec((tm, d), lambda i: (i + ro, 0))]
        args = [x]

    def mod_row(i):
        gi = i + ro
        return jnp.where(gi < n_ctx_tiles, nlb, (gi - n_ctx_tiles) // tiles_per_lat)

    has_delta = delta is not None
    if has_delta:
        in_specs.append(pl.BlockSpec((tm, d), lambda i: (i + ro, 0)))
        args.append(delta)
    if has_delta or mod_idx is not None:
        in_specs.append(pl.BlockSpec((None, N_MOD, d), lambda i: (mod_row(i), 0, 0)))
        args.append(mod3)
    in_specs.append(pl.BlockSpec((1, d), lambda i: (0, 0)))
    args.append(gain.reshape(1, d))
    out_shape, out_specs = [], []
    if emit_x:
        out_shape.append(jax.ShapeDtypeStruct((m_all, d), F32))
        out_specs.append(pl.BlockSpec((tm, d), lambda i: (i, 0)))
    out_shape.append(jax.ShapeDtypeStruct((m_all, d), out_dtype))
    out_specs.append(pl.BlockSpec((tm, d), lambda i: (i, 0)))
    kern = functools.partial(_resnorm_kernel, has_delta=has_delta, res_scale=res_scale,
                             gate_idx=gate_idx, mod_idx=mod_idx, emit_x=emit_x,
                             ctx_tiles=n_ctx_tiles if split_x else None)
    return pl.pallas_call(
        kern, out_shape=out_shape, grid=(m_all // tm,), in_specs=in_specs, out_specs=out_specs,
        compiler_params=_cparams("arbitrary"), name="resnorm",
    )(*args)


def _mm_kernel(a_ref, w_ref, o_ref):
    o_ref[...] = jnp.dot(a_ref[...], w_ref[...], preferred_element_type=F32).astype(o_ref.dtype)


def _matmul(a, w, *, out_dtype, tm, tn, row_off=0, rows=None, name="matmul"):
    k = a.shape[1]
    n = w.shape[1]
    rows = a.shape[0] if rows is None else rows
    ro = row_off // tm
    return pl.pallas_call(
        _mm_kernel,
        out_shape=jax.ShapeDtypeStruct((rows, n), out_dtype),
        grid=(rows // tm, n // tn),
        in_specs=[pl.BlockSpec((tm, k), lambda i, j: (i + ro, 0)),
                  pl.BlockSpec((k, tn), lambda i, j: (0, j))],
        out_specs=pl.BlockSpec((tm, tn), lambda i, j: (i, j)),
        compiler_params=_cparams("arbitrary", "arbitrary"), name=name,
    )(a, w)


def _ffn_kernel(h_ref, wa_ref, wb_ref, wo_ref, o_ref):
    @pl.when(pl.program_id(1) == 0)
    def _():
        o_ref[...] = jnp.zeros_like(o_ref)

    h = h_ref[...]
    a = jnp.dot(h, wa_ref[...], preferred_element_type=F32)
    b = jnp.dot(h, wb_ref[...], preferred_element_type=F32)
    u = (a * jax.nn.sigmoid(a) * b).astype(BF16)
    o_ref[...] += jnp.dot(u, wo_ref[...], preferred_element_type=F32)


def _ffn(h, w_in, w_out, *, tm, tf):
    m, d = h.shape
    dff = w_out.shape[0]
    nf = dff // tf
    return pl.pallas_call(
        _ffn_kernel,
        out_shape=jax.ShapeDtypeStruct((m, d), F32),
        grid=(m // tm, nf),
        in_specs=[pl.BlockSpec((tm, d), lambda i, j: (i, 0)),
                  pl.BlockSpec((d, tf), lambda i, j: (0, j)),
                  pl.BlockSpec((d, tf), lambda i, j: (0, j + nf)),
                  pl.BlockSpec((tf, d), lambda i, j: (j, 0))],
        out_specs=pl.BlockSpec((tm, d), lambda i, j: (i, 0), pipeline_mode=pl.Buffered(1)),
        compiler_params=_cparams("arbitrary", "arbitrary"), name="ffn",
    )(h, w_in, w_in, w_out)


def _merge_kernel(yr_ref, yh_ref, wr_ref, wh_ref, gr_ref, gh_ref, o_ref):
    pr = jnp.dot(yr_ref[...], wr_ref[...], preferred_element_type=F32)
    ph = jnp.dot(yh_ref[...], wh_ref[...], preferred_element_type=F32)
    o_ref[...] = (jax.nn.sigmoid(gr_ref[...]) * pr
                  + jax.nn.sigmoid(gh_ref[...]) * ph).astype(o_ref.dtype)


def _merge(y_r, y_h, w_br, w_bh, z_rest, col_gr, col_gh, *, tm, tn):
    m, kr = y_r.shape
    kh = y_h.shape[1]
    d = w_br.shape[1]
    cgr, cgh = col_gr // tn, col_gh // tn
    return pl.pallas_call(
        _merge_kernel,
        out_shape=jax.ShapeDtypeStruct((m, d), BF16),
        grid=(m // tm, d // tn),
        in_specs=[pl.BlockSpec((tm, kr), lambda i, j: (i, 0)),
                  pl.BlockSpec((tm, kh), lambda i, j: (i, 0)),
                  pl.BlockSpec((kr, tn), lambda i, j: (0, j)),
                  pl.BlockSpec((kh, tn), lambda i, j: (0, j)),
                  pl.BlockSpec((tm, tn), lambda i, j: (i, j + cgr)),
                  pl.BlockSpec((tm, tn), lambda i, j: (i, j + cgh))],
        out_specs=pl.BlockSpec((tm, tn), lambda i, j: (i, j)),
        compiler_params=_cparams("arbitrary", "arbitrary"), name="merge",
    )(y_r, y_h, w_br, w_bh, z_rest, z_rest)


def _log2(n):
    assert n & (n - 1) == 0
    return n.bit_length() - 1


def _head_ones(n, group):
    sh = _log2(group)
    i = lax.broadcasted_iota(jnp.int32, (n, n), 0) >> sh
    j = lax.broadcasted_iota(jnp.int32, (n, n), 1) >> sh
    return jnp.where(i == j, 1.0, 0.0).astype(BF16)


def _rwkv_prep_kernel(z_ref, zp_ref, zn_ref, mup_ref, mun_ref, w0_ref, w2_ref, a0_ref, a2_ref,
                      g2_ref, kk_ref, ka_ref, rk_ref,
                      r_o, k_o, v_o, kkn_o, af_o, ab_o, lwf_o, lwb_o, g_o, bonus_o,
                      *, tm, mc, tc, tl, dr, wl, al, gl):
    row0 = pl.program_id(0) * tm
    in_ctx = row0 < mc
    pos = jnp.where(in_ctx, row0 % tc, (row0 - mc) % tl)
    seqlen = jnp.where(in_ctx, tc, tl)
    keep_prev = jnp.where(pos == 0, 0.0, 1.0)
    keep_next = jnp.where(pos + tm == seqlen, 0.0, 1.0)
    rows = lax.broadcasted_iota(jnp.int32, (tm, 1), 0)

    def shifted(c0, c1):
        z = z_ref[:, c0:c1]
        zp = jnp.where(rows == 0, keep_prev * zp_ref[7:8, c0:c1], pltpu.roll(z, 1, 0))
        zn = jnp.where(rows == tm - 1, keep_next * zn_ref[0:1, c0:c1], pltpu.roll(z, tm - 1, 0))
        return z + mup_ref[:, c0:c1] * (zp - z) + mun_ref[:, c0:c1] * (zn - z)

    bd = _head_ones(LANES, RWKV_HEAD)
    for j in range(dr // LANES):
        c0 = j * LANES
        r_o[:, c0:c0 + LANES] = shifted(c0, c0 + LANES)
        k = shifted(dr + c0, dr + c0 + LANES)
        k_o[:, c0:c0 + LANES] = k
        v_o[:, c0:c0 + LANES] = shifted(2 * dr + c0, 2 * dr + c0 + LANES)
        kk = k * kk_ref[:, c0:c0 + LANES]
        ss = _group_sum(kk * kk, bd)
        kkn_o[:, c0:c0 + LANES] = kk * lax.rsqrt(jnp.maximum(ss, 1e-24))

    c = 3 * dr
    for d, (lw_o, a_o) in enumerate(((lwf_o, af_o), (lwb_o, ab_o))):
        wd = shifted(c + d * wl, c + (d + 1) * wl)
        ad = shifted(c + 2 * wl + d * al, c + 2 * wl + (d + 1) * al)
        wpre = w0_ref[d:d + 1, :] + _dot(jnp.tanh(wd), w2_ref[d])
        lw_o[...] = -DECAY_SCALE * jax.nn.sigmoid(wpre)
        a_o[...] = jax.nn.sigmoid(a0_ref[d:d + 1, :] + _dot(ad, a2_ref[d]))
    gd = shifted(c + 2 * wl + 2 * al, c + 2 * wl + 2 * al + gl)
    g_o[...] = _dot(jax.nn.sigmoid(gd), g2_ref[...])

    for j in range(dr // LANES):
        sl = slice(j * LANES, (j + 1) * LANES)
        r, k, v = r_o[:, sl], k_o[:, sl], v_o[:, sl]
        bonus = 0.0
        for a_o in (af_o, ab_o):
            kd = k * (1.0 + (a_o[:, sl] - 1.0) * ka_ref[:, sl])
            bonus = bonus + _group_sum(r * kd * rk_ref[:, sl], bd) * v
        bonus_o[:, sl] = bonus


def _rwkv_prep(z_rest, p, dims, *, tm, zr_cols, dr, wl, al, gl):
    mc, tc, tl, nlb = dims
    m = z_rest.shape[0]
    nblk8 = m // 8
    t8 = tm // 8
    full = lambda shape: pl.BlockSpec(shape, lambda i: (0,) * len(shape))
    outs = [jax.ShapeDtypeStruct((m, dr), F32)] * 10
    kern = functools.partial(_rwkv_prep_kernel, tm=tm, mc=mc, tc=tc, tl=tl, dr=dr, wl=wl, al=al,
                             gl=gl)
    return pl.pallas_call(
        kern, out_shape=outs, grid=(m // tm,),
        in_specs=[pl.BlockSpec((tm, zr_cols), lambda i: (i, 0)),
                  pl.BlockSpec((8, zr_cols), lambda i: (jnp.maximum(i * t8 - 1, 0), 0)),
                  pl.BlockSpec((8, zr_cols), lambda i: (jnp.minimum((i + 1) * t8, nblk8 - 1), 0)),
                  full((1, zr_cols)), full((1, zr_cols)),
                  full((2, dr)), full((2, wl, dr)), full((2, dr)), full((2, al, dr)),
                  full((gl, dr)), full((1, dr)), full((1, dr)), full((1, dr))],
        out_specs=[pl.BlockSpec((tm, dr), lambda i: (i, 0))] * 10,
        compiler_params=_cparams("arbitrary"), name="rwkv_prep",
    )(z_rest, z_rest, z_rest, p["mu_prev"], p["mu_next"], p["w0"], p["w2"], p["a0"], p["a2"],
      p["g2"], p["k_k"], p["k_a"], p["r_k"])


def _stack_heads(x, lo):
    return jnp.concatenate([jnp.where(lo, x, 0.0), jnp.where(lo, 0.0, x)], axis=0)


SUB_BLOCK = 16
assert CHUNK // SUB_BLOCK == 4


def _rwkv_masks(rev):
    c = CHUNK
    n2 = 2 * c
    ti = lax.broadcasted_iota(jnp.int32, (c, c), 0)
    tj = lax.broadcasted_iota(jnp.int32, (c, c), 1)
    cum = jnp.where((tj >= ti) if rev else (tj <= ti), 1.0, 0.0).astype(BF16)
    cum = jnp.concatenate([cum, cum, cum], axis=1)
    lo = lax.broadcasted_iota(jnp.int32, (c, LANES), 1) < RWKV_HEAD
    ii = lax.broadcasted_iota(jnp.int32, (n2, n2), 0)
    jj = lax.broadcasted_iota(jnp.int32, (n2, n2), 1)
    im, jm = ii & (c - 1), jj & (c - 1)
    strict = (jm > im) if rev else (jm < im)
    incl = (jm >= im) if rev else (jm <= im)
    eye = jnp.where(ii == jj, 1.0, 0.0)
    sub = (ii >> _log2(SUB_BLOCK)) == (jj >> _log2(SUB_BLOCK))
    return cum, lo, strict, incl, eye, sub


def _rwkv_chunk(r, k, v, kk, a, lw, k_a, s, rev, masks):
    c = CHUNK
    n2 = 2 * c
    cum, lo, strict, incl, eye, sub = masks
    bc = _exact_rowmix(cum, lw)
    yield
    bex = bc - lw
    btot = bc[0:1, :] if rev else bc[c - 1:c, :]
    kd = k * (1.0 + (a - 1.0) * k_a)
    bv = kk * a
    e_neg = jnp.exp(-bc)
    e_tot = jnp.exp(btot - bc)
    at = _stack_heads(-kk * jnp.exp(bex), lo).astype(BF16)
    rt_f = _stack_heads(r * jnp.exp(bc), lo)
    kh = _stack_heads(kd * e_neg, lo).astype(BF16)
    bh = _stack_heads(bv * e_neg, lo).astype(BF16)
    vb_f = _stack_heads(v, lo)
    vb = vb_f.astype(BF16)
    kc = _stack_heads(kd * e_tot, lo).astype(BF16)
    bcc = _stack_heads(bv * e_tot, lo).astype(BF16)

    gram = _dot_nt(jnp.concatenate([at, rt_f.astype(BF16)], axis=0),
                   jnp.concatenate([bh, kh], axis=0))
    yield
    nmat = jnp.where(strict, gram[:n2, :n2], 0.0)
    a_ka = jnp.where(strict, gram[:n2, n2:], 0.0)
    a_rb = jnp.where(incl, gram[n2:, :n2], 0.0)
    a_rk = jnp.where(incl, gram[n2:, n2:], 0.0)
    n_d = jnp.where(sub, nmat, 0.0)
    t_d = eye + n_d
    pw = n_d
    for _ in range(_log2(SUB_BLOCK) - 1):
        pw = _dot(pw, pw)
        yield
        t_d = t_d + _dot(t_d, pw)
    x1 = _dot(a_ka, vb)
    yield
    xm = _dot(t_d, jnp.where(sub, 0.0, nmat))
    yield
    xm2 = _dot(xm, xm)
    yield
    ym = eye + xm + xm2 + _dot(xm, xm2)
    yield
    tm_ = _dot(ym, t_d)
    yield
    aw = _dot(tm_, jnp.concatenate([at, x1.astype(BF16)], axis=1))
    yield
    ah, w = aw[:, :n2], aw[:, n2:]
    rhs = jnp.concatenate(
        [jnp.concatenate([vb, jnp.zeros_like(vb)], axis=1),
         jnp.concatenate([w.astype(BF16), ah.astype(BF16)], axis=1)], axis=0)
    qr = _dot(jnp.concatenate([a_rk, a_rb], axis=1), rhs)
    q = qr[:, :n2]
    rr = rt_f + qr[:, n2:]
    mm = _dot_tn(ah, bcc)
    nn = _dot_tn(jnp.concatenate([vb_f, w], axis=0), jnp.concatenate([kc, bcc], axis=0))
    yield
    o_bd = _dot_nt(rr, s) + q
    o = o_bd[:c, :] + o_bd[c:, :]
    s_new = s * jnp.exp(btot) + _dot(s, mm) + nn
    return o, s_new


def _rwkv_scan_kernel(*refs, has_init, emit_final, nchunk, npairs, aliased):
    it = iter(refs)
    fwd = [next(it) for _ in range(6)]
    bwd = [next(it) for _ in range(6)]
    ka_ref = next(it)
    s0_ref = next(it) if has_init else None
    if aliased:
        next(it), next(it)
    of_ref = next(it)
    ob_ref = next(it)
    sfin_ref = next(it) if emit_final else None
    s_scr = next(it)
    ci = pl.program_id(2)

    @pl.when(ci == 0)
    def _():
        if has_init:
            s_scr[...] = s0_ref[...]
        else:
            s_scr[...] = jnp.zeros_like(s_scr)

    chains, sinks = [], []
    for d, (ins, o_ref) in enumerate(((fwd, of_ref), (bwd, ob_ref))):
        masks = _rwkv_masks(rev=(d == 1))
        for hp in range(npairs):
            sl = slice(hp * LANES, (hp + 1) * LANES)
            r, k, v, kk, a, lw = (x[:, sl] for x in ins)
            chains.append(_rwkv_chunk(r, k, v, kk, a, lw, ka_ref[:, sl], s_scr[d, hp],
                                      rev=(d == 1), masks=masks))
            sinks.append((o_ref, sl, d, hp))
    for (o, s_new), (o_ref, sl, d, hp) in zip(_lockstep(chains), sinks):
        o_ref[:, sl] = o
        s_scr[d, hp] = s_new

    if emit_final:
        @pl.when(ci == nchunk - 1)
        def _():
            sfin_ref[...] = s_scr[...]


def _rwkv_scan(pre, k_a, s0_bd, *, row_off, nb, t, emit_final, npairs, into=None):
    m, dr = pre["r"].shape
    ngroup = dr // (LANES * npairs)
    nchunk = t // CHUNK
    rb = row_off // CHUNK
    has_init = s0_bd is not None
    fmap = lambda b, h, c: (rb + b * nchunk + c, h)
    bmap = lambda b, h, c: (rb + b * nchunk + (nchunk - 1 - c), h)
    blk = (CHUNK, LANES * npairs)
    in_specs = ([pl.BlockSpec(blk, fmap)] * 6 + [pl.BlockSpec(blk, bmap)] * 6
                + [pl.BlockSpec((1, LANES * npairs), lambda b, h, c: (0, h))])
    args = [pre["r"], pre["k"], pre["v"], pre["kk"], pre["a_f"], pre["lw_f"],
            pre["r"], pre["k"], pre["v"], pre["kk"], pre["a_b"], pre["lw_b"], k_a]
    sblk = pl.BlockSpec((None, 2, npairs, LANES, LANES), lambda b, h, c: (b, 0, h, 0, 0))
    if has_init:
        in_specs.append(sblk)
        args.append(s0_bd)
    aliases = {}
    if into is not None:
        aliases = {len(args): 0, len(args) + 1: 1}
        in_specs += [pl.BlockSpec(memory_space=pl.ANY)] * 2
        args += list(into)
    out_shape = [jax.ShapeDtypeStruct((m, dr), F32)] * 2
    out_specs = [pl.BlockSpec(blk, fmap), pl.BlockSpec(blk, bmap)]
    if emit_final:
        out_shape.append(jax.ShapeDtypeStruct((nb, 2, dr // LANES, LANES, LANES), F32))
        out_specs.append(sblk)
    kern = functools.partial(_rwkv_scan_kernel, has_init=has_init, emit_final=emit_final,
                             nchunk=nchunk, npairs=npairs, aliased=into is not None)
    return pl.pallas_call(
        kern, out_shape=out_shape, grid=(nb, ngroup, nchunk), in_specs=in_specs,
        out_specs=out_specs, scratch_shapes=[pltpu.VMEM((2, npairs, LANES, LANES), F32)],
        input_output_aliases=aliases,
        compiler_params=_cparams("arbitrary", "arbitrary", "arbitrary"), name="rwkv_scan",
    )(*args)


def _rwkv_post_kernel(of_ref, ob_ref, bonus_ref, g_ref, lnw_ref, lnb_ref, y_ref, *, dr):
    bd = _head_ones(LANES, RWKV_HEAD)
    inv = 1.0 / RWKV_HEAD
    for j in range(dr // LANES):
        sl = slice(j * LANES, (j + 1) * LANES)
        o = of_ref[:, sl] + ob_ref[:, sl]
        mean = _group_sum(o, bd) * inv
        oc = o - mean
        var = _group_sum(oc * oc, bd) * inv
        on = oc * lax.rsqrt(var + GN_EPS) * lnw_ref[:, sl] + lnb_ref[:, sl]
        y_ref[:, sl] = ((on + bonus_ref[:, sl]) * g_ref[:, sl]).astype(y_ref.dtype)


def _rwkv_post(o_f, o_b, pre, p, *, tm):
    rows, dr = o_f.shape
    own = pl.BlockSpec((tm, dr), lambda i: (i, 0))
    par = pl.BlockSpec((1, dr), lambda i: (0, 0))
    return pl.pallas_call(
        functools.partial(_rwkv_post_kernel, dr=dr),
        out_shape=jax.ShapeDtypeStruct((rows, dr), BF16),
        grid=(rows // tm,),
        in_specs=[own] * 4 + [par] * 2,
        out_specs=own,
        compiler_params=_cparams("arbitrary"), name="rwkv_post",
    )(o_f, o_b, pre["bonus"], pre["g"], p["ln_w"], p["ln_b"])


HGRN_LEVELS = 6


SUBLANES = 8


def _boundary_rows(b, lvl, rev, sub):
    c = CHUNK
    half = 1 << lvl
    blk = 2 * half
    src = (lambda base: base + half) if rev else (lambda base: base + half - 1)

    def rows(starts, nrows):
        return jnp.concatenate(
            [jnp.broadcast_to(b[src(s):src(s) + 1, :], (nrows, LANES)) for s in starts], axis=0)

    if blk >= SUBLANES:
        return rows(range(0, c, blk), blk)
    out = None
    for k in range(SUBLANES // blk):
        arr = rows(range(k * blk, c, SUBLANES), SUBLANES)
        out = arr if out is None else jnp.where(sub >= k * blk, arr, out)
    return out


def _hgrn_masks(rev):
    c = CHUNK
    ti = lax.broadcasted_iota(jnp.int32, (c, c), 0)
    tj = lax.broadcasted_iota(jnp.int32, (c, c), 1)
    late, early = (tj, ti) if rev else (ti, tj)
    sels = [((late >> lvl) == (early >> lvl) + 1) & (((early >> lvl) & 1) == 0)
            for lvl in range(HGRN_LEVELS)]
    cum = jnp.where((tj >= ti) if rev else (tj <= ti), 1.0, 0.0).astype(BF16)
    sub = lax.broadcasted_iota(jnp.int32, (c, LANES), 0) & (SUBLANES - 1)
    return jnp.concatenate([cum, cum, cum], axis=1), ti == tj, sels, sub


def _hgrn_chunk(q_pre, f_pre, v, lb, st, rev, masks):
    c = CHUNK
    cum3, diag, sels, sub = masks
    q = q_pre * jax.nn.sigmoid(q_pre)
    f = lb + (1.0 - lb) * jax.nn.sigmoid(f_pre)
    kf = 1.0 - f
    b = _exact_rowmix(cum3, jnp.log(f))
    yield
    btot = b[0:1, :] if rev else b[c - 1:c, :]
    q_b, kf_b = q.astype(BF16), kf.astype(BF16)
    attn = jnp.where(diag, _dot_nt(q_b, kf_b), 0.0)
    for lvl in range(HGRN_LEVELS):
        dq = b - _boundary_rows(b, lvl, rev, sub)
        wgt = jnp.exp(-jnp.abs(dq)).astype(BF16)
        attn = jnp.where(sels[lvl], _dot_nt(q_b * wgt, kf_b * wgt), attn)
    yield
    o = _dot_nt(q * jnp.exp(b), st) + _dot(attn, v)
    st_new = st * jnp.exp(btot) + _dot_tn(v, kf * jnp.exp(btot - b))
    return o, st_new


def _hgrn_scan_kernel(*refs, has_init, emit_final, nsteps, nheads, cols, aliased):
    it = iter(refs)
    qf_ref, ff_ref, vf_ref = next(it), next(it), next(it)
    qb_ref, fb_ref, vb_ref = next(it), next(it), next(it)
    lb_ref = next(it)
    s0_ref = next(it) if has_init else None
    if aliased:
        next(it), next(it)
    of_ref, ob_ref = next(it), next(it)
    sfin_ref = next(it) if emit_final else None
    s_scr = next(it)
    in_scr, out_scr = (next(it), next(it)) if cols is not None else (None, None)
    ci = pl.program_id(2)

    @pl.when(ci == 0)
    def _():
        if has_init:
            for d in range(2):
                for h in range(nheads):
                    s_scr[d, h] = s0_ref[d, h].T
        else:
            s_scr[...] = jnp.zeros_like(s_scr)

    ins = ((qf_ref, ff_ref, vf_ref), (qb_ref, fb_ref, vb_ref))
    outs = (of_ref, ob_ref)
    all_masks = [_hgrn_masks(rev=False), _hgrn_masks(rev=True)]

    def advance(load, store):
        chains, sinks = [], []
        for d in range(2):
            for h in range(nheads):
                sl = slice(h * LANES, (h + 1) * LANES)
                qp, fp, v = (load(d, k, sl) for k in range(3))
                chains.append(_hgrn_chunk(qp, fp, v, lb_ref[:, sl], s_scr[d, h], rev=(d == 1),
                                          masks=all_masks[d]))
                sinks.append((d, h, sl))
        for (o, s_new), (d, h, sl) in zip(_lockstep(chains), sinks):
            store(d, sl, o)
            s_scr[d, h] = s_new

    if cols is None:
        def store_rows(d, sl, o):
            outs[d][:, sl] = o
        advance(lambda d, k, sl: ins[d][k][:, sl], store_rows)
    else:
        for d in range(2):
            for k in range(3):
                in_scr[d, k] = jnp.swapaxes(ins[d][k][...], 0, 1)

        def body(j, carry):
            col = (j, cols - 1 - j)

            def store_col(d, sl, o):
                out_scr[d, col[d], :, sl] = o
            advance(lambda d, k, sl: in_scr[d, k, col[d], :, sl], store_col)
            return carry
        lax.fori_loop(0, cols, body, 0)
        for d in range(2):
            outs[d][...] = jnp.swapaxes(out_scr[d], 0, 1)

    if emit_final:
        @pl.when(ci == nsteps - 1)
        def _():
            for d in range(2):
                for h in range(nheads):
                    sfin_ref[d, h] = s_scr[d, h].T


HGRN_COLS = 8


def _hgrn_scan(zhs, lb, s0, *, row_off, nb, t, dg, emit_final, nheads, column_major,
               into=None):
    m = zhs.shape[0]
    ng = dg // (LANES * nheads)
    has_init = s0 is not None
    lanes = LANES * nheads
    if column_major:
        nrow = t // GRID_W
        assert nrow == CHUNK and m % t == 0 and row_off % t == 0
        nsteps = GRID_W // HGRN_COLS
        seq0 = row_off // t
        zsrc = zhs.reshape(m // t, nrow, GRID_W, 4 * dg)
        blk = (None, nrow, HGRN_COLS, lanes)
        fwd = lambda g: (lambda b, h, c: (b + seq0, 0, c, g * ng + h))
        bwd = lambda g: (lambda b, h, c: (b + seq0, 0, nsteps - 1 - c, g * ng + h))
        out_dims = (m // t, nrow, GRID_W, dg)
        ofwd = lambda b, h, c: (b + seq0, 0, c, h)
        obwd = lambda b, h, c: (b + seq0, 0, nsteps - 1 - c, h)
        cols = HGRN_COLS
    else:
        nsteps = t // CHUNK
        rb = row_off // CHUNK
        zsrc = zhs
        blk = (CHUNK, lanes)
        fwd = lambda g: (lambda b, h, c: (rb + b * nsteps + c, g * ng + h))
        bwd = lambda g: (lambda b, h, c: (rb + b * nsteps + nsteps - 1 - c, g * ng + h))
        out_dims = (m, dg)
        ofwd = lambda b, h, c: (rb + b * nsteps + c, h)
        obwd = lambda b, h, c: (rb + b * nsteps + nsteps - 1 - c, h)
        cols = None

    in_specs = [pl.BlockSpec(blk, fwd(0)), pl.BlockSpec(blk, fwd(1)), pl.BlockSpec(blk, fwd(3)),
                pl.BlockSpec(blk, bwd(0)), pl.BlockSpec(blk, bwd(2)), pl.BlockSpec(blk, bwd(3)),
                pl.BlockSpec((1, lanes), lambda b, h, c: (0, h))]
    args = [zsrc] * 6 + [lb]
    sblk = pl.BlockSpec((None, 2, nheads, LANES, LANES), lambda b, h, c: (b, 0, h, 0, 0))
    if has_init:
        in_specs.append(sblk)
        args.append(s0)
    aliases = {}
    if into is not None:
        aliases = {len(args): 0, len(args) + 1: 1}
        in_specs += [pl.BlockSpec(memory_space=pl.ANY)] * 2
        args += [o.reshape(out_dims) for o in into]
    out_shape = [jax.ShapeDtypeStruct(out_dims, F32)] * 2
    out_specs = [pl.BlockSpec(blk, ofwd), pl.BlockSpec(blk, obwd)]
    if emit_final:
        out_shape.append(jax.ShapeDtypeStruct((nb, 2, dg // LANES, LANES, LANES), F32))
        out_specs.append(sblk)
    kern = functools.partial(_hgrn_scan_kernel, has_init=has_init, emit_final=emit_final,
                             nsteps=nsteps, nheads=nheads, cols=cols, aliased=into is not None)
    scratch = [pltpu.VMEM((2, nheads, LANES, LANES), F32)]
    if column_major:
        scratch += [pltpu.VMEM((2, 3, cols, CHUNK, lanes), F32),
                    pltpu.VMEM((2, cols, CHUNK, lanes), F32)]
    outs = pl.pallas_call(
        kern, out_shape=out_shape, grid=(nb, ng, nsteps), in_specs=in_specs, out_specs=out_specs,
        scratch_shapes=scratch, input_output_aliases=aliases,
        compiler_params=_cparams("arbitrary", "arbitrary", "arbitrary"), name="hgrn_scan",
    )(*args)
    return [outs[0].reshape(m, dg), outs[1].reshape(m, dg)] + list(outs[2:])


def _hgrn_post_kernel(of_ref, ob_ref, gate_ref, gain_ref, y_ref, *, dg):
    gain = gain_ref[...]
    for j in range(dg // LANES):
        sl = slice(j * LANES, (j + 1) * LANES)
        o = of_ref[:, sl] + ob_ref[:, sl]
        on = o * lax.rsqrt(jnp.mean(o * o, axis=-1, keepdims=True) + RMS_EPS) * gain
        gt = gate_ref[:, sl]
        y_ref[:, sl] = (on * (gt * jax.nn.sigmoid(gt))).astype(y_ref.dtype)


def _hgrn_post(o_f, o_b, z_gates, gain, *, gate_col):
    rows_all, dg = o_f.shape
    kern = functools.partial(_hgrn_post_kernel, dg=dg)
    gcb = gate_col // dg
    par = pl.BlockSpec((1, LANES), lambda *_: (0, 0))
    tm = min(256, rows_all)
    ospec = pl.BlockSpec((tm, dg), lambda i: (i, 0))
    return pl.pallas_call(
        kern, out_shape=jax.ShapeDtypeStruct((rows_all, dg), BF16), grid=(rows_all // tm,),
        in_specs=[ospec, ospec, pl.BlockSpec((tm, dg), lambda i: (i, gcb)), par],
        out_specs=ospec,
        compiler_params=_cparams("arbitrary"), name="hgrn_post",
    )(o_f, o_b, z_gates, gain)


def _largest_divisor(n, candidates):
    for cand in candidates:
        if n % cand == 0:
            return cand
    raise ValueError(f"no tile in {candidates} divides {n}")


def _pad_to(x, axis, size):
    pad = size - x.shape[axis]
    if pad == 0:
        return x
    widths = [(0, 0)] * x.ndim
    widths[axis] = (0, pad)
    return jnp.pad(x, widths)


def _pair_state_to_bd(s):
    b, two, h, n, _ = s.shape
    s = s.reshape(b, two, h // 2, 2, n, n)
    z = jnp.zeros_like(s[:, :, :, 0])
    top = jnp.concatenate([s[:, :, :, 0], z], axis=-1)
    bot = jnp.concatenate([z, s[:, :, :, 1]], axis=-1)
    return jnp.concatenate([top, bot], axis=-2)


def _bd_to_pair_state(sbd):
    b, two, hp, n2, _ = sbd.shape
    n = n2 // 2
    s0 = sbd[:, :, :, :n, :n]
    s1 = sbd[:, :, :, n:, n:]
    return jnp.stack([s0, s1], axis=3).reshape(b, two, hp * 2, n, n)


def kernel(x_prompt, x_sample, state_rwkv, state_hgrn, c, c_ctx, w_mod, b_mod, norm_ffn1, norm_mix, norm_ffn2, ffn1_w_in, ffn1_w_out, ffn2_w_in, ffn2_w_out, w_in, rwkv_mu_prev, rwkv_mu_next, rwkv_w0, rwkv_w2, rwkv_a0, rwkv_a2, rwkv_g2, rwkv_k_k, rwkv_k_a, rwkv_r_k, rwkv_ln_w, rwkv_ln_b, hgrn_lb_logits, hgrn_norm, w_branch_rwkv, w_branch_hgrn, w_out, norm_final):
    nbc, tc, d = x_prompt.shape
    nbl, tl, _ = x_sample.shape
    depth = w_mod.shape[0]
    mc, ml = nbc * tc, nbl * tl
    m = mc + ml
    dims = (mc, tc, tl, nbl)
    dr = rwkv_w0.shape[-1]
    dg = w_branch_hgrn.shape[1]
    dff = ffn1_w_out.shape[1]
    wl, al, gl = rwkv_w2.shape[2], rwkv_a2.shape[2], rwkv_g2.shape[1]
    glp = -(-gl // LANES) * LANES
    rw_cols = 3 * dr + 2 * wl + 2 * al + gl
    zr_cols = 3 * dr + 2 * wl + 2 * al + glp
    assert nbl < MOD_ROWS and tl % GRID_W == 0 and tl // GRID_W == CHUNK and tc % CHUNK == 0

    tm_norm = min(256, tc)
    tm_proj = min(1024, mc)
    tm_merge = min(512, mc)
    tn_of = lambda n: _largest_divisor(n, (1024, 512, 256, 128))
    tf = _largest_divisor(dff, (256, 128))

    lb_all = jnp.cumsum(jax.nn.softmax(hgrn_lb_logits.astype(F32), axis=0), axis=0)
    cond = jnp.concatenate([c, c_ctx[None, :], jnp.zeros((MOD_ROWS - nbl - 1, d), F32)], axis=0)

    x = (x_prompt.reshape(mc, d), x_sample.reshape(ml, d))
    new_r, new_h = [], []
    for l in range(depth):
        mod3 = _modulation(cond, w_mod[l], b_mod[l], tn_of(N_MOD * d)).reshape(MOD_ROWS, N_MOD, d)

        h = _resnorm(x, None, mod3, norm_ffn1[l], dims, res_scale=None, gate_idx=None, mod_idx=0,
                     emit_x=False, out_dtype=BF16, tm=tm_norm)[0]
        delta = _ffn(h, ffn1_w_in[l].astype(BF16), ffn1_w_out[l].astype(BF16), tm=tm_proj, tf=tf)

        x, h = _resnorm(x, delta, mod3, norm_mix[l], dims, res_scale=FFN_HALF, gate_idx=2,
                        mod_idx=3, emit_x=True, out_dtype=BF16, tm=tm_norm)
        wi = w_in[l]
        hg0 = rw_cols
        w_r = _pad_to(wi[:, :rw_cols], 1, zr_cols).astype(BF16)
        w_hs = wi[:, hg0:hg0 + 4 * dg].astype(BF16)
        w_gates = wi[:, hg0 + 4 * dg:].astype(BF16)
        col_gate, col_gr, col_gh = 0, dg, dg + d
        z_r = _matmul(h, w_r, out_dtype=F32, tm=tm_proj, tn=tn_of(zr_cols), name="proj_rwkv")
        z_g = _matmul(h, w_gates, out_dtype=F32, tm=tm_proj, tn=tn_of(dg), name="proj_gates")
        zhs = _matmul(h, w_hs, out_dtype=F32, tm=tm_proj, tn=tn_of(dg), name="proj_hgrn")

        p = {
            "mu_prev": _pad_to(rwkv_mu_prev[l][None, :], 1, zr_cols),
            "mu_next": _pad_to(rwkv_mu_next[l][None, :], 1, zr_cols),
            "w0": rwkv_w0[l], "w2": rwkv_w2[l].astype(BF16),
            "a0": rwkv_a0[l], "a2": rwkv_a2[l].astype(BF16),
            "g2": _pad_to(rwkv_g2[l], 0, glp).astype(BF16),
            "k_k": rwkv_k_k[l][None, :], "k_a": rwkv_k_a[l][None, :],
            "r_k": rwkv_r_k[l].reshape(1, dr),
            "ln_w": rwkv_ln_w[l][None, :], "ln_b": rwkv_ln_b[l][None, :],
        }
        names = ("r", "k", "v", "kk", "a_f", "a_b", "lw_f", "lw_b", "g", "bonus")
        pre = dict(zip(names, _rwkv_prep(z_r, p, dims, tm=min(128, tc), zr_cols=zr_cols, dr=dr,
                                         wl=wl, al=al, gl=glp)))
        npairs = _largest_divisor(dr // LANES, (16, 8, 4, 2, 1))
        of_c, ob_c, sr_c = _rwkv_scan(pre, p["k_a"], None, row_off=0, nb=nbc, t=tc,
                                      emit_final=True, npairs=npairs)
        o_f, o_b = _rwkv_scan(pre, p["k_a"], _pair_state_to_bd(state_rwkv[:, l].astype(F32)),
                              row_off=mc, nb=nbl, t=tl, emit_final=False, npairs=npairs,
                              into=(of_c, ob_c))
        y_r = _rwkv_post(o_f, o_b, pre, p, tm=min(256, tc))
        new_r.append(_bd_to_pair_state(sr_c))

        lb = lb_all[l][None, :]
        gain_h = hgrn_norm[l][None, :]
        nheads = _largest_divisor(dg // LANES, (8, 4, 2, 1))
        nheads_cols = _largest_divisor(dg // LANES, (4, 2, 1))
        hf_c, hb_c, sh_c = _hgrn_scan(zhs, lb, None, row_off=0, nb=nbc, t=tc, dg=dg,
                                      emit_final=True, nheads=nheads, column_major=False)
        h_f, h_b = _hgrn_scan(zhs, lb, state_hgrn[:, l].astype(F32), row_off=mc, nb=nbl, t=tl,
                              dg=dg, emit_final=False, nheads=nheads_cols, column_major=True,
                              into=(hf_c, hb_c))
        y_h = _hgrn_post(h_f, h_b, z_g, gain_h, gate_col=col_gate)
        new_h.append(sh_c)

        merged = _merge(y_r, y_h, w_branch_rwkv[l].astype(BF16), w_branch_hgrn[l].astype(BF16),
                        z_g, col_gr, col_gh, tm=tm_merge, tn=tn_of(dg))
        delta = _matmul(merged, w_out[l].astype(BF16), out_dtype=F32, tm=tm_proj, tn=tn_of(d),
                        name="mix_out")

        x, h = _resnorm(x, delta, mod3, norm_ffn2[l], dims, res_scale=1.0, gate_idx=5, mod_idx=6,
                        emit_x=True, out_dtype=BF16, tm=tm_norm)
        delta = _ffn(h, ffn2_w_in[l].astype(BF16), ffn2_w_out[l].astype(BF16), tm=tm_proj, tf=tf)
        last = l == depth - 1
        if not last:
            x = _resnorm(x, delta, mod3, norm_ffn2[l], dims, res_scale=FFN_HALF, gate_idx=8,
                         mod_idx=None, emit_x=True, out_dtype=F32, tm=tm_norm)[0]

    y_c = _resnorm(x, delta, mod3, norm_final, dims, res_scale=FFN_HALF, gate_idx=8, mod_idx=None,
                   emit_x=False, out_dtype=F32, tm=tm_norm, row_off=0, rows=mc)[0]
    y_l = _resnorm(x, delta, mod3, norm_final, dims, res_scale=FFN_HALF, gate_idx=8, mod_idx=None,
                   emit_x=False, out_dtype=F32, tm=tm_norm, row_off=mc, rows=ml)[0]
    return (y_c.reshape(nbc, tc, d), y_l.reshape(nbl, tl, d),
            jnp.stack(new_r, axis=1), jnp.stack(new_h, axis=1))
```

```python
import functools
import math

import jax
import jax.numpy as jnp
from jax import lax
from jax.experimental import pallas as pl
from jax.experimental.pallas import tpu as pltpu

F32 = jnp.float32
BF16 = jnp.bfloat16

CHUNK = 64
GRID_W = 64
RWKV_HEAD = 64
HGRN_EXPAND = 128
N_MOD = 9
RMS_EPS = 1e-6
GN_EPS = 64e-5
FFN_HALF = 0.5
DECAY_SCALE = math.exp(-0.5)
LANES = 128
MOD_ROWS = 16
VMEM_LIMIT_BYTES = 56 * 1024 * 1024


def _cparams(*sem):
    return pltpu.CompilerParams(dimension_semantics=sem, vmem_limit_bytes=VMEM_LIMIT_BYTES)


def _dot(a, b):
    return jnp.dot(a.astype(BF16), b.astype(BF16), preferred_element_type=F32)


def _dot_nt(a, b):
    return lax.dot_general(a.astype(BF16), b.astype(BF16), (((1,), (1,)), ((), ())),
                           preferred_element_type=F32)


def _dot_tn(a, b):
    return jnp.dot(a.T.astype(BF16), b.astype(BF16), preferred_element_type=F32)


def _split2(x):
    hi = x.astype(BF16)
    return hi, (x - hi.astype(F32)).astype(BF16)


def _split3(x):
    x1 = x.astype(BF16)
    d1 = x - x1.astype(F32)
    x2 = d1.astype(BF16)
    x3 = (d1 - x2.astype(F32)).astype(BF16)
    return x1, x2, x3


def _exact_rowmix(mat3, x):
    return jnp.dot(mat3, jnp.concatenate(_split3(x), axis=0), preferred_element_type=F32)


def _group_sum(x, bd_ones):
    x1, x2 = _split2(x)
    return (jnp.dot(x1, bd_ones, preferred_element_type=F32)
            + jnp.dot(x2, bd_ones, preferred_element_type=F32))


def _lockstep(chains):
    results = [None] * len(chains)
    active = list(enumerate(chains))
    while active:
        still = []
        for idx, chain in active:
            try:
                next(chain)
                still.append((idx, chain))
            except StopIteration as done:
                results[idx] = done.value
        active = still
    return results


def _mod_kernel(c_ref, w_ref, b_ref, o_ref):
    c = c_ref[...]
    s = c * jax.nn.sigmoid(c)
    o_ref[...] = _dot(s, w_ref[...]) + b_ref[...]


def _modulation(cond, w_mod, b_mod, tn):
    d, n = w_mod.shape
    return pl.pallas_call(
        _mod_kernel,
        out_shape=jax.ShapeDtypeStruct((MOD_ROWS, n), F32),
        grid=(n // tn,),
        in_specs=[pl.BlockSpec((MOD_ROWS, d), lambda j: (0, 0)),
                  pl.BlockSpec((d, tn), lambda j: (0, j)),
                  pl.BlockSpec((1, tn), lambda j: (0, j))],
        out_specs=pl.BlockSpec((MOD_ROWS, tn), lambda j: (0, j)),
        compiler_params=_cparams("arbitrary"),
        name="modulation",
    )(cond, w_mod, b_mod.reshape(1, n))


def _resnorm_kernel(*refs, has_delta, res_scale, gate_idx, mod_idx, emit_x, ctx_tiles):
    it = iter(refs)
    x_ref = next(it)
    xl_ref = next(it) if ctx_tiles is not None else None
    delta_ref = next(it) if has_delta else None
    mod_ref = next(it) if (has_delta or mod_idx is not None) else None
    gain_ref = next(it)
    xo_ref = next(it) if emit_x else None
    h_ref = next(it)
    x = x_ref[...]
    if ctx_tiles is not None:
        x = jnp.where(pl.program_id(0) < ctx_tiles, x, xl_ref[...])
    if has_delta:
        g = mod_ref[gate_idx:gate_idx + 1, :]
        x = x + g * (res_scale * delta_ref[...])
    if emit_x:
        xo_ref[...] = x
    y = x * lax.rsqrt(jnp.mean(x * x, axis=-1, keepdims=True) + RMS_EPS)
    y = y * gain_ref[...]
    if mod_idx is not None:
        sh = mod_ref[mod_idx:mod_idx + 1, :]
        sc = mod_ref[mod_idx + 1:mod_idx + 2, :]
        y = y * (1.0 + sc) + sh
    h_ref[...] = y.astype(h_ref.dtype)


def _resnorm(x, delta, mod3, gain, dims, *, res_scale, gate_idx, mod_idx, emit_x, out_dtype,
             tm, row_off=0, rows=None):
    mc, tc, tl, nlb = dims
    n_ctx_tiles = mc // tm
    tiles_per_lat = tl // tm
    ro = row_off // tm
    split_x = isinstance(x, tuple)
    if split_x:
        assert row_off == 0 and rows is None
        d = x[0].shape[1]
        m_all = x[0].shape[0] + x[1].shape[0]
        in_specs = [pl.BlockSpec((tm, d), lambda i: (jnp.minimum(i, n_ctx_tiles - 1), 0)),
                    pl.BlockSpec((tm, d), lambda i: (jnp.maximum(i - n_ctx_tiles, 0), 0))]
        args = list(x)
    else:
        d = x.shape[1]
        m_all = rows if rows is not None else x.shape[0]
        in_specs = [pl.BlockSpec((tm, d), lambda i: (i + ro, 0))]
        args = [x]

    def mod_row(i):
        gi = i + ro
        return jnp.where(gi < n_ctx_tiles, nlb, (gi - n_ctx_tiles) // tiles_per_lat)

    has_delta = delta is not None
    if has_delta:
        in_specs.append(pl.BlockSpec((tm, d), lambda i: (i + ro, 0)))
        args.append(delta)
    if has_delta or mod_idx is not None:
        in_specs.append(pl.BlockSpec((None, N_MOD, d), lambda i: (mod_row(i), 0, 0)))
        args.append(mod3)
    in_specs.append(pl.BlockSpec((1, d), lambda i: (0, 0)))
    args.append(gain.reshape(1, d))
    out_shape, out_specs = [], []
    if emit_x:
        out_shape.append(jax.ShapeDtypeStruct((m_all, d), F32))
        out_specs.append(pl.BlockSpec((tm, d), lambda i: (i, 0)))
    out_shape.append(jax.ShapeDtypeStruct((m_all, d), out_dtype))
    out_specs.append(pl.BlockSpec((tm, d), lambda i: (i, 0)))
    kern = functools.partial(_resnorm_kernel, has_delta=has_delta, res_scale=res_scale,
                             gate_idx=gate_idx, mod_idx=mod_idx, emit_x=emit_x,
                             ctx_tiles=n_ctx_tiles if split_x else None)
    return pl.pallas_call(
        kern, out_shape=out_shape, grid=(m_all // tm,), in_specs=in_specs, out_specs=out_specs,
        compiler_params=_cparams("arbitrary"), name="resnorm",
    )(*args)


def _mm_kernel(a_ref, w_ref, o_ref):
    o_ref[...] = jnp.dot(a_ref[...], w_ref[...], preferred_element_type=F32).astype(o_ref.dtype)


def _matmul(a, w, *, out_dtype, tm, tn, row_off=0, rows=None, name="matmul"):
    k = a.shape[1]
    n = w.shape[1]
    rows = a.shape[0] if rows is None else rows
    ro = row_off // tm
    return pl.pallas_call(
        _mm_kernel,
        out_shape=jax.ShapeDtypeStruct((rows, n), out_dtype),
        grid=(rows // tm, n // tn),
        in_specs=[pl.BlockSpec((tm, k), lambda i, j: (i + ro, 0)),
                  pl.BlockSpec((k, tn), lambda i, j: (0, j))],
        out_specs=pl.BlockSpec((tm, tn), lambda i, j: (i, j)),
        compiler_params=_cparams("arbitrary", "arbitrary"), name=name,
    )(a, w)


def _ffn_kernel(h_ref, wa_ref, wb_ref, wo_ref, o_ref):
    @pl.when(pl.program_id(1) == 0)
    def _():
        o_ref[...] = jnp.zeros_like(o_ref)

    h = h_ref[...]
    a = jnp.dot(h, wa_ref[...], preferred_element_type=F32)
    b = jnp.dot(h, wb_ref[...], preferred_element_type=F32)
    u = (a * jax.nn.sigmoid(a) * b).astype(BF16)
    o_ref[...] += jnp.dot(u, wo_ref[...], preferred_element_type=F32)


def _ffn(h, w_in, w_out, *, tm, tf):
    m, d = h.shape
    dff = w_out.shape[0]
    nf = dff // tf
    return pl.pallas_call(
        _ffn_kernel,
        out_shape=jax.ShapeDtypeStruct((m, d), F32),
        grid=(m // tm, nf),
        in_specs=[pl.BlockSpec((tm, d), lambda i, j: (i, 0)),
                  pl.BlockSpec((d, tf), lambda i, j: (0, j)),
                  pl.BlockSpec((d, tf), lambda i, j: (0, j + nf)),
                  pl.BlockSpec((tf, d), lambda i, j: (j, 0))],
        out_specs=pl.BlockSpec((tm, d), lambda i, j: (i, 0), pipeline_mode=pl.Buffered(1)),
        compiler_params=_cparams("arbitrary", "arbitrary"), name="ffn",
    )(h, w_in, w_in, w_out)


def _merge_kernel(yr_ref, yh_ref, wr_ref, wh_ref, gr_ref, gh_ref, o_ref):
    pr = jnp.dot(yr_ref[...], wr_ref[...], preferred_element_type=F32)
    ph = jnp.dot(yh_ref[...], wh_ref[...], preferred_element_type=F32)
    o_ref[...] = (jax.nn.sigmoid(gr_ref[...]) * pr
                  + jax.nn.sigmoid(gh_ref[...]) * ph).astype(o_ref.dtype)


def _merge(y_r, y_h, w_br, w_bh, z_rest, col_gr, col_gh, *, tm, tn):
    m, kr = y_r.shape
    kh = y_h.shape[1]
    d = w_br.shape[1]
    cgr, cgh = col_gr // tn, col_gh // tn
    return pl.pallas_call(
        _merge_kernel,
        out_shape=jax.ShapeDtypeStruct((m, d), BF16),
        grid=(m // tm, d // tn),
        in_specs=[pl.BlockSpec((tm, kr), lambda i, j: (i, 0)),
                  pl.BlockSpec((tm, kh), lambda i, j: (i, 0)),
                  pl.BlockSpec((kr, tn), lambda i, j: (0, j)),
                  pl.BlockSpec((kh, tn), lambda i, j: (0, j)),
                  pl.BlockSpec((tm, tn), lambda i, j: (i, j + cgr)),
                  pl.BlockSpec((tm, tn), lambda i, j: (i, j + cgh))],
        out_specs=pl.BlockSpec((tm, tn), lambda i, j: (i, j)),
        compiler_params=_cparams("arbitrary", "arbitrary"), name="merge",
    )(y_r, y_h, w_br, w_bh, z_rest, z_rest)


def _log2(n):
    assert n & (n - 1) == 0
    return n.bit_length() - 1


def _head_ones(n, group):
    sh = _log2(group)
    i = lax.broadcasted_iota(jnp.int32, (n, n), 0) >> sh
    j = lax.broadcasted_iota(jnp.int32, (n, n), 1) >> sh
    return jnp.where(i == j, 1.0, 0.0).astype(BF16)


def _rwkv_prep_kernel(z_ref, zp_ref, zn_ref, mup_ref, mun_ref, w0_ref, w2_ref, a0_ref, a2_ref,
                      g2_ref, kk_ref, ka_ref, rk_ref,
                      r_o, k_o, v_o, kkn_o, af_o, ab_o, lwf_o, lwb_o, g_o, bonus_o,
                      *, tm, mc, tc, tl, dr, wl, al, gl):
    row0 = pl.program_id(0) * tm
    in_ctx = row0 < mc
    pos = jnp.where(in_ctx, row0 % tc, (row0 - mc) % tl)
    seqlen = jnp.where(in_ctx, tc, tl)
    keep_prev = jnp.where(pos == 0, 0.0, 1.0)
    keep_next = jnp.where(pos + tm == seqlen, 0.0, 1.0)
    rows = lax.broadcasted_iota(jnp.int32, (tm, 1), 0)

    def shifted(c0, c1):
        z = z_ref[:, c0:c1]
        zp = jnp.where(rows == 0, keep_prev * zp_ref[7:8, c0:c1], pltpu.roll(z, 1, 0))
        zn = jnp.where(rows == tm - 1, keep_next * zn_ref[0:1, c0:c1], pltpu.roll(z, tm - 1, 0))
        return z + mup_ref[:, c0:c1] * (zp - z) + mun_ref[:, c0:c1] * (zn - z)

    bd = _head_ones(LANES, RWKV_HEAD)
    for j in range(dr // LANES):
        c0 = j * LANES
        r_o[:, c0:c0 + LANES] = shifted(c0, c0 + LANES)
        k = shifted(dr + c0, dr + c0 + LANES)
        k_o[:, c0:c0 + LANES] = k
        v_o[:, c0:c0 + LANES] = shifted(2 * dr + c0, 2 * dr + c0 + LANES)
        kk = k * kk_ref[:, c0:c0 + LANES]
        ss = _group_sum(kk * kk, bd)
        kkn_o[:, c0:c0 + LANES] = kk * lax.rsqrt(jnp.maximum(ss, 1e-24))

    c = 3 * dr
    for d, (lw_o, a_o) in enumerate(((lwf_o, af_o), (lwb_o, ab_o))):
        wd = shifted(c + d * wl, c + (d + 1) * wl)
        ad = shifted(c + 2 * wl + d * al, c + 2 * wl + (d + 1) * al)
        wpre = w0_ref[d:d + 1, :] + _dot(jnp.tanh(wd), w2_ref[d])
        lw_o[...] = -DECAY_SCALE * jax.nn.sigmoid(wpre)
        a_o[...] = jax.nn.sigmoid(a0_ref[d:d + 1, :] + _dot(ad, a2_ref[d]))
    gd = shifted(c + 2 * wl + 2 * al, c + 2 * wl + 2 * al + gl)
    g_o[...] = _dot(jax.nn.sigmoid(gd), g2_ref[...])

    for j in range(dr // LANES):
        sl = slice(j * LANES, (j + 1) * LANES)
        r, k, v = r_o[:, sl], k_o[:, sl], v_o[:, sl]
        bonus = 0.0
        for a_o in (af_o, ab_o):
            kd = k * (1.0 + (a_o[:, sl] - 1.0) * ka_ref[:, sl])
            bonus = bonus + _group_sum(r * kd * rk_ref[:, sl], bd) * v
        bonus_o[:, sl] = bonus


def _rwkv_prep(z_rest, p, dims, *, tm, zr_cols, dr, wl, al, gl):
    mc, tc, tl, nlb = dims
    m = z_rest.shape[0]
    nblk8 = m // 8
    t8 = tm // 8
    full = lambda shape: pl.BlockSpec(shape, lambda i: (0,) * len(shape))
    outs = [jax.ShapeDtypeStruct((m, dr), F32)] * 10
    kern = functools.partial(_rwkv_prep_kernel, tm=tm, mc=mc, tc=tc, tl=tl, dr=dr, wl=wl, al=al,
                             gl=gl)
    return pl.pallas_call(
        kern, out_shape=outs, grid=(m // tm,),
        in_specs=[pl.BlockSpec((tm, zr_cols), lambda i: (i, 0)),
                  pl.BlockSpec((8, zr_cols), lambda i: (jnp.maximum(i * t8 - 1, 0), 0)),
                  pl.BlockSpec((8, zr_cols), lambda i: (jnp.minimum((i + 1) * t8, nblk8 - 1), 0)),
                  full((1, zr_cols)), full((1, zr_cols)),
                  full((2, dr)), full((2, wl, dr)), full((2, dr)), full((2, al, dr)),
                  full((gl, dr)), full((1, dr)), full((1, dr)), full((1, dr))],
        out_specs=[pl.BlockSpec((tm, dr), lambda i: (i, 0))] * 10,
        compiler_params=_cparams("arbitrary"), name="rwkv_prep",
    )(z_rest, z_rest, z_rest, p["mu_prev"], p["mu_next"], p["w0"], p["w2"], p["a0"], p["a2"],
      p["g2"], p["k_k"], p["k_a"], p["r_k"])


def _stack_heads(x, lo):
    return jnp.concatenate([jnp.where(lo, x, 0.0), jnp.where(lo, 0.0, x)], axis=0)


SUB_BLOCK = 16
assert CHUNK // SUB_BLOCK == 4
assert CHUNK == RWKV_HEAD


def _rwkv_masks(rev):
    c = CHUNK
    n2 = 2 * c
    ti = lax.broadcasted_iota(jnp.int32, (c, c), 0)
    tj = lax.broadcasted_iota(jnp.int32, (c, c), 1)
    cum = jnp.where((tj >= ti) if rev else (tj <= ti), 1.0, 0.0).astype(BF16)
    cum = jnp.concatenate([cum, cum, cum], axis=1)
    row = lax.broadcasted_iota(jnp.int32, (c, LANES), 0)
    lane = lax.broadcasted_iota(jnp.int32, (c, LANES), 1)
    col = lane & (c - 1)
    lo = lane < RWKV_HEAD
    strict = (col > row) if rev else (col < row)
    incl = (col >= row) if rev else (col <= row)
    eye = jnp.where(col == row, 1.0, 0.0)
    sub = (row >> _log2(SUB_BLOCK)) == (col >> _log2(SUB_BLOCK))
    ii = lax.broadcasted_iota(jnp.int32, (n2, n2), 0)
    jj = lax.broadcasted_iota(jnp.int32, (n2, n2), 1)
    same_head = (ii >> _log2(c)) == (jj >> _log2(c))
    return cum, lo, strict, incl, eye, sub, same_head


def _rwkv_chunk(r, k, v, kk, a, lw, k_a, s, rev, masks):
    c = CHUNK
    n2 = 2 * c
    cum, lo, strict, incl, eye, sub, same_head = masks
    bc = _exact_rowmix(cum, lw)
    yield
    bex = bc - lw
    btot = bc[0:1, :] if rev else bc[c - 1:c, :]
    kd = k * (1.0 + (a - 1.0) * k_a)
    bv = kk * a
    e_neg = jnp.exp(-bc)
    e_tot = jnp.exp(btot - bc)
    stack = lambda x: _stack_heads(x.astype(BF16), lo)
    at = (-kk * jnp.exp(bex)).astype(BF16)
    rt_f = r * jnp.exp(bc)
    kc = (kd * e_tot).astype(BF16)
    bcc = (bv * e_tot).astype(BF16)
    s_v = stack(v)

    gram = _dot_nt(jnp.concatenate([at, rt_f.astype(BF16)], axis=0),
                   jnp.concatenate([stack(bv * e_neg), stack(kd * e_neg)], axis=0))
    yield
    nmat = jnp.where(strict, gram[:c, :n2], 0.0)
    a_ka = jnp.where(strict, gram[:c, n2:], 0.0)
    a_rb = jnp.where(incl, gram[c:, :n2], 0.0)
    a_rk = jnp.where(incl, gram[c:, n2:], 0.0)
    n_d = jnp.where(sub, nmat, 0.0)
    t_d = eye + n_d
    pw = n_d
    s_pw = stack(pw)
    for _ in range(_log2(SUB_BLOCK) - 1):
        pw = _dot(pw, s_pw)
        yield
        s_pw = stack(pw)
        t_d = t_d + _dot(t_d, s_pw)
    x1 = _dot(a_ka, s_v)
    yield
    xm = _dot(t_d, stack(jnp.where(sub, 0.0, nmat)))
    yield
    xm2 = _dot(xm, stack(xm))
    yield
    ym = eye + xm + xm2 + _dot(xm, stack(xm2))
    yield
    tm_ = _dot(ym, stack(t_d))
    yield
    aw = _dot(tm_, jnp.concatenate([stack(at), stack(x1)], axis=1))
    yield
    ah, w = aw[:, :n2], aw[:, n2:]
    rhs = jnp.concatenate(
        [jnp.concatenate([s_v, jnp.zeros_like(s_v)], axis=1),
         jnp.concatenate([stack(w), stack(ah)], axis=1)], axis=0)
    qr = _dot(jnp.concatenate([a_rk, a_rb], axis=1), rhs)
    q = qr[:, :n2]
    rr = rt_f + qr[:, n2:]
    mm = jnp.where(same_head, _dot_tn(ah, bcc), 0.0)
    nn = jnp.where(same_head, _dot_tn(jnp.concatenate([v, w], axis=0),
                                      jnp.concatenate([kc, bcc], axis=0)), 0.0)
    yield
    o = _dot_nt(rr, s) + q
    s_new = s * jnp.exp(btot) + _dot(s, mm) + nn
    return o, s_new


def _rwkv_scan_kernel(*refs, has_init, emit_final, nchunk, npairs, aliased):
    it = iter(refs)
    fwd = [next(it) for _ in range(6)]
    bwd = [next(it) for _ in range(6)]
    ka_ref = next(it)
    s0_ref = next(it) if has_init else None
    if aliased:
        next(it), next(it)
    of_ref = next(it)
    ob_ref = next(it)
    sfin_ref = next(it) if emit_final else None
    s_scr = next(it)
    ci = pl.program_id(2)

    @pl.when(ci == 0)
    def _():
        if has_init:
            s_scr[...] = s0_ref[...]
        else:
            s_scr[...] = jnp.zeros_like(s_scr)

    chains, sinks = [], []
    for d, (ins, o_ref) in enumerate(((fwd, of_ref), (bwd, ob_ref))):
        masks = _rwkv_masks(rev=(d == 1))
        for hp in range(npairs):
            sl = slice(hp * LANES, (hp + 1) * LANES)
            r, k, v, kk, a, lw = (x[:, sl] for x in ins)
            chains.append(_rwkv_chunk(r, k, v, kk, a, lw, ka_ref[:, sl], s_scr[d, hp],
                                      rev=(d == 1), masks=masks))
            sinks.append((o_ref, sl, d, hp))
    for (o, s_new), (o_ref, sl, d, hp) in zip(_lockstep(chains), sinks):
        o_ref[:, sl] = o
        s_scr[d, hp] = s_new

    if emit_final:
        @pl.when(ci == nchunk - 1)
        def _():
            sfin_ref[...] = s_scr[...]


def _rwkv_scan(pre, k_a, s0_bd, *, row_off, nb, t, emit_final, npairs, into=None):
    m, dr = pre["r"].shape
    ngroup = dr // (LANES * npairs)
    nchunk = t // CHUNK
    rb = row_off // CHUNK
    has_init = s0_bd is not None
    fmap = lambda b, h, c: (rb + b * nchunk + c, h)
    bmap = lambda b, h, c: (rb + b * nchunk + (nchunk - 1 - c), h)
    blk = (CHUNK, LANES * npairs)
    in_specs = ([pl.BlockSpec(blk, fmap)] * 6 + [pl.BlockSpec(blk, bmap)] * 6
                + [pl.BlockSpec((1, LANES * npairs), lambda b, h, c: (0, h))])
    args = [pre["r"], pre["k"], pre["v"], pre["kk"], pre["a_f"], pre["lw_f"],
            pre["r"], pre["k"], pre["v"], pre["kk"], pre["a_b"], pre["lw_b"], k_a]
    sblk = pl.BlockSpec((None, 2, npairs, LANES, LANES), lambda b, h, c: (b, 0, h, 0, 0))
    if has_init:
        in_specs.append(sblk)
        args.append(s0_bd)
    aliases = {}
    if into is not None:
        aliases = {len(args): 0, len(args) + 1: 1}
        in_specs += [pl.BlockSpec(memory_space=pl.ANY)] * 2
        args += list(into)
    out_shape = [jax.ShapeDtypeStruct((m, dr), F32)] * 2
    out_specs = [pl.BlockSpec(blk, fmap), pl.BlockSpec(blk, bmap)]
    if emit_final:
        out_shape.append(jax.ShapeDtypeStruct((nb, 2, dr // LANES, LANES, LANES), F32))
        out_specs.append(sblk)
    kern = functools.partial(_rwkv_scan_kernel, has_init=has_init, emit_final=emit_final,
                             nchunk=nchunk, npairs=npairs, aliased=into is not None)
    return pl.pallas_call(
        kern, out_shape=out_shape, grid=(nb, ngroup, nchunk), in_specs=in_specs,
        out_specs=out_specs, scratch_shapes=[pltpu.VMEM((2, npairs, LANES, LANES), F32)],
        input_output_aliases=aliases,
        compiler_params=_cparams("arbitrary", "arbitrary", "arbitrary"), name="rwkv_scan",
    )(*args)


def _rwkv_post_kernel(of_ref, ob_ref, bonus_ref, g_ref, lnw_ref, lnb_ref, y_ref, *, dr):
    bd = _head_ones(LANES, RWKV_HEAD)
    inv = 1.0 / RWKV_HEAD
    for j in range(dr // LANES):
        sl = slice(j * LANES, (j + 1) * LANES)
        o = of_ref[:, sl] + ob_ref[:, sl]
        mean = _group_sum(o, bd) * inv
        oc = o - mean
        var = _group_sum(oc * oc, bd) * inv
        on = oc * lax.rsqrt(var + GN_EPS) * lnw_ref[:, sl] + lnb_ref[:, sl]
        y_ref[:, sl] = ((on + bonus_ref[:, sl]) * g_ref[:, sl]).astype(y_ref.dtype)


def _rwkv_post(o_f, o_b, pre, p, *, tm):
    rows, dr = o_f.shape
    own = pl.BlockSpec((tm, dr), lambda i: (i, 0))
    par = pl.BlockSpec((1, dr), lambda i: (0, 0))
    return pl.pallas_call(
        functools.partial(_rwkv_post_kernel, dr=dr),
        out_shape=jax.ShapeDtypeStruct((rows, dr), BF16),
        grid=(rows // tm,),
        in_specs=[own] * 4 + [par] * 2,
        out_specs=own,
        compiler_params=_cparams("arbitrary"), name="rwkv_post",
    )(o_f, o_b, pre["bonus"], pre["g"], p["ln_w"], p["ln_b"])


HGRN_LEVELS = 6


SUBLANES = 8


def _boundary_rows(b, lvl, rev, sub):
    c = CHUNK
    half = 1 << lvl
    blk = 2 * half
    src = (lambda base: base + half) if rev else (lambda base: base + half - 1)

    def rows(starts, nrows):
        return jnp.concatenate(
            [jnp.broadcast_to(b[src(s):src(s) + 1, :], (nrows, LANES)) for s in starts], axis=0)

    if blk >= SUBLANES:
        return rows(range(0, c, blk), blk)
    out = None
    for k in range(SUBLANES // blk):
        arr = rows(range(k * blk, c, SUBLANES), SUBLANES)
        out = arr if out is None else jnp.where(sub >= k * blk, arr, out)
    return out


def _hgrn_masks(rev):
    c = CHUNK
    ti = lax.broadcasted_iota(jnp.int32, (c, c), 0)
    tj = lax.broadcasted_iota(jnp.int32, (c, c), 1)
    late, early = (tj, ti) if rev else (ti, tj)
    sels = [((late >> lvl) == (early >> lvl) + 1) & (((early >> lvl) & 1) == 0)
            for lvl in range(HGRN_LEVELS)]
    cum = jnp.where((tj >= ti) if rev else (tj <= ti), 1.0, 0.0).astype(BF16)
    sub = lax.broadcasted_iota(jnp.int32, (c, LANES), 0) & (SUBLANES - 1)
    return jnp.concatenate([cum, cum, cum], axis=1), ti == tj, sels, sub


def _hgrn_chunk(q_pre, f_pre, v, lb, st, rev, masks):
    c = CHUNK
    cum3, diag, sels, sub = masks
    q = q_pre * jax.nn.sigmoid(q_pre)
    f = lb + (1.0 - lb) * jax.nn.sigmoid(f_pre)
    kf = 1.0 - f
    b = _exact_rowmix(cum3, jnp.log(f))
    yield
    btot = b[0:1, :] if rev else b[c - 1:c, :]
    q_b, kf_b = q.astype(BF16), kf.astype(BF16)
    attn = jnp.where(diag, _dot_nt(q_b, kf_b), 0.0)
    for lvl in range(HGRN_LEVELS):
        dq = b - _boundary_rows(b, lvl, rev, sub)
        wgt = jnp.exp(-jnp.abs(dq)).astype(BF16)
        attn = jnp.where(sels[lvl], _dot_nt(q_b * wgt, kf_b * wgt), attn)
    yield
    o = _dot_nt(q * jnp.exp(b), st) + _dot(attn, v)
    st_new = st * jnp.exp(btot) + _dot_tn(v, kf * jnp.exp(btot - b))
    return o, st_new


def _hgrn_scan_kernel(*refs, has_init, emit_final, nsteps, nheads, cols, aliased):
    it = iter(refs)
    qf_ref, ff_ref, vf_ref = next(it), next(it), next(it)
    qb_ref, fb_ref, vb_ref = next(it), next(it), next(it)
    lb_ref = next(it)
    s0_ref = next(it) if has_init else None
    if aliased:
        next(it), next(it)
    of_ref, ob_ref = next(it), next(it)
    sfin_ref = next(it) if emit_final else None
    s_scr = next(it)
    in_scr, out_scr = (next(it), next(it)) if cols is not None else (None, None)
    ci = pl.program_id(2)

    @pl.when(ci == 0)
    def _():
        if has_init:
            for d in range(2):
                for h in range(nheads):
                    s_scr[d, h] = s0_ref[d, h].T
        else:
            s_scr[...] = jnp.zeros_like(s_scr)

    ins = ((qf_ref, ff_ref, vf_ref), (qb_ref, fb_ref, vb_ref))
    outs = (of_ref, ob_ref)
    all_masks = [_hgrn_masks(rev=False), _hgrn_masks(rev=True)]

    def advance(load, store):
        chains, sinks = [], []
        for d in range(2):
            for h in range(nheads):
                sl = slice(h * LANES, (h + 1) * LANES)
                qp, fp, v = (load(d, k, sl) for k in range(3))
                chains.append(_hgrn_chunk(qp, fp, v, lb_ref[:, sl], s_scr[d, h], rev=(d == 1),
                                          masks=all_masks[d]))
                sinks.append((d, h, sl))
        for (o, s_new), (d, h, sl) in zip(_lockstep(chains), sinks):
            store(d, sl, o)
            s_scr[d, h] = s_new

    if cols is None:
        def store_rows(d, sl, o):
            outs[d][:, sl] = o
        advance(lambda d, k, sl: ins[d][k][:, sl], store_rows)
    else:
        for d in range(2):
            for k in range(3):
                in_scr[d, k] = jnp.swapaxes(ins[d][k][...], 0, 1)

        def body(j, carry):
            col = (j, cols - 1 - j)

            def store_col(d, sl, o):
                out_scr[d, col[d], :, sl] = o
            advance(lambda d, k, sl: in_scr[d, k, col[d], :, sl], store_col)
            return carry
        lax.fori_loop(0, cols, body, 0)
        for d in range(2):
            outs[d][...] = jnp.swapaxes(out_scr[d], 0, 1)

    if emit_final:
        @pl.when(ci == nsteps - 1)
        def _():
            for d in range(2):
                for h in range(nheads):
                    sfin_ref[d, h] = s_scr[d, h].T


HGRN_COLS = 8


def _hgrn_scan(zhs, lb, s0, *, row_off, nb, t, dg, emit_final, nheads, column_major,
               into=None):
    m = zhs.shape[0]
    ng = dg // (LANES * nheads)
    has_init = s0 is not None
    lanes = LANES * nheads
    if column_major:
        nrow = t // GRID_W
        assert nrow == CHUNK and m % t == 0 and row_off % t == 0
        nsteps = GRID_W // HGRN_COLS
        seq0 = row_off // t
        zsrc = zhs.reshape(m // t, nrow, GRID_W, 4 * dg)
        blk = (None, nrow, HGRN_COLS, lanes)
        fwd = lambda g: (lambda b, h, c: (b + seq0, 0, c, g * ng + h))
        bwd = lambda g: (lambda b, h, c: (b + seq0, 0, nsteps - 1 - c, g * ng + h))
        out_dims = (m // t, nrow, GRID_W, dg)
        ofwd = lambda b, h, c: (b + seq0, 0, c, h)
        obwd = lambda b, h, c: (b + seq0, 0, nsteps - 1 - c, h)
        cols = HGRN_COLS
    else:
        nsteps = t // CHUNK
        rb = row_off // CHUNK
        zsrc = zhs
        blk = (CHUNK, lanes)
        fwd = lambda g: (lambda b, h, c: (rb + b * nsteps + c, g * ng + h))
        bwd = lambda g: (lambda b, h, c: (rb + b * nsteps + nsteps - 1 - c, g * ng + h))
        out_dims = (m, dg)
        ofwd = lambda b, h, c: (rb + b * nsteps + c, h)
        obwd = lambda b, h, c: (rb + b * nsteps + nsteps - 1 - c, h)
        cols = None

    in_specs = [pl.BlockSpec(blk, fwd(0)), pl.BlockSpec(blk, fwd(1)), pl.BlockSpec(blk, fwd(3)),
                pl.BlockSpec(blk, bwd(0)), pl.BlockSpec(blk, bwd(2)), pl.BlockSpec(blk, bwd(3)),
                pl.BlockSpec((1, lanes), lambda b, h, c: (0, h))]
    args = [zsrc] * 6 + [lb]
    sblk = pl.BlockSpec((None, 2, nheads, LANES, LANES), lambda b, h, c: (b, 0, h, 0, 0))
    if has_init:
        in_specs.append(sblk)
        args.append(s0)
    aliases = {}
    if into is not None:
        aliases = {len(args): 0, len(args) + 1: 1}
        in_specs += [pl.BlockSpec(memory_space=pl.ANY)] * 2
        args += [o.reshape(out_dims) for o in into]
    out_shape = [jax.ShapeDtypeStruct(out_dims, F32)] * 2
    out_specs = [pl.BlockSpec(blk, ofwd), pl.BlockSpec(blk, obwd)]
    if emit_final:
        out_shape.append(jax.ShapeDtypeStruct((nb, 2, dg // LANES, LANES, LANES), F32))
        out_specs.append(sblk)
    kern = functools.partial(_hgrn_scan_kernel, has_init=has_init, emit_final=emit_final,
                             nsteps=nsteps, nheads=nheads, cols=cols, aliased=into is not None)
    scratch = [pltpu.VMEM((2, nheads, LANES, LANES), F32)]
    if column_major:
        scratch += [pltpu.VMEM((2, 3, cols, CHUNK, lanes), F32),
                    pltpu.VMEM((2, cols, CHUNK, lanes), F32)]
    outs = pl.pallas_call(
        kern, out_shape=out_shape, grid=(nb, ng, nsteps), in_specs=in_specs, out_specs=out_specs,
        scratch_shapes=scratch, input_output_aliases=aliases,
        compiler_params=_cparams("arbitrary", "arbitrary", "arbitrary"), name="hgrn_scan",
    )(*args)
    return [outs[0].reshape(m, dg), outs[1].reshape(m, dg)] + list(outs[2:])


def _hgrn_post_kernel(of_ref, ob_ref, gate_ref, gain_ref, y_ref, *, dg):
    gain = gain_ref[...]
    for j in range(dg // LANES):
        sl = slice(j * LANES, (j + 1) * LANES)
        o = of_ref[:, sl] + ob_ref[:, sl]
        on = o * lax.rsqrt(jnp.mean(o * o, axis=-1, keepdims=True) + RMS_EPS) * gain
        gt = gate_ref[:, sl]
        y_ref[:, sl] = (on * (gt * jax.nn.sigmoid(gt))).astype(y_ref.dtype)


def _hgrn_post(o_f, o_b, z_gates, gain, *, gate_col):
    rows_all, dg = o_f.shape
    kern = functools.partial(_hgrn_post_kernel, dg=dg)
    gcb = gate_col // dg
    par = pl.BlockSpec((1, LANES), lambda *_: (0, 0))
    tm = min(256, rows_all)
    ospec = pl.BlockSpec((tm, dg), lambda i: (i, 0))
    return pl.pallas_call(
        kern, out_shape=jax.ShapeDtypeStruct((rows_all, dg), BF16), grid=(rows_all // tm,),
        in_specs=[ospec, ospec, pl.BlockSpec((tm, dg), lambda i: (i, gcb)), par],
        out_specs=ospec,
        compiler_params=_cparams("arbitrary"), name="hgrn_post",
    )(o_f, o_b, z_gates, gain)


def _largest_divisor(n, candidates):
    for cand in candidates:
        if n % cand == 0:
            return cand
    raise ValueError(f"no tile in {candidates} divides {n}")


def _pad_to(x, axis, size):
    pad = size - x.shape[axis]
    if pad == 0:
        return x
    widths = [(0, 0)] * x.ndim
    widths[axis] = (0, pad)
    return jnp.pad(x, widths)


def _pair_state_to_bd(s):
    b, two, h, n, _ = s.shape
    s = s.reshape(b, two, h // 2, 2, n, n)
    z = jnp.zeros_like(s[:, :, :, 0])
    top = jnp.concatenate([s[:, :, :, 0], z], axis=-1)
    bot = jnp.concatenate([z, s[:, :, :, 1]], axis=-1)
    return jnp.concatenate([top, bot], axis=-2)


def _bd_to_pair_state(sbd):
    b, two, hp, n2, _ = sbd.shape
    n = n2 // 2
    s0 = sbd[:, :, :, :n, :n]
    s1 = sbd[:, :, :, n:, n:]
    return jnp.stack([s0, s1], axis=3).reshape(b, two, hp * 2, n, n)


def kernel(x_prompt, x_sample, state_rwkv, state_hgrn, c, c_ctx, w_mod, b_mod, norm_ffn1, norm_mix, norm_ffn2, ffn1_w_in, ffn1_w_out, ffn2_w_in, ffn2_w_out, w_in, rwkv_mu_prev, rwkv_mu_next, rwkv_w0, rwkv_w2, rwkv_a0, rwkv_a2, rwkv_g2, rwkv_k_k, rwkv_k_a, rwkv_r_k, rwkv_ln_w, rwkv_ln_b, hgrn_lb_logits, hgrn_norm, w_branch_rwkv, w_branch_hgrn, w_out, norm_final):
    nbc, tc, d = x_prompt.shape
    nbl, tl, _ = x_sample.shape
    depth = w_mod.shape[0]
    mc, ml = nbc * tc, nbl * tl
    m = mc + ml
    dims = (mc, tc, tl, nbl)
    dr = rwkv_w0.shape[-1]
    dg = w_branch_hgrn.shape[1]
    dff = ffn1_w_out.shape[1]
    wl, al, gl = rwkv_w2.shape[2], rwkv_a2.shape[2], rwkv_g2.shape[1]
    glp = -(-gl // LANES) * LANES
    rw_cols = 3 * dr + 2 * wl + 2 * al + gl
    zr_cols = 3 * dr + 2 * wl + 2 * al + glp
    assert nbl < MOD_ROWS and tl % GRID_W == 0 and tl // GRID_W == CHUNK and tc % CHUNK == 0

    tm_norm = min(256, tc)
    tm_proj = min(1024, mc)
    tm_merge = min(512, mc)
    tn_of = lambda n: _largest_divisor(n, (1024, 512, 256, 128))
    tf = _largest_divisor(dff, (256, 128))

    lb_all = jnp.cumsum(jax.nn.softmax(hgrn_lb_logits.astype(F32), axis=0), axis=0)
    cond = jnp.concatenate([c, c_ctx[None, :], jnp.zeros((MOD_ROWS - nbl - 1, d), F32)], axis=0)

    x = (x_prompt.reshape(mc, d), x_sample.reshape(ml, d))
    new_r, new_h = [], []
    for l in range(depth):
        mod3 = _modulation(cond, w_mod[l], b_mod[l], tn_of(N_MOD * d)).reshape(MOD_ROWS, N_MOD, d)

        h = _resnorm(x, None, mod3, norm_ffn1[l], dims, res_scale=None, gate_idx=None, mod_idx=0,
                     emit_x=False, out_dtype=BF16, tm=tm_norm)[0]
        delta = _ffn(h, ffn1_w_in[l].astype(BF16), ffn1_w_out[l].astype(BF16), tm=tm_proj, tf=tf)

        x, h = _resnorm(x, delta, mod3, norm_mix[l], dims, res_scale=FFN_HALF, gate_idx=2,
                        mod_idx=3, emit_x=True, out_dtype=BF16, tm=tm_norm)
        wi = w_in[l]
        hg0 = rw_cols
        w_r = _pad_to(wi[:, :rw_cols], 1, zr_cols).astype(BF16)
        w_hs = wi[:, hg0:hg0 + 4 * dg].astype(BF16)
        w_gates = wi[:, hg0 + 4 * dg:].astype(BF16)
        col_gate, col_gr, col_gh = 0, dg, dg + d
        z_r = _matmul(h, w_r, out_dtype=F32, tm=tm_proj, tn=tn_of(zr_cols), name="proj_rwkv")
        z_g = _matmul(h, w_gates, out_dtype=F32, tm=tm_proj, tn=tn_of(dg), name="proj_gates")
        zhs = _matmul(h, w_hs, out_dtype=F32, tm=tm_proj, tn=tn_of(dg), name="proj_hgrn")

        p = {
            "mu_prev": _pad_to(rwkv_mu_prev[l][None, :], 1, zr_cols),
            "mu_next": _pad_to(rwkv_mu_next[l][None, :], 1, zr_cols),
            "w0": rwkv_w0[l], "w2": rwkv_w2[l].astype(BF16),
            "a0": rwkv_a0[l], "a2": rwkv_a2[l].astype(BF16),
            "g2": _pad_to(rwkv_g2[l], 0, glp).astype(BF16),
            "k_k": rwkv_k_k[l][None, :], "k_a": rwkv_k_a[l][None, :],
            "r_k": rwkv_r_k[l].reshape(1, dr),
            "ln_w": rwkv_ln_w[l][None, :], "ln_b": rwkv_ln_b[l][None, :],
        }
        names = ("r", "k", "v", "kk", "a_f", "a_b", "lw_f", "lw_b", "g", "bonus")
        pre = dict(zip(names, _rwkv_prep(z_r, p, dims, tm=min(128, tc), zr_cols=zr_cols, dr=dr,
                                         wl=wl, al=al, gl=glp)))
        npairs = _largest_divisor(dr // LANES, (16, 8, 4, 2, 1))
        of_c, ob_c, sr_c = _rwkv_scan(pre, p["k_a"], None, row_off=0, nb=nbc, t=tc,
                                      emit_final=True, npairs=npairs)
        o_f, o_b = _rwkv_scan(pre, p["k_a"], _pair_state_to_bd(state_rwkv[:, l].astype(F32)),
                              row_off=mc, nb=nbl, t=tl, emit_final=False, npairs=npairs,
                              into=(of_c, ob_c))
        y_r = _rwkv_post(o_f, o_b, pre, p, tm=min(256, tc))
        new_r.append(_bd_to_pair_state(sr_c))

        lb = lb_all[l][None, :]
        gain_h = hgrn_norm[l][None, :]
        nheads = _largest_divisor(dg // LANES, (8, 4, 2, 1))
        nheads_cols = _largest_divisor(dg // LANES, (4, 2, 1))
        hf_c, hb_c, sh_c = _hgrn_scan(zhs, lb, None, row_off=0, nb=nbc, t=tc, dg=dg,
                                      emit_final=True, nheads=nheads, column_major=False)
        h_f, h_b = _hgrn_scan(zhs, lb, state_hgrn[:, l].astype(F32), row_off=mc, nb=nbl, t=tl,
                              dg=dg, emit_final=False, nheads=nheads_cols, column_major=True,
                              into=(hf_c, hb_c))
        y_h = _hgrn_post(h_f, h_b, z_g, gain_h, gate_col=col_gate)
        new_h.append(sh_c)

        merged = _merge(y_r, y_h, w_branch_rwkv[l].astype(BF16), w_branch_hgrn[l].astype(BF16),
                        z_g, col_gr, col_gh, tm=tm_merge, tn=tn_of(dg))
        delta = _matmul(merged, w_out[l].astype(BF16), out_dtype=F32, tm=tm_proj, tn=tn_of(d),
                        name="mix_out")

        x, h = _resnorm(x, delta, mod3, norm_ffn2[l], dims, res_scale=1.0, gate_idx=5, mod_idx=6,
                        emit_x=True, out_dtype=BF16, tm=tm_norm)
        delta = _ffn(h, ffn2_w_in[l].astype(BF16), ffn2_w_out[l].astype(BF16), tm=tm_proj, tf=tf)
        last = l == depth - 1
        if not last:
            x = _resnorm(x, delta, mod3, norm_ffn2[l], dims, res_scale=FFN_HALF, gate_idx=8,
                         mod_idx=None, emit_x=True, out_dtype=F32, tm=tm_norm)[0]

    y_c = _resnorm(x, delta, mod3, norm_final, dims, res_scale=FFN_HALF, gate_idx=8, mod_idx=None,
                   emit_x=False, out_dtype=F32, tm=tm_norm, row_off=0, rows=mc)[0]
    y_l = _resnorm(x, delta, mod3, norm_final, dims, res_scale=FFN_HALF, gate_idx=8, mod_idx=None,
                   emit_x=False, out_dtype=F32, tm=tm_norm, row_off=mc, rows=ml)[0]
    return (y_c.reshape(nbc, tc, d), y_l.reshape(nbl, tl, d),
            jnp.stack(new_r, axis=1), jnp.stack(new_h, axis=1))
```

```python
import functools
import math

import jax
import jax.numpy as jnp
from jax import lax
from jax.experimental import pallas as pl
from jax.experimental.pallas import tpu as pltpu

F32 = jnp.float32
BF16 = jnp.bfloat16

CHUNK = 64
GRID_W = 64
RWKV_HEAD = 64
HGRN_EXPAND = 128
N_MOD = 9
RMS_EPS = 1e-6
GN_EPS = 64e-5
FFN_HALF = 0.5
DECAY_SCALE = math.exp(-0.5)
LANES = 128
MOD_ROWS = 16
VMEM_LIMIT_BYTES = 56 * 1024 * 1024


def _cparams(*sem):
    return pltpu.CompilerParams(dimension_semantics=sem, vmem_limit_bytes=VMEM_LIMIT_BYTES)


def _dot(a, b):
    return jnp.dot(a.astype(BF16), b.astype(BF16), preferred_element_type=F32)


def _dot_nt(a, b):
    return lax.dot_general(a.astype(BF16), b.astype(BF16), (((1,), (1,)), ((), ())),
                           preferred_element_type=F32)


def _dot_tn(a, b):
    return jnp.dot(a.T.astype(BF16), b.astype(BF16), preferred_element_type=F32)


def _split2(x):
    hi = x.astype(BF16)
    return hi, (x - hi.astype(F32)).astype(BF16)


def _split3(x):
    x1 = x.astype(BF16)
    d1 = x - x1.astype(F32)
    x2 = d1.astype(BF16)
    x3 = (d1 - x2.astype(F32)).astype(BF16)
    return x1, x2, x3


def _exact_rowmix(mat3, x):
    return jnp.dot(mat3, jnp.concatenate(_split3(x), axis=0), preferred_element_type=F32)


def _group_sum(x, bd_ones):
    x1, x2 = _split2(x)
    return (jnp.dot(x1, bd_ones, preferred_element_type=F32)
            + jnp.dot(x2, bd_ones, preferred_element_type=F32))


def _lockstep(chains):
    results = [None] * len(chains)
    active = list(enumerate(chains))
    while active:
        still = []
        for idx, chain in active:
            try:
                next(chain)
                still.append((idx, chain))
            except StopIteration as done:
                results[idx] = done.value
        active = still
    return results


def _mod_kernel(c_ref, w_ref, b_ref, o_ref):
    c = c_ref[...]
    s = c * jax.nn.sigmoid(c)
    o_ref[...] = _dot(s, w_ref[...]) + b_ref[...]


def _modulation(cond, w_mod, b_mod, tn):
    d, n = w_mod.shape
    return pl.pallas_call(
        _mod_kernel,
        out_shape=jax.ShapeDtypeStruct((MOD_ROWS, n), F32),
        grid=(n // tn,),
        in_specs=[pl.BlockSpec((MOD_ROWS, d), lambda j: (0, 0)),
                  pl.BlockSpec((d, tn), lambda j: (0, j)),
                  pl.BlockSpec((1, tn), lambda j: (0, j))],
        out_specs=pl.BlockSpec((MOD_ROWS, tn), lambda j: (0, j)),
        compiler_params=_cparams("arbitrary"),
        name="modulation",
    )(cond, w_mod, b_mod.reshape(1, n))


def _resnorm_kernel(*refs, has_delta, res_scale, gate_idx, mod_idx, emit_x, ctx_tiles):
    it = iter(refs)
    x_ref = next(it)
    xl_ref = next(it) if ctx_tiles is not None else None
    delta_ref = next(it) if has_delta else None
    mod_ref = next(it) if (has_delta or mod_idx is not None) else None
    gain_ref = next(it)
    xo_ref = next(it) if emit_x else None
    h_ref = next(it)
    x = x_ref[...]
    if ctx_tiles is not None:
        x = jnp.where(pl.program_id(0) < ctx_tiles, x, xl_ref[...])
    if has_delta:
        g = mod_ref[gate_idx:gate_idx + 1, :]
        x = x + g * (res_scale * delta_ref[...])
    if emit_x:
        xo_ref[...] = x
    y = x * lax.rsqrt(jnp.mean(x * x, axis=-1, keepdims=True) + RMS_EPS)
    y = y * gain_ref[...]
    if mod_idx is not None:
        sh = mod_ref[mod_idx:mod_idx + 1, :]
        sc = mod_ref[mod_idx + 1:mod_idx + 2, :]
        y = y * (1.0 + sc) + sh
    h_ref[...] = y.astype(h_ref.dtype)


def _resnorm(x, delta, mod3, gain, dims, *, res_scale, gate_idx, mod_idx, emit_x, out_dtype,
             tm, row_off=0, rows=None):
    mc, tc, tl, nlb = dims
    n_ctx_tiles = mc // tm
    tiles_per_lat = tl // tm
    ro = row_off // tm
    split_x = isinstance(x, tuple)
    if split_x:
        assert row_off == 0 and rows is None
        d = x[0].shape[1]
        m_all = x[0].shape[0] + x[1].shape[0]
        in_specs = [pl.BlockSpec((tm, d), lambda i: (jnp.minimum(i, n_ctx_tiles - 1), 0)),
                    pl.BlockSpec((tm, d), lambda i: (jnp.maximum(i - n_ctx_tiles, 0), 0))]
        args = list(x)
    else:
        d = x.shape[1]
        m_all = rows if rows is not None else x.shape[0]
        in_specs = [pl.BlockSpec((tm, d), lambda i: (i + ro, 0))]
        args = [x]

    def mod_row(i):
        gi = i + ro
        return jnp.where(gi < n_ctx_tiles, nlb, (gi - n_ctx_tiles) // tiles_per_lat)

    has_delta = delta is not None
    if has_delta:
        in_specs.append(pl.BlockSpec((tm, d), lambda i: (i + ro, 0)))
        args.append(delta)
    if has_delta or mod_idx is not None:
        in_specs.append(pl.BlockSpec((None, N_MOD, d), lambda i: (mod_row(i), 0, 0)))
        args.append(mod3)
    in_specs.append(pl.BlockSpec((1, d), lambda i: (0, 0)))
    args.append(gain.reshape(1, d))
    out_shape, out_specs = [], []
    if emit_x:
        out_shape.append(jax.ShapeDtypeStruct((m_all, d), F32))
        out_specs.append(pl.BlockSpec((tm, d), lambda i: (i, 0)))
    out_shape.append(jax.ShapeDtypeStruct((m_all, d), out_dtype))
    out_specs.append(pl.BlockSpec((tm, d), lambda i: (i, 0)))
    kern = functools.partial(_resnorm_kernel, has_delta=has_delta, res_scale=res_scale,
                             gate_idx=gate_idx, mod_idx=mod_idx, emit_x=emit_x,
                             ctx_tiles=n_ctx_tiles if split_x else None)
    return pl.pallas_call(
        kern, out_shape=out_shape, grid=(m_all // tm,), in_specs=in_specs, out_specs=out_specs,
        compiler_params=_cparams("arbitrary"), name="resnorm",
    )(*args)


def _mm_kernel(a_ref, w_ref, o_ref):
    o_ref[...] = jnp.dot(a_ref[...], w_ref[...], preferred_element_type=F32).astype(o_ref.dtype)


def _matmul(a, w, *, out_dtype, tm, tn, row_off=0, rows=None, name="matmul"):
    k = a.shape[1]
    n = w.shape[1]
    rows = a.shape[0] if rows is None else rows
    ro = row_off // tm
    return pl.pallas_call(
        _mm_kernel,
        out_shape=jax.ShapeDtypeStruct((rows, n), out_dtype),
        grid=(rows // tm, n // tn),
        in_specs=[pl.BlockSpec((tm, k), lambda i, j: (i + ro, 0)),
                  pl.BlockSpec((k, tn), lambda i, j: (0, j))],
        out_specs=pl.BlockSpec((tm, tn), lambda i, j: (i, j)),
        compiler_params=_cparams("arbitrary", "arbitrary"), name=name,
    )(a, w)


def _ffn_kernel(h_ref, wa_ref, wb_ref, wo_ref, o_ref):
    @pl.when(pl.program_id(1) == 0)
    def _():
        o_ref[...] = jnp.zeros_like(o_ref)

    h = h_ref[...]
    a = jnp.dot(h, wa_ref[...], preferred_element_type=F32)
    b = jnp.dot(h, wb_ref[...], preferred_element_type=F32)
    u = (a * jax.nn.sigmoid(a) * b).astype(BF16)
    o_ref[...] += jnp.dot(u, wo_ref[...], preferred_element_type=F32)


def _ffn(h, w_in, w_out, *, tm, tf):
    m, d = h.shape
    dff = w_out.shape[0]
    nf = dff // tf
    return pl.pallas_call(
        _ffn_kernel,
        out_shape=jax.ShapeDtypeStruct((m, d), F32),
        grid=(m // tm, nf),
        in_specs=[pl.BlockSpec((tm, d), lambda i, j: (i, 0)),
                  pl.BlockSpec((d, tf), lambda i, j: (0, j)),
                  pl.BlockSpec((d, tf), lambda i, j: (0, j + nf)),
                  pl.BlockSpec((tf, d), lambda i, j: (j, 0))],
        out_specs=pl.BlockSpec((tm, d), lambda i, j: (i, 0), pipeline_mode=pl.Buffered(1)),
        compiler_params=_cparams("arbitrary", "arbitrary"), name="ffn",
    )(h, w_in, w_in, w_out)


def _merge_kernel(yr_ref, yh_ref, wr_ref, wh_ref, gr_ref, gh_ref, o_ref):
    pr = jnp.dot(yr_ref[...], wr_ref[...], preferred_element_type=F32)
    ph = jnp.dot(yh_ref[...], wh_ref[...], preferred_element_type=F32)
    o_ref[...] = (jax.nn.sigmoid(gr_ref[...]) * pr
                  + jax.nn.sigmoid(gh_ref[...]) * ph).astype(o_ref.dtype)


def _merge(y_r, y_h, w_br, w_bh, z_rest, col_gr, col_gh, *, tm, tn):
    m, kr = y_r.shape
    kh = y_h.shape[1]
    d = w_br.shape[1]
    cgr, cgh = col_gr // tn, col_gh // tn
    return pl.pallas_call(
        _merge_kernel,
        out_shape=jax.ShapeDtypeStruct((m, d), BF16),
        grid=(m // tm, d // tn),
        in_specs=[pl.BlockSpec((tm, kr), lambda i, j: (i, 0)),
                  pl.BlockSpec((tm, kh), lambda i, j: (i, 0)),
                  pl.BlockSpec((kr, tn), lambda i, j: (0, j)),
                  pl.BlockSpec((kh, tn), lambda i, j: (0, j)),
                  pl.BlockSpec((tm, tn), lambda i, j: (i, j + cgr)),
                  pl.BlockSpec((tm, tn), lambda i, j: (i, j + cgh))],
        out_specs=pl.BlockSpec((tm, tn), lambda i, j: (i, j)),
        compiler_params=_cparams("arbitrary", "arbitrary"), name="merge",
    )(y_r, y_h, w_br, w_bh, z_rest, z_rest)


def _log2(n):
    assert n & (n - 1) == 0
    return n.bit_length() - 1


def _head_ones(n, group):
    sh = _log2(group)
    i = lax.broadcasted_iota(jnp.int32, (n, n), 0) >> sh
    j = lax.broadcasted_iota(jnp.int32, (n, n), 1) >> sh
    return jnp.where(i == j, 1.0, 0.0).astype(BF16)


def _rwkv_prep_kernel(z_ref, zp_ref, zn_ref, mup_ref, mun_ref, w0_ref, w2_ref, a0_ref, a2_ref,
                      g2_ref, kk_ref, ka_ref, rk_ref,
                      r_o, k_o, v_o, kkn_o, af_o, ab_o, lwf_o, lwb_o, g_o, bonus_o,
                      *, tm, mc, tc, tl, dr, wl, al, gl):
    row0 = pl.program_id(0) * tm
    in_ctx = row0 < mc
    pos = jnp.where(in_ctx, row0 % tc, (row0 - mc) % tl)
    seqlen = jnp.where(in_ctx, tc, tl)
    keep_prev = jnp.where(pos == 0, 0.0, 1.0)
    keep_next = jnp.where(pos + tm == seqlen, 0.0, 1.0)
    rows = lax.broadcasted_iota(jnp.int32, (tm, 1), 0)

    def shifted(c0, c1):
        z = z_ref[:, c0:c1]
        zp = jnp.where(rows == 0, keep_prev * zp_ref[7:8, c0:c1], pltpu.roll(z, 1, 0))
        zn = jnp.where(rows == tm - 1, keep_next * zn_ref[0:1, c0:c1], pltpu.roll(z, tm - 1, 0))
        return z + mup_ref[:, c0:c1] * (zp - z) + mun_ref[:, c0:c1] * (zn - z)

    bd = _head_ones(LANES, RWKV_HEAD)
    for j in range(dr // LANES):
        c0 = j * LANES
        r_o[:, c0:c0 + LANES] = shifted(c0, c0 + LANES)
        k = shifted(dr + c0, dr + c0 + LANES)
        k_o[:, c0:c0 + LANES] = k
        v_o[:, c0:c0 + LANES] = shifted(2 * dr + c0, 2 * dr + c0 + LANES)
        kk = k * kk_ref[:, c0:c0 + LANES]
        ss = _group_sum(kk * kk, bd)
        kkn_o[:, c0:c0 + LANES] = kk * lax.rsqrt(jnp.maximum(ss, 1e-24))

    c = 3 * dr
    for d, (lw_o, a_o) in enumerate(((lwf_o, af_o), (lwb_o, ab_o))):
        wd = shifted(c + d * wl, c + (d + 1) * wl)
        ad = shifted(c + 2 * wl + d * al, c + 2 * wl + (d + 1) * al)
        wpre = w0_ref[d:d + 1, :] + _dot(jnp.tanh(wd), w2_ref[d])
        lw_o[...] = -DECAY_SCALE * jax.nn.sigmoid(wpre)
        a_o[...] = jax.nn.sigmoid(a0_ref[d:d + 1, :] + _dot(ad, a2_ref[d]))
    gd = shifted(c + 2 * wl + 2 * al, c + 2 * wl + 2 * al + gl)
    g_o[...] = _dot(jax.nn.sigmoid(gd), g2_ref[...])

    for j in range(dr // LANES):
        sl = slice(j * LANES, (j + 1) * LANES)
        r, k, v = r_o[:, sl], k_o[:, sl], v_o[:, sl]
        bonus = 0.0
        for a_o in (af_o, ab_o):
            kd = k * (1.0 + (a_o[:, sl] - 1.0) * ka_ref[:, sl])
            bonus = bonus + _group_sum(r * kd * rk_ref[:, sl], bd) * v
        bonus_o[:, sl] = bonus


def _rwkv_prep(z_rest, p, dims, *, tm, zr_cols, dr, wl, al, gl):
    mc, tc, tl, nlb = dims
    m = z_rest.shape[0]
    nblk8 = m // 8
    t8 = tm // 8
    full = lambda shape: pl.BlockSpec(shape, lambda i: (0,) * len(shape))
    outs = [jax.ShapeDtypeStruct((m, dr), F32)] * 10
    kern = functools.partial(_rwkv_prep_kernel, tm=tm, mc=mc, tc=tc, tl=tl, dr=dr, wl=wl, al=al,
                             gl=gl)
    return pl.pallas_call(
        kern, out_shape=outs, grid=(m // tm,),
        in_specs=[pl.BlockSpec((tm, zr_cols), lambda i: (i, 0)),
                  pl.BlockSpec((8, zr_cols), lambda i: (jnp.maximum(i * t8 - 1, 0), 0)),
                  pl.BlockSpec((8, zr_cols), lambda i: (jnp.minimum((i + 1) * t8, nblk8 - 1), 0)),
                  full((1, zr_cols)), full((1, zr_cols)),
                  full((2, dr)), full((2, wl, dr)), full((2, dr)), full((2, al, dr)),
                  full((gl, dr)), full((1, dr)), full((1, dr)), full((1, dr))],
        out_specs=[pl.BlockSpec((tm, dr), lambda i: (i, 0))] * 10,
        compiler_params=_cparams("arbitrary"), name="rwkv_prep",
    )(z_rest, z_rest, z_rest, p["mu_prev"], p["mu_next"], p["w0"], p["w2"], p["a0"], p["a2"],
      p["g2"], p["k_k"], p["k_a"], p["r_k"])


def _stack_heads(x, lo):
    return jnp.concatenate([jnp.where(lo, x, 0.0), jnp.where(lo, 0.0, x)], axis=0)


SUB_BLOCK = 16
assert CHUNK // SUB_BLOCK == 4
assert CHUNK == RWKV_HEAD


def _rwkv_masks(rev):
    c = CHUNK
    n2 = 2 * c
    ti = lax.broadcasted_iota(jnp.int32, (c, c), 0)
    tj = lax.broadcasted_iota(jnp.int32, (c, c), 1)
    cum = jnp.where((tj >= ti) if rev else (tj <= ti), 1.0, 0.0).astype(BF16)
    cum = jnp.concatenate([cum, cum, cum], axis=1)
    row = lax.broadcasted_iota(jnp.int32, (c, LANES), 0)
    lane = lax.broadcasted_iota(jnp.int32, (c, LANES), 1)
    col = lane & (c - 1)
    lo = lane < RWKV_HEAD
    strict = (col > row) if rev else (col < row)
    incl = (col >= row) if rev else (col <= row)
    eye = jnp.where(col == row, 1.0, 0.0)
    sub = (row >> _log2(SUB_BLOCK)) == (col >> _log2(SUB_BLOCK))
    ii = lax.broadcasted_iota(jnp.int32, (n2, n2), 0)
    jj = lax.broadcasted_iota(jnp.int32, (n2, n2), 1)
    same_head = (ii >> _log2(c)) == (jj >> _log2(c))
    return cum, lo, strict, incl, eye, sub, same_head


def _rwkv_chunk(r, k, v, kk, a, lw, k_a, s, rev, masks):
    c = CHUNK
    n2 = 2 * c
    cum, lo, strict, incl, eye, sub, same_head = masks
    bc = _exact_rowmix(cum, lw)
    yield
    bex = bc - lw
    btot = bc[0:1, :] if rev else bc[c - 1:c, :]
    kd = k * (1.0 + (a - 1.0) * k_a)
    bv = kk * a
    e_neg = jnp.exp(-bc)
    e_tot = jnp.exp(btot - bc)
    stack = lambda x: _stack_heads(x.astype(BF16), lo)
    at = (-kk * jnp.exp(bex)).astype(BF16)
    rt_f = r * jnp.exp(bc)
    kc = (kd * e_tot).astype(BF16)
    bcc = (bv * e_tot).astype(BF16)
    s_v = stack(v)

    gram = _dot_nt(jnp.concatenate([at, rt_f.astype(BF16)], axis=0),
                   jnp.concatenate([stack(bv * e_neg), stack(kd * e_neg)], axis=0))
    yield
    nmat = jnp.where(strict, gram[:c, :n2], 0.0)
    a_ka = jnp.where(strict, gram[:c, n2:], 0.0)
    a_rb = jnp.where(incl, gram[c:, :n2], 0.0)
    a_rk = jnp.where(incl, gram[c:, n2:], 0.0)
    n_d = jnp.where(sub, nmat, 0.0)
    t_d = eye + n_d
    pw = _dot(n_d, stack(n_d))
    yield
    for _ in range(_log2(SUB_BLOCK) - 2):
        both = _dot(jnp.concatenate([t_d, pw], axis=0), stack(pw))
        yield
        t_d = t_d + both[:c]
        pw = both[c:]
    t_d = t_d + _dot(t_d, stack(pw))
    xv = _dot(jnp.concatenate([a_ka, a_rk], axis=0), s_v)
    yield
    x1, q_v = xv[:c], xv[c:]
    xm = _dot(t_d, stack(jnp.where(sub, 0.0, nmat)))
    yield
    xm2 = _dot(xm, stack(xm))
    yield
    ym = eye + xm + xm2 + _dot(xm, stack(xm2))
    yield
    tm_ = _dot(ym, stack(t_d))
    yield
    aw = _dot(tm_, jnp.concatenate([stack(at), stack(x1)], axis=1))
    yield
    ah, w = aw[:, :n2], aw[:, n2:]
    wa = _dot(a_rb, jnp.concatenate([stack(w), stack(ah)], axis=1))
    q = q_v + wa[:, :n2]
    rr = rt_f + wa[:, n2:]
    mm = jnp.where(same_head, _dot_tn(ah, bcc), 0.0)
    nn = jnp.where(same_head, _dot_tn(jnp.concatenate([v, w], axis=0),
                                      jnp.concatenate([kc, bcc], axis=0)), 0.0)
    yield
    o = _dot_nt(rr, s) + q
    s_new = s * jnp.exp(btot) + _dot(s, mm) + nn
    return o, s_new


def _rwkv_scan_kernel(*refs, has_init, emit_final, nchunk, npairs, aliased):
    it = iter(refs)
    fwd = [next(it) for _ in range(6)]
    bwd = [next(it) for _ in range(6)]
    ka_ref = next(it)
    s0_ref = next(it) if has_init else None
    if aliased:
        next(it), next(it)
    of_ref = next(it)
    ob_ref = next(it)
    sfin_ref = next(it) if emit_final else None
    s_scr = next(it)
    ci = pl.program_id(2)

    @pl.when(ci == 0)
    def _():
        if has_init:
            s_scr[...] = s0_ref[...]
        else:
            s_scr[...] = jnp.zeros_like(s_scr)

    chains, sinks = [], []
    for d, (ins, o_ref) in enumerate(((fwd, of_ref), (bwd, ob_ref))):
        masks = _rwkv_masks(rev=(d == 1))
        for hp in range(npairs):
            sl = slice(hp * LANES, (hp + 1) * LANES)
            r, k, v, kk, a, lw = (x[:, sl] for x in ins)
            chains.append(_rwkv_chunk(r, k, v, kk, a, lw, ka_ref[:, sl], s_scr[d, hp],
                                      rev=(d == 1), masks=masks))
            sinks.append((o_ref, sl, d, hp))
    for (o, s_new), (o_ref, sl, d, hp) in zip(_lockstep(chains), sinks):
        o_ref[:, sl] = o
        s_scr[d, hp] = s_new

    if emit_final:
        @pl.when(ci == nchunk - 1)
        def _():
            sfin_ref[...] = s_scr[...]


def _rwkv_scan(pre, k_a, s0_bd, *, row_off, nb, t, emit_final, npairs, into=None):
    m, dr = pre["r"].shape
    ngroup = dr // (LANES * npairs)
    nchunk = t // CHUNK
    rb = row_off // CHUNK
    has_init = s0_bd is not None
    fmap = lambda b, h, c: (rb + b * nchunk + c, h)
    bmap = lambda b, h, c: (rb + b * nchunk + (nchunk - 1 - c), h)
    blk = (CHUNK, LANES * npairs)
    in_specs = ([pl.BlockSpec(blk, fmap)] * 6 + [pl.BlockSpec(blk, bmap)] * 6
                + [pl.BlockSpec((1, LANES * npairs), lambda b, h, c: (0, h))])
    args = [pre["r"], pre["k"], pre["v"], pre["kk"], pre["a_f"], pre["lw_f"],
            pre["r"], pre["k"], pre["v"], pre["kk"], pre["a_b"], pre["lw_b"], k_a]
    sblk = pl.BlockSpec((None, 2, npairs, LANES, LANES), lambda b, h, c: (b, 0, h, 0, 0))
    if has_init:
        in_specs.append(sblk)
        args.append(s0_bd)
    aliases = {}
    if into is not None:
        aliases = {len(args): 0, len(args) + 1: 1}
        in_specs += [pl.BlockSpec(memory_space=pl.ANY)] * 2
        args += list(into)
    out_shape = [jax.ShapeDtypeStruct((m, dr), F32)] * 2
    out_specs = [pl.BlockSpec(blk, fmap), pl.BlockSpec(blk, bmap)]
    if emit_final:
        out_shape.append(jax.ShapeDtypeStruct((nb, 2, dr // LANES, LANES, LANES), F32))
        out_specs.append(sblk)
    kern = functools.partial(_rwkv_scan_kernel, has_init=has_init, emit_final=emit_final,
                             nchunk=nchunk, npairs=npairs, aliased=into is not None)
    return pl.pallas_call(
        kern, out_shape=out_shape, grid=(nb, ngroup, nchunk), in_specs=in_specs,
        out_specs=out_specs, scratch_shapes=[pltpu.VMEM((2, npairs, LANES, LANES), F32)],
        input_output_aliases=aliases,
        compiler_params=_cparams("arbitrary", "arbitrary", "arbitrary"), name="rwkv_scan",
    )(*args)


def _rwkv_post_kernel(of_ref, ob_ref, bonus_ref, g_ref, lnw_ref, lnb_ref, y_ref, *, dr):
    bd = _head_ones(LANES, RWKV_HEAD)
    inv = 1.0 / RWKV_HEAD
    for j in range(dr // LANES):
        sl = slice(j * LANES, (j + 1) * LANES)
        o = of_ref[:, sl] + ob_ref[:, sl]
        mean = _group_sum(o, bd) * inv
        oc = o - mean
        var = _group_sum(oc * oc, bd) * inv
        on = oc * lax.rsqrt(var + GN_EPS) * lnw_ref[:, sl] + lnb_ref[:, sl]
        y_ref[:, sl] = ((on + bonus_ref[:, sl]) * g_ref[:, sl]).astype(y_ref.dtype)


def _rwkv_post(o_f, o_b, pre, p, *, tm):
    rows, dr = o_f.shape
    own = pl.BlockSpec((tm, dr), lambda i: (i, 0))
    par = pl.BlockSpec((1, dr), lambda i: (0, 0))
    return pl.pallas_call(
        functools.partial(_rwkv_post_kernel, dr=dr),
        out_shape=jax.ShapeDtypeStruct((rows, dr), BF16),
        grid=(rows // tm,),
        in_specs=[own] * 4 + [par] * 2,
        out_specs=own,
        compiler_params=_cparams("arbitrary"), name="rwkv_post",
    )(o_f, o_b, pre["bonus"], pre["g"], p["ln_w"], p["ln_b"])


HGRN_LEVELS = 6


SUBLANES = 8


def _boundary_rows(b, lvl, rev, sub):
    c = CHUNK
    half = 1 << lvl
    blk = 2 * half
    src = (lambda base: base + half) if rev else (lambda base: base + half - 1)

    def rows(starts, nrows):
        return jnp.concatenate(
            [jnp.broadcast_to(b[src(s):src(s) + 1, :], (nrows, LANES)) for s in starts], axis=0)

    if blk >= SUBLANES:
        return rows(range(0, c, blk), blk)
    out = None
    for k in range(SUBLANES // blk):
        arr = rows(range(k * blk, c, SUBLANES), SUBLANES)
        out = arr if out is None else jnp.where(sub >= k * blk, arr, out)
    return out


def _hgrn_masks(rev):
    c = CHUNK
    ti = lax.broadcasted_iota(jnp.int32, (c, c), 0)
    tj = lax.broadcasted_iota(jnp.int32, (c, c), 1)
    late, early = (tj, ti) if rev else (ti, tj)
    sels = [((late >> lvl) == (early >> lvl) + 1) & (((early >> lvl) & 1) == 0)
            for lvl in range(HGRN_LEVELS)]
    cum = jnp.where((tj >= ti) if rev else (tj <= ti), 1.0, 0.0).astype(BF16)
    sub = lax.broadcasted_iota(jnp.int32, (c, LANES), 0) & (SUBLANES - 1)
    return jnp.concatenate([cum, cum, cum], axis=1), ti == tj, sels, sub


def _hgrn_chunk(q_pre, f_pre, v, lb, st, rev, masks):
    c = CHUNK
    cum3, diag, sels, sub = masks
    q = q_pre * jax.nn.sigmoid(q_pre)
    f = lb + (1.0 - lb) * jax.nn.sigmoid(f_pre)
    kf = 1.0 - f
    b = _exact_rowmix(cum3, jnp.log(f))
    yield
    btot = b[0:1, :] if rev else b[c - 1:c, :]
    q_b, kf_b = q.astype(BF16), kf.astype(BF16)
    attn = jnp.where(diag, _dot_nt(q_b, kf_b), 0.0)
    for lvl in range(HGRN_LEVELS):
        dq = b - _boundary_rows(b, lvl, rev, sub)
        wgt = jnp.exp(-jnp.abs(dq)).astype(BF16)
        attn = jnp.where(sels[lvl], _dot_nt(q_b * wgt, kf_b * wgt), attn)
    yield
    o = _dot_nt(q * jnp.exp(b), st) + _dot(attn, v)
    st_new = st * jnp.exp(btot) + _dot_tn(v, kf * jnp.exp(btot - b))
    return o, st_new


def _hgrn_scan_kernel(*refs, has_init, emit_final, nsteps, nheads, cols, aliased):
    it = iter(refs)
    qf_ref, ff_ref, vf_ref = next(it), next(it), next(it)
    qb_ref, fb_ref, vb_ref = next(it), next(it), next(it)
    lb_ref = next(it)
    s0_ref = next(it) if has_init else None
    if aliased:
        next(it), next(it)
    of_ref, ob_ref = next(it), next(it)
    sfin_ref = next(it) if emit_final else None
    s_scr = next(it)
    in_scr, out_scr = (next(it), next(it)) if cols is not None else (None, None)
    ci = pl.program_id(2)

    @pl.when(ci == 0)
    def _():
        if has_init:
            for d in range(2):
                for h in range(nheads):
                    s_scr[d, h] = s0_ref[d, h].T
        else:
            s_scr[...] = jnp.zeros_like(s_scr)

    ins = ((qf_ref, ff_ref, vf_ref), (qb_ref, fb_ref, vb_ref))
    outs = (of_ref, ob_ref)
    all_masks = [_hgrn_masks(rev=False), _hgrn_masks(rev=True)]

    def advance(load, store):
        chains, sinks = [], []
        for d in range(2):
            for h in range(nheads):
                sl = slice(h * LANES, (h + 1) * LANES)
                qp, fp, v = (load(d, k, sl) for k in range(3))
                chains.append(_hgrn_chunk(qp, fp, v, lb_ref[:, sl], s_scr[d, h], rev=(d == 1),
                                          masks=all_masks[d]))
                sinks.append((d, h, sl))
        for (o, s_new), (d, h, sl) in zip(_lockstep(chains), sinks):
            store(d, sl, o)
            s_scr[d, h] = s_new

    if cols is None:
        def store_rows(d, sl, o):
            outs[d][:, sl] = o
        advance(lambda d, k, sl: ins[d][k][:, sl], store_rows)
    else:
        for d in range(2):
            for k in range(3):
                in_scr[d, k] = jnp.swapaxes(ins[d][k][...], 0, 1)

        def body(j, carry):
            col = (j, cols - 1 - j)

            def store_col(d, sl, o):
                out_scr[d, col[d], :, sl] = o
            advance(lambda d, k, sl: in_scr[d, k, col[d], :, sl], store_col)
            return carry
        lax.fori_loop(0, cols, body, 0)
        for d in range(2):
            outs[d][...] = jnp.swapaxes(out_scr[d], 0, 1)

    if emit_final:
        @pl.when(ci == nsteps - 1)
        def _():
            for d in range(2):
                for h in range(nheads):
                    sfin_ref[d, h] = s_scr[d, h].T


HGRN_COLS = 8


def _hgrn_scan(zhs, lb, s0, *, row_off, nb, t, dg, emit_final, nheads, column_major,
               into=None):
    m = zhs.shape[0]
    ng = dg // (LANES * nheads)
    has_init = s0 is not None
    lanes = LANES * nheads
    if column_major:
        nrow = t // GRID_W
        assert nrow == CHUNK and m % t == 0 and row_off % t == 0
        nsteps = GRID_W // HGRN_COLS
        seq0 = row_off // t
        zsrc = zhs.reshape(m // t, nrow, GRID_W, 4 * dg)
        blk = (None, nrow, HGRN_COLS, lanes)
        fwd = lambda g: (lambda b, h, c: (b + seq0, 0, c, g * ng + h))
        bwd = lambda g: (lambda b, h, c: (b + seq0, 0, nsteps - 1 - c, g * ng + h))
        out_dims = (m // t, nrow, GRID_W, dg)
        ofwd = lambda b, h, c: (b + seq0, 0, c, h)
        obwd = lambda b, h, c: (b + seq0, 0, nsteps - 1 - c, h)
        cols = HGRN_COLS
    else:
        nsteps = t // CHUNK
        rb = row_off // CHUNK
        zsrc = zhs
        blk = (CHUNK, lanes)
        fwd = lambda g: (lambda b, h, c: (rb + b * nsteps + c, g * ng + h))
        bwd = lambda g: (lambda b, h, c: (rb + b * nsteps + nsteps - 1 - c, g * ng + h))
        out_dims = (m, dg)
        ofwd = lambda b, h, c: (rb + b * nsteps + c, h)
        obwd = lambda b, h, c: (rb + b * nsteps + nsteps - 1 - c, h)
        cols = None

    in_specs = [pl.BlockSpec(blk, fwd(0)), pl.BlockSpec(blk, fwd(1)), pl.BlockSpec(blk, fwd(3)),
                pl.BlockSpec(blk, bwd(0)), pl.BlockSpec(blk, bwd(2)), pl.BlockSpec(blk, bwd(3)),
                pl.BlockSpec((1, lanes), lambda b, h, c: (0, h))]
    args = [zsrc] * 6 + [lb]
    sblk = pl.BlockSpec((None, 2, nheads, LANES, LANES), lambda b, h, c: (b, 0, h, 0, 0))
    if has_init:
        in_specs.append(sblk)
        args.append(s0)
    aliases = {}
    if into is not None:
        aliases = {len(args): 0, len(args) + 1: 1}
        in_specs += [pl.BlockSpec(memory_space=pl.ANY)] * 2
        args += [o.reshape(out_dims) for o in into]
    out_shape = [jax.ShapeDtypeStruct(out_dims, F32)] * 2
    out_specs = [pl.BlockSpec(blk, ofwd), pl.BlockSpec(blk, obwd)]
    if emit_final:
        out_shape.append(jax.ShapeDtypeStruct((nb, 2, dg // LANES, LANES, LANES), F32))
        out_specs.append(sblk)
    kern = functools.partial(_hgrn_scan_kernel, has_init=has_init, emit_final=emit_final,
                             nsteps=nsteps, nheads=nheads, cols=cols, aliased=into is not None)
    scratch = [pltpu.VMEM((2, nheads, LANES, LANES), F32)]
    if column_major:
        scratch += [pltpu.VMEM((2, 3, cols, CHUNK, lanes), F32),
                    pltpu.VMEM((2, cols, CHUNK, lanes), F32)]
    outs = pl.pallas_call(
        kern, out_shape=out_shape, grid=(nb, ng, nsteps), in_specs=in_specs, out_specs=out_specs,
        scratch_shapes=scratch, input_output_aliases=aliases,
        compiler_params=_cparams("arbitrary", "arbitrary", "arbitrary"), name="hgrn_scan",
    )(*args)
    return [outs[0].reshape(m, dg), outs[1].reshape(m, dg)] + list(outs[2:])


def _hgrn_post_kernel(of_ref, ob_ref, gate_ref, gain_ref, y_ref, *, dg):
    gain = gain_ref[...]
    for j in range(dg // LANES):
        sl = slice(j * LANES, (j + 1) * LANES)
        o = of_ref[:, sl] + ob_ref[:, sl]
        on = o * lax.rsqrt(jnp.mean(o * o, axis=-1, keepdims=True) + RMS_EPS) * gain
        gt = gate_ref[:, sl]
        y_ref[:, sl] = (on * (gt * jax.nn.sigmoid(gt))).astype(y_ref.dtype)


def _hgrn_post(o_f, o_b, z_gates, gain, *, gate_col):
    rows_all, dg = o_f.shape
    kern = functools.partial(_hgrn_post_kernel, dg=dg)
    gcb = gate_col // dg
    par = pl.BlockSpec((1, LANES), lambda *_: (0, 0))
    tm = min(256, rows_all)
    ospec = pl.BlockSpec((tm, dg), lambda i: (i, 0))
    return pl.pallas_call(
        kern, out_shape=jax.ShapeDtypeStruct((rows_all, dg), BF16), grid=(rows_all // tm,),
        in_specs=[ospec, ospec, pl.BlockSpec((tm, dg), lambda i: (i, gcb)), par],
        out_specs=ospec,
        compiler_params=_cparams("arbitrary"), name="hgrn_post",
    )(o_f, o_b, z_gates, gain)


def _largest_divisor(n, candidates):
    for cand in candidates:
        if n % cand == 0:
            return cand
    raise ValueError(f"no tile in {candidates} divides {n}")


def _pad_to(x, axis, size):
    pad = size - x.shape[axis]
    if pad == 0:
        return x
    widths = [(0, 0)] * x.ndim
    widths[axis] = (0, pad)
    return jnp.pad(x, widths)


def _pair_state_to_bd(s):
    b, two, h, n, _ = s.shape
    s = s.reshape(b, two, h // 2, 2, n, n)
    z = jnp.zeros_like(s[:, :, :, 0])
    top = jnp.concatenate([s[:, :, :, 0], z], axis=-1)
    bot = jnp.concatenate([z, s[:, :, :, 1]], axis=-1)
    return jnp.concatenate([top, bot], axis=-2)


def _bd_to_pair_state(sbd):
    b, two, hp, n2, _ = sbd.shape
    n = n2 // 2
    s0 = sbd[:, :, :, :n, :n]
    s1 = sbd[:, :, :, n:, n:]
    return jnp.stack([s0, s1], axis=3).reshape(b, two, hp * 2, n, n)


def kernel(x_prompt, x_sample, state_rwkv, state_hgrn, c, c_ctx, w_mod, b_mod, norm_ffn1, norm_mix, norm_ffn2, ffn1_w_in, ffn1_w_out, ffn2_w_in, ffn2_w_out, w_in, rwkv_mu_prev, rwkv_mu_next, rwkv_w0, rwkv_w2, rwkv_a0, rwkv_a2, rwkv_g2, rwkv_k_k, rwkv_k_a, rwkv_r_k, rwkv_ln_w, rwkv_ln_b, hgrn_lb_logits, hgrn_norm, w_branch_rwkv, w_branch_hgrn, w_out, norm_final):
    nbc, tc, d = x_prompt.shape
    nbl, tl, _ = x_sample.shape
    depth = w_mod.shape[0]
    mc, ml = nbc * tc, nbl * tl
    m = mc + ml
    dims = (mc, tc, tl, nbl)
    dr = rwkv_w0.shape[-1]
    dg = w_branch_hgrn.shape[1]
    dff = ffn1_w_out.shape[1]
    wl, al, gl = rwkv_w2.shape[2], rwkv_a2.shape[2], rwkv_g2.shape[1]
    glp = -(-gl // LANES) * LANES
    rw_cols = 3 * dr + 2 * wl + 2 * al + gl
    zr_cols = 3 * dr + 2 * wl + 2 * al + glp
    assert nbl < MOD_ROWS and tl % GRID_W == 0 and tl // GRID_W == CHUNK and tc % CHUNK == 0

    tm_norm = min(256, tc)
    tm_proj = min(1024, mc)
    tm_merge = min(512, mc)
    tn_of = lambda n: _largest_divisor(n, (1024, 512, 256, 128))
    tf = _largest_divisor(dff, (256, 128))

    lb_all = jnp.cumsum(jax.nn.softmax(hgrn_lb_logits.astype(F32), axis=0), axis=0)
    cond = jnp.concatenate([c, c_ctx[None, :], jnp.zeros((MOD_ROWS - nbl - 1, d), F32)], axis=0)

    x = (x_prompt.reshape(mc, d), x_sample.reshape(ml, d))
    new_r, new_h = [], []
    for l in range(depth):
        mod3 = _modulation(cond, w_mod[l], b_mod[l], tn_of(N_MOD * d)).reshape(MOD_ROWS, N_MOD, d)

        h = _resnorm(x, None, mod3, norm_ffn1[l], dims, res_scale=None, gate_idx=None, mod_idx=0,
                     emit_x=False, out_dtype=BF16, tm=tm_norm)[0]
        delta = _ffn(h, ffn1_w_in[l].astype(BF16), ffn1_w_out[l].astype(BF16), tm=tm_proj, tf=tf)

        x, h = _resnorm(x, delta, mod3, norm_mix[l], dims, res_scale=FFN_HALF, gate_idx=2,
                        mod_idx=3, emit_x=True, out_dtype=BF16, tm=tm_norm)
        wi = w_in[l]
        hg0 = rw_cols
        w_r = _pad_to(wi[:, :rw_cols], 1, zr_cols).astype(BF16)
        w_hs = wi[:, hg0:hg0 + 4 * dg].astype(BF16)
        w_gates = wi[:, hg0 + 4 * dg:].astype(BF16)
        col_gate, col_gr, col_gh = 0, dg, dg + d
        z_r = _matmul(h, w_r, out_dtype=F32, tm=tm_proj, tn=tn_of(zr_cols), name="proj_rwkv")
        z_g = _matmul(h, w_gates, out_dtype=F32, tm=tm_proj, tn=tn_of(dg), name="proj_gates")
        zhs = _matmul(h, w_hs, out_dtype=F32, tm=tm_proj, tn=tn_of(dg), name="proj_hgrn")

        p = {
            "mu_prev": _pad_to(rwkv_mu_prev[l][None, :], 1, zr_cols),
            "mu_next": _pad_to(rwkv_mu_next[l][None, :], 1, zr_cols),
            "w0": rwkv_w0[l], "w2": rwkv_w2[l].astype(BF16),
            "a0": rwkv_a0[l], "a2": rwkv_a2[l].astype(BF16),
            "g2": _pad_to(rwkv_g2[l], 0, glp).astype(BF16),
            "k_k": rwkv_k_k[l][None, :], "k_a": rwkv_k_a[l][None, :],
            "r_k": rwkv_r_k[l].reshape(1, dr),
            "ln_w": rwkv_ln_w[l][None, :], "ln_b": rwkv_ln_b[l][None, :],
        }
        names = ("r", "k", "v", "kk", "a_f", "a_b", "lw_f", "lw_b", "g", "bonus")
        pre = dict(zip(names, _rwkv_prep(z_r, p, dims, tm=min(128, tc), zr_cols=zr_cols, dr=dr,
                                         wl=wl, al=al, gl=glp)))
        npairs = _largest_divisor(dr // LANES, (16, 8, 4, 2, 1))
        of_c, ob_c, sr_c = _rwkv_scan(pre, p["k_a"], None, row_off=0, nb=nbc, t=tc,
                                      emit_final=True, npairs=npairs)
        o_f, o_b = _rwkv_scan(pre, p["k_a"], _pair_state_to_bd(state_rwkv[:, l].astype(F32)),
                              row_off=mc, nb=nbl, t=tl, emit_final=False, npairs=npairs,
                              into=(of_c, ob_c))
        y_r = _rwkv_post(o_f, o_b, pre, p, tm=min(256, tc))
        new_r.append(_bd_to_pair_state(sr_c))

        lb = lb_all[l][None, :]
        gain_h = hgrn_norm[l][None, :]
        nheads = _largest_divisor(dg // LANES, (8, 4, 2, 1))
        nheads_cols = _largest_divisor(dg // LANES, (4, 2, 1))
        hf_c, hb_c, sh_c = _hgrn_scan(zhs, lb, None, row_off=0, nb=nbc, t=tc, dg=dg,
                                      emit_final=True, nheads=nheads, column_major=False)
        h_f, h_b = _hgrn_scan(zhs, lb, state_hgrn[:, l].astype(F32), row_off=mc, nb=nbl, t=tl,
                              dg=dg, emit_final=False, nheads=nheads_cols, column_major=True,
                              into=(hf_c, hb_c))
        y_h = _hgrn_post(h_f, h_b, z_g, gain_h, gate_col=col_gate)
        new_h.append(sh_c)

        merged = _merge(y_r, y_h, w_branch_rwkv[l].astype(BF16), w_branch_hgrn[l].astype(BF16),
                        z_g, col_gr, col_gh, tm=tm_merge, tn=tn_of(dg))
        delta = _matmul(merged, w_out[l].astype(BF16), out_dtype=F32, tm=tm_proj, tn=tn_of(d),
                        name="mix_out")

        x, h = _resnorm(x, delta, mod3, norm_ffn2[l], dims, res_scale=1.0, gate_idx=5, mod_idx=6,
                        emit_x=True, out_dtype=BF16, tm=tm_norm)
        delta = _ffn(h, ffn2_w_in[l].astype(BF16), ffn2_w_out[l].astype(BF16), tm=tm_proj, tf=tf)
        last = l == depth - 1
        if not last:
            x = _resnorm(x, delta, mod3, norm_ffn2[l], dims, res_scale=FFN_HALF, gate_idx=8,
                         mod_idx=None, emit_x=True, out_dtype=F32, tm=tm_norm)[0]

    y_c = _resnorm(x, delta, mod3, norm_final, dims, res_scale=FFN_HALF, gate_idx=8, mod_idx=None,
                   emit_x=False, out_dtype=F32, tm=tm_norm, row_off=0, rows=mc)[0]
    y_l = _resnorm(x, delta, mod3, norm_final, dims, res_scale=FFN_HALF, gate_idx=8, mod_idx=None,
                   emit_x=False, out_dtype=F32, tm=tm_norm, row_off=mc, rows=ml)[0]
    return (y_c.reshape(nbc, tc, d), y_l.reshape(nbl, tl, d),
            jnp.stack(new_r, axis=1), jnp.stack(new_h, axis=1))
```

```python
import functools
import math

import jax
import jax.numpy as jnp
from jax import lax
from jax.experimental import pallas as pl
from jax.experimental.pallas import tpu as pltpu

F32 = jnp.float32
BF16 = jnp.bfloat16

CHUNK = 64
GRID_W = 64
RWKV_HEAD = 64
N_MOD = 9
RMS_EPS = 1e-6
GN_EPS = 64e-5
FFN_HALF = 0.5
DECAY_SCALE = math.exp(-0.5)
LANES = 128
MOD_ROWS = 16
VMEM_LIMIT_BYTES = 56 * 1024 * 1024


def _cparams(*sem):
    return pltpu.CompilerParams(dimension_semantics=sem, vmem_limit_bytes=VMEM_LIMIT_BYTES)


def _dot(a, b):
    return jnp.dot(a.astype(BF16), b.astype(BF16), preferred_element_type=F32)


def _dot_nt(a, b):
    return lax.dot_general(a.astype(BF16), b.astype(BF16), (((1,), (1,)), ((), ())),
                           preferred_element_type=F32)


def _dot_tn(a, b):
    return jnp.dot(a.T.astype(BF16), b.astype(BF16), preferred_element_type=F32)


def _split2(x):
    hi = x.astype(BF16)
    return hi, (x - hi.astype(F32)).astype(BF16)


def _split3(x):
    x1 = x.astype(BF16)
    d1 = x - x1.astype(F32)
    x2 = d1.astype(BF16)
    x3 = (d1 - x2.astype(F32)).astype(BF16)
    return x1, x2, x3


def _exact_rowmix(mat3, x):
    return jnp.dot(mat3, jnp.concatenate(_split3(x), axis=0), preferred_element_type=F32)


def _group_sum(x, bd_ones):
    x1, x2 = _split2(x)
    return (jnp.dot(x1, bd_ones, preferred_element_type=F32)
            + jnp.dot(x2, bd_ones, preferred_element_type=F32))


def _lockstep(chains):
    results = [None] * len(chains)
    active = list(enumerate(chains))
    while active:
        still = []
        for idx, chain in active:
            try:
                next(chain)
                still.append((idx, chain))
            except StopIteration as done:
                results[idx] = done.value
        active = still
    return results


def _mod_kernel(c_ref, w_ref, b_ref, o_ref):
    c = c_ref[...]
    s = c * jax.nn.sigmoid(c)
    o_ref[...] = _dot(s, w_ref[...]) + b_ref[...]


def _modulation(cond, w_mod, b_mod, tn):
    d, n = w_mod.shape
    return pl.pallas_call(
        _mod_kernel,
        out_shape=jax.ShapeDtypeStruct((MOD_ROWS, n), F32),
        grid=(n // tn,),
        in_specs=[pl.BlockSpec((MOD_ROWS, d), lambda j: (0, 0)),
                  pl.BlockSpec((d, tn), lambda j: (0, j)),
                  pl.BlockSpec((1, tn), lambda j: (0, j))],
        out_specs=pl.BlockSpec((MOD_ROWS, tn), lambda j: (0, j)),
        compiler_params=_cparams("arbitrary"),
        name="modulation",
    )(cond, w_mod, b_mod.reshape(1, n))


def _resnorm_kernel(*refs, has_delta, res_scale, gate_idx, mod_idx, emit_x, ctx_tiles):
    it = iter(refs)
    x_ref = next(it)
    xl_ref = next(it) if ctx_tiles is not None else None
    delta_ref = next(it) if has_delta else None
    mod_ref = next(it) if (has_delta or mod_idx is not None) else None
    gain_ref = next(it)
    xo_ref = next(it) if emit_x else None
    h_ref = next(it)
    x = x_ref[...]
    if ctx_tiles is not None:
        x = jnp.where(pl.program_id(0) < ctx_tiles, x, xl_ref[...])
    if has_delta:
        g = mod_ref[gate_idx:gate_idx + 1, :]
        x = x + g * (res_scale * delta_ref[...])
    if emit_x:
        xo_ref[...] = x
    y = x * lax.rsqrt(jnp.mean(x * x, axis=-1, keepdims=True) + RMS_EPS)
    y = y * gain_ref[...]
    if mod_idx is not None:
        sh = mod_ref[mod_idx:mod_idx + 1, :]
        sc = mod_ref[mod_idx + 1:mod_idx + 2, :]
        y = y * (1.0 + sc) + sh
    h_ref[...] = y.astype(h_ref.dtype)


def _resnorm(x, delta, mod3, gain, dims, *, res_scale, gate_idx, mod_idx, emit_x, out_dtype,
             tm, row_off=0, rows=None):
    mc, tc, tl, nlb = dims
    n_ctx_tiles = mc // tm
    tiles_per_lat = tl // tm
    ro = row_off // tm
    split_x = isinstance(x, tuple)
    if split_x:
        assert row_off == 0 and rows is None
        d = x[0].shape[1]
        m_all = x[0].shape[0] + x[1].shape[0]
        in_specs = [pl.BlockSpec((tm, d), lambda i: (jnp.minimum(i, n_ctx_tiles - 1), 0)),
                    pl.BlockSpec((tm, d), lambda i: (jnp.maximum(i - n_ctx_tiles, 0), 0))]
        args = list(x)
    else:
        d = x.shape[1]
        m_all = rows if rows is not None else x.shape[0]
        in_specs = [pl.BlockSpec((tm, d), lambda i: (i + ro, 0))]
        args = [x]

    def mod_row(i):
        gi = i + ro
        return jnp.where(gi < n_ctx_tiles, nlb, (gi - n_ctx_tiles) // tiles_per_lat)

    has_delta = delta is not None
    if has_delta:
        in_specs.append(pl.BlockSpec((tm, d), lambda i: (i + ro, 0)))
        args.append(delta)
    if has_delta or mod_idx is not None:
        in_specs.append(pl.BlockSpec((None, N_MOD, d), lambda i: (mod_row(i), 0, 0)))
        args.append(mod3)
    in_specs.append(pl.BlockSpec((1, d), lambda i: (0, 0)))
    args.append(gain.reshape(1, d))
    out_shape, out_specs = [], []
    if emit_x:
        out_shape.append(jax.ShapeDtypeStruct((m_all, d), F32))
        out_specs.append(pl.BlockSpec((tm, d), lambda i: (i, 0)))
    out_shape.append(jax.ShapeDtypeStruct((m_all, d), out_dtype))
    out_specs.append(pl.BlockSpec((tm, d), lambda i: (i, 0)))
    kern = functools.partial(_resnorm_kernel, has_delta=has_delta, res_scale=res_scale,
                             gate_idx=gate_idx, mod_idx=mod_idx, emit_x=emit_x,
                             ctx_tiles=n_ctx_tiles if split_x else None)
    return pl.pallas_call(
        kern, out_shape=out_shape, grid=(m_all // tm,), in_specs=in_specs, out_specs=out_specs,
        compiler_params=_cparams("arbitrary"), name="resnorm",
    )(*args)


def _mm_kernel(a_ref, w_ref, o_ref):
    o_ref[...] = jnp.dot(a_ref[...], w_ref[...], preferred_element_type=F32).astype(o_ref.dtype)


def _matmul(a, w, *, out_dtype, tm, tn, row_off=0, rows=None, name="matmul"):
    k = a.shape[1]
    n = w.shape[1]
    rows = a.shape[0] if rows is None else rows
    ro = row_off // tm
    return pl.pallas_call(
        _mm_kernel,
        out_shape=jax.ShapeDtypeStruct((rows, n), out_dtype),
        grid=(rows // tm, n // tn),
        in_specs=[pl.BlockSpec((tm, k), lambda i, j: (i + ro, 0)),
                  pl.BlockSpec((k, tn), lambda i, j: (0, j))],
        out_specs=pl.BlockSpec((tm, tn), lambda i, j: (i, j)),
        compiler_params=_cparams("arbitrary", "arbitrary"), name=name,
    )(a, w)


def _ffn_kernel(h_ref, wa_ref, wb_ref, wo_ref, o_ref):
    @pl.when(pl.program_id(1) == 0)
    def _():
        o_ref[...] = jnp.zeros_like(o_ref)

    h = h_ref[...]
    a = jnp.dot(h, wa_ref[...], preferred_element_type=F32)
    b = jnp.dot(h, wb_ref[...], preferred_element_type=F32)
    u = (a * jax.nn.sigmoid(a) * b).astype(BF16)
    o_ref[...] += jnp.dot(u, wo_ref[...], preferred_element_type=F32)


def _ffn(h, w_in, w_out, *, tm, tf):
    m, d = h.shape
    dff = w_out.shape[0]
    nf = dff // tf
    return pl.pallas_call(
        _ffn_kernel,
        out_shape=jax.ShapeDtypeStruct((m, d), F32),
        grid=(m // tm, nf),
        in_specs=[pl.BlockSpec((tm, d), lambda i, j: (i, 0)),
                  pl.BlockSpec((d, tf), lambda i, j: (0, j)),
                  pl.BlockSpec((d, tf), lambda i, j: (0, j + nf)),
                  pl.BlockSpec((tf, d), lambda i, j: (j, 0))],
        out_specs=pl.BlockSpec((tm, d), lambda i, j: (i, 0), pipeline_mode=pl.Buffered(1)),
        compiler_params=_cparams("arbitrary", "arbitrary"), name="ffn",
    )(h, w_in, w_in, w_out)


def _merge_kernel(yr_ref, yh_ref, wr_ref, wh_ref, gr_ref, gh_ref, o_ref):
    pr = jnp.dot(yr_ref[...], wr_ref[...], preferred_element_type=F32)
    ph = jnp.dot(yh_ref[...], wh_ref[...], preferred_element_type=F32)
    o_ref[...] = (jax.nn.sigmoid(gr_ref[...]) * pr
                  + jax.nn.sigmoid(gh_ref[...]) * ph).astype(o_ref.dtype)


def _merge(y_r, y_h, w_br, w_bh, z_rest, col_gr, col_gh, *, tm, tn):
    m, kr = y_r.shape
    kh = y_h.shape[1]
    d = w_br.shape[1]
    cgr, cgh = col_gr // tn, col_gh // tn
    return pl.pallas_call(
        _merge_kernel,
        out_shape=jax.ShapeDtypeStruct((m, d), BF16),
        grid=(m // tm, d // tn),
        in_specs=[pl.BlockSpec((tm, kr), lambda i, j: (i, 0)),
                  pl.BlockSpec((tm, kh), lambda i, j: (i, 0)),
                  pl.BlockSpec((kr, tn), lambda i, j: (0, j)),
                  pl.BlockSpec((kh, tn), lambda i, j: (0, j)),
                  pl.BlockSpec((tm, tn), lambda i, j: (i, j + cgr)),
                  pl.BlockSpec((tm, tn), lambda i, j: (i, j + cgh))],
        out_specs=pl.BlockSpec((tm, tn), lambda i, j: (i, j)),
        compiler_params=_cparams("arbitrary", "arbitrary"), name="merge",
    )(y_r, y_h, w_br, w_bh, z_rest, z_rest)


def _log2(n):
    assert n & (n - 1) == 0
    return n.bit_length() - 1


def _head_ones(n, group):
    sh = _log2(group)
    i = lax.broadcasted_iota(jnp.int32, (n, n), 0) >> sh
    j = lax.broadcasted_iota(jnp.int32, (n, n), 1) >> sh
    return jnp.where(i == j, 1.0, 0.0).astype(BF16)


def _rwkv_prep_kernel(z_ref, zp_ref, zn_ref, mup_ref, mun_ref, w0_ref, w2_ref, a0_ref, a2_ref,
                      g2_ref, kk_ref, ka_ref, rk_ref,
                      r_o, k_o, v_o, kkn_o, af_o, ab_o, lwf_o, lwb_o, g_o, bonus_o,
                      *, tm, mc, tc, tl, dr, wl, al, gl):
    row0 = pl.program_id(0) * tm
    in_ctx = row0 < mc
    pos = jnp.where(in_ctx, row0 % tc, (row0 - mc) % tl)
    seqlen = jnp.where(in_ctx, tc, tl)
    keep_prev = jnp.where(pos == 0, 0.0, 1.0)
    keep_next = jnp.where(pos + tm == seqlen, 0.0, 1.0)
    rows = lax.broadcasted_iota(jnp.int32, (tm, 1), 0)

    def shifted(c0, c1):
        z = z_ref[:, c0:c1]
        zp = jnp.where(rows == 0, keep_prev * zp_ref[7:8, c0:c1], pltpu.roll(z, 1, 0))
        zn = jnp.where(rows == tm - 1, keep_next * zn_ref[0:1, c0:c1], pltpu.roll(z, tm - 1, 0))
        return z + mup_ref[:, c0:c1] * (zp - z) + mun_ref[:, c0:c1] * (zn - z)

    bd = _head_ones(LANES, RWKV_HEAD)
    for j in range(dr // LANES):
        c0 = j * LANES
        r_o[:, c0:c0 + LANES] = shifted(c0, c0 + LANES)
        k = shifted(dr + c0, dr + c0 + LANES)
        k_o[:, c0:c0 + LANES] = k
        v_o[:, c0:c0 + LANES] = shifted(2 * dr + c0, 2 * dr + c0 + LANES)
        kk = k * kk_ref[:, c0:c0 + LANES]
        ss = _group_sum(kk * kk, bd)
        kkn_o[:, c0:c0 + LANES] = kk * lax.rsqrt(jnp.maximum(ss, 1e-24))

    c = 3 * dr
    for d, (lw_o, a_o) in enumerate(((lwf_o, af_o), (lwb_o, ab_o))):
        wd = shifted(c + d * wl, c + (d + 1) * wl)
        ad = shifted(c + 2 * wl + d * al, c + 2 * wl + (d + 1) * al)
        wpre = w0_ref[d:d + 1, :] + _dot(jnp.tanh(wd), w2_ref[d])
        lw_o[...] = -DECAY_SCALE * jax.nn.sigmoid(wpre)
        a_o[...] = jax.nn.sigmoid(a0_ref[d:d + 1, :] + _dot(ad, a2_ref[d]))
    gd = shifted(c + 2 * wl + 2 * al, c + 2 * wl + 2 * al + gl)
    g_o[...] = _dot(jax.nn.sigmoid(gd), g2_ref[...])

    for j in range(dr // LANES):
        sl = slice(j * LANES, (j + 1) * LANES)
        r, k, v = r_o[:, sl], k_o[:, sl], v_o[:, sl]
        bonus = 0.0
        for a_o in (af_o, ab_o):
            kd = k * (1.0 + (a_o[:, sl] - 1.0) * ka_ref[:, sl])
            bonus = bonus + _group_sum(r * kd * rk_ref[:, sl], bd) * v
        bonus_o[:, sl] = bonus


def _rwkv_prep(z_rest, p, dims, *, tm, zr_cols, dr, wl, al, gl):
    mc, tc, tl, nlb = dims
    m = z_rest.shape[0]
    nblk8 = m // 8
    t8 = tm // 8
    full = lambda shape: pl.BlockSpec(shape, lambda i: (0,) * len(shape))
    outs = [jax.ShapeDtypeStruct((m, dr), F32)] * 10
    kern = functools.partial(_rwkv_prep_kernel, tm=tm, mc=mc, tc=tc, tl=tl, dr=dr, wl=wl, al=al,
                             gl=gl)
    return pl.pallas_call(
        kern, out_shape=outs, grid=(m // tm,),
        in_specs=[pl.BlockSpec((tm, zr_cols), lambda i: (i, 0)),
                  pl.BlockSpec((8, zr_cols), lambda i: (jnp.maximum(i * t8 - 1, 0), 0)),
                  pl.BlockSpec((8, zr_cols), lambda i: (jnp.minimum((i + 1) * t8, nblk8 - 1), 0)),
                  full((1, zr_cols)), full((1, zr_cols)),
                  full((2, dr)), full((2, wl, dr)), full((2, dr)), full((2, al, dr)),
                  full((gl, dr)), full((1, dr)), full((1, dr)), full((1, dr))],
        out_specs=[pl.BlockSpec((tm, dr), lambda i: (i, 0))] * 10,
        compiler_params=_cparams("arbitrary"), name="rwkv_prep",
    )(z_rest, z_rest, z_rest, p["mu_prev"], p["mu_next"], p["w0"], p["w2"], p["a0"], p["a2"],
      p["g2"], p["k_k"], p["k_a"], p["r_k"])


def _stack_heads(x, lo):
    return jnp.concatenate([jnp.where(lo, x, 0.0), jnp.where(lo, 0.0, x)], axis=0)


SUB_BLOCK = 16
assert CHUNK // SUB_BLOCK == 4
assert CHUNK == RWKV_HEAD


def _rwkv_masks(rev):
    c = CHUNK
    n2 = 2 * c
    ti = lax.broadcasted_iota(jnp.int32, (c, c), 0)
    tj = lax.broadcasted_iota(jnp.int32, (c, c), 1)
    cum = jnp.where((tj >= ti) if rev else (tj <= ti), 1.0, 0.0).astype(BF16)
    cum = jnp.concatenate([cum, cum, cum], axis=1)
    row = lax.broadcasted_iota(jnp.int32, (c, LANES), 0)
    lane = lax.broadcasted_iota(jnp.int32, (c, LANES), 1)
    col = lane & (c - 1)
    lo = lane < RWKV_HEAD
    strict = (col > row) if rev else (col < row)
    incl = (col >= row) if rev else (col <= row)
    eye = jnp.where(col == row, 1.0, 0.0)
    sub = (row >> _log2(SUB_BLOCK)) == (col >> _log2(SUB_BLOCK))
    ii = lax.broadcasted_iota(jnp.int32, (n2, n2), 0)
    jj = lax.broadcasted_iota(jnp.int32, (n2, n2), 1)
    same_head = (ii >> _log2(c)) == (jj >> _log2(c))
    return cum, lo, strict, incl, eye, sub, same_head


def _rwkv_chunk(r, k, v, kk, a, lw, k_a, s, rev, masks):
    c = CHUNK
    n2 = 2 * c
    cum, lo, strict, incl, eye, sub, same_head = masks
    bc = _exact_rowmix(cum, lw)
    yield
    bex = bc - lw
    btot = bc[0:1, :] if rev else bc[c - 1:c, :]
    kd = k * (1.0 + (a - 1.0) * k_a)
    bv = kk * a
    e_neg = jnp.exp(-bc)
    e_tot = jnp.exp(btot - bc)
    stack = lambda x: _stack_heads(x.astype(BF16), lo)
    at = (-kk * jnp.exp(bex)).astype(BF16)
    rt_f = r * jnp.exp(bc)
    kc = (kd * e_tot).astype(BF16)
    bcc = (bv * e_tot).astype(BF16)
    s_v = stack(v)

    gram = _dot_nt(jnp.concatenate([at, rt_f.astype(BF16)], axis=0),
                   jnp.concatenate([stack(bv * e_neg), stack(kd * e_neg)], axis=0))
    yield
    nmat = jnp.where(strict, gram[:c, :n2], 0.0)
    a_ka = jnp.where(strict, gram[:c, n2:], 0.0)
    a_rb = jnp.where(incl, gram[c:, :n2], 0.0)
    a_rk = jnp.where(incl, gram[c:, n2:], 0.0)
    n_d = jnp.where(sub, nmat, 0.0)
    t_d = eye + n_d
    pw = _dot(n_d, stack(n_d))
    yield
    for _ in range(_log2(SUB_BLOCK) - 2):
        both = _dot(jnp.concatenate([t_d, pw], axis=0), stack(pw))
        yield
        t_d = t_d + both[:c]
        pw = both[c:]
    t_d = t_d + _dot(t_d, stack(pw))
    xv = _dot(jnp.concatenate([a_ka, a_rk], axis=0), s_v)
    yield
    x1, q_v = xv[:c], xv[c:]
    xm = _dot(t_d, stack(jnp.where(sub, 0.0, nmat)))
    yield
    xm2 = _dot(xm, stack(xm))
    yield
    ym = eye + xm + xm2 + _dot(xm, stack(xm2))
    yield
    tm_ = _dot(ym, stack(t_d))
    yield
    aw = _dot(tm_, jnp.concatenate([stack(at), stack(x1)], axis=1))
    yield
    ah, w = aw[:, :n2], aw[:, n2:]
    wa = _dot(a_rb, jnp.concatenate([stack(w), stack(ah)], axis=1))
    q = q_v + wa[:, :n2]
    rr = rt_f + wa[:, n2:]
    mm = jnp.where(same_head, _dot_tn(ah, bcc), 0.0)
    nn = jnp.where(same_head, _dot_tn(jnp.concatenate([v, w], axis=0),
                                      jnp.concatenate([kc, bcc], axis=0)), 0.0)
    yield
    o = _dot_nt(rr, s) + q
    s_new = s * jnp.exp(btot) + _dot(s, mm) + nn
    return o, s_new


def _rwkv_scan_kernel(*refs, has_init, emit_final, nchunk, npairs, aliased):
    it = iter(refs)
    fwd = [next(it) for _ in range(6)]
    bwd = [next(it) for _ in range(6)]
    ka_ref = next(it)
    s0_ref = next(it) if has_init else None
    if aliased:
        next(it), next(it)
    of_ref = next(it)
    ob_ref = next(it)
    sfin_ref = next(it) if emit_final else None
    s_scr = next(it)
    ci = pl.program_id(2)

    @pl.when(ci == 0)
    def _():
        if has_init:
            s_scr[...] = s0_ref[...]
        else:
            s_scr[...] = jnp.zeros_like(s_scr)

    chains, sinks = [], []
    for d, (ins, o_ref) in enumerate(((fwd, of_ref), (bwd, ob_ref))):
        masks = _rwkv_masks(rev=(d == 1))
        for hp in range(npairs):
            sl = slice(hp * LANES, (hp + 1) * LANES)
            r, k, v, kk, a, lw = (x[:, sl] for x in ins)
            chains.append(_rwkv_chunk(r, k, v, kk, a, lw, ka_ref[:, sl], s_scr[d, hp],
                                      rev=(d == 1), masks=masks))
            sinks.append((o_ref, sl, d, hp))
    for (o, s_new), (o_ref, sl, d, hp) in zip(_lockstep(chains), sinks):
        o_ref[:, sl] = o
        s_scr[d, hp] = s_new

    if emit_final:
        @pl.when(ci == nchunk - 1)
        def _():
            sfin_ref[...] = s_scr[...]


def _rwkv_scan(pre, k_a, s0_bd, *, row_off, nb, t, emit_final, npairs, into=None):
    m, dr = pre["r"].shape
    ngroup = dr // (LANES * npairs)
    nchunk = t // CHUNK
    rb = row_off // CHUNK
    has_init = s0_bd is not None
    fmap = lambda b, h, c: (rb + b * nchunk + c, h)
    bmap = lambda b, h, c: (rb + b * nchunk + (nchunk - 1 - c), h)
    blk = (CHUNK, LANES * npairs)
    in_specs = ([pl.BlockSpec(blk, fmap)] * 6 + [pl.BlockSpec(blk, bmap)] * 6
                + [pl.BlockSpec((1, LANES * npairs), lambda b, h, c: (0, h))])
    args = [pre["r"], pre["k"], pre["v"], pre["kk"], pre["a_f"], pre["lw_f"],
            pre["r"], pre["k"], pre["v"], pre["kk"], pre["a_b"], pre["lw_b"], k_a]
    sblk = pl.BlockSpec((None, 2, npairs, LANES, LANES), lambda b, h, c: (b, 0, h, 0, 0))
    if has_init:
        in_specs.append(sblk)
        args.append(s0_bd)
    aliases = {}
    if into is not None:
        aliases = {len(args): 0, len(args) + 1: 1}
        in_specs += [pl.BlockSpec(memory_space=pl.ANY)] * 2
        args += list(into)
    out_shape = [jax.ShapeDtypeStruct((m, dr), F32)] * 2
    out_specs = [pl.BlockSpec(blk, fmap), pl.BlockSpec(blk, bmap)]
    if emit_final:
        out_shape.append(jax.ShapeDtypeStruct((nb, 2, dr // LANES, LANES, LANES), F32))
        out_specs.append(sblk)
    kern = functools.partial(_rwkv_scan_kernel, has_init=has_init, emit_final=emit_final,
                             nchunk=nchunk, npairs=npairs, aliased=into is not None)
    return pl.pallas_call(
        kern, out_shape=out_shape, grid=(nb, ngroup, nchunk), in_specs=in_specs,
        out_specs=out_specs, scratch_shapes=[pltpu.VMEM((2, npairs, LANES, LANES), F32)],
        input_output_aliases=aliases,
        compiler_params=_cparams("arbitrary", "arbitrary", "arbitrary"), name="rwkv_scan",
    )(*args)


def _rwkv_post_kernel(of_ref, ob_ref, bonus_ref, g_ref, lnw_ref, lnb_ref, y_ref, *, dr):
    bd = _head_ones(LANES, RWKV_HEAD)
    inv = 1.0 / RWKV_HEAD
    for j in range(dr // LANES):
        sl = slice(j * LANES, (j + 1) * LANES)
        o = of_ref[:, sl] + ob_ref[:, sl]
        mean = _group_sum(o, bd) * inv
        oc = o - mean
        var = _group_sum(oc * oc, bd) * inv
        on = oc * lax.rsqrt(var + GN_EPS) * lnw_ref[:, sl] + lnb_ref[:, sl]
        y_ref[:, sl] = ((on + bonus_ref[:, sl]) * g_ref[:, sl]).astype(y_ref.dtype)


def _rwkv_post(o_f, o_b, pre, p, *, tm):
    rows, dr = o_f.shape
    own = pl.BlockSpec((tm, dr), lambda i: (i, 0))
    par = pl.BlockSpec((1, dr), lambda i: (0, 0))
    return pl.pallas_call(
        functools.partial(_rwkv_post_kernel, dr=dr),
        out_shape=jax.ShapeDtypeStruct((rows, dr), BF16),
        grid=(rows // tm,),
        in_specs=[own] * 4 + [par] * 2,
        out_specs=own,
        compiler_params=_cparams("arbitrary"), name="rwkv_post",
    )(o_f, o_b, pre["bonus"], pre["g"], p["ln_w"], p["ln_b"])


HGRN_LEVELS = 6


SUBLANES = 8


def _boundary_rows(b, lvl, rev, sub):
    c = CHUNK
    half = 1 << lvl
    blk = 2 * half
    src = (lambda base: base + half) if rev else (lambda base: base + half - 1)

    def rows(starts, nrows):
        return jnp.concatenate(
            [jnp.broadcast_to(b[src(s):src(s) + 1, :], (nrows, LANES)) for s in starts], axis=0)

    if blk >= SUBLANES:
        return rows(range(0, c, blk), blk)
    out = None
    for k in range(SUBLANES // blk):
        arr = rows(range(k * blk, c, SUBLANES), SUBLANES)
        out = arr if out is None else jnp.where(sub >= k * blk, arr, out)
    return out


def _hgrn_masks(rev):
    c = CHUNK
    ti = lax.broadcasted_iota(jnp.int32, (c, c), 0)
    tj = lax.broadcasted_iota(jnp.int32, (c, c), 1)
    late, early = (tj, ti) if rev else (ti, tj)
    sels = [((late >> lvl) == (early >> lvl) + 1) & (((early >> lvl) & 1) == 0)
            for lvl in range(HGRN_LEVELS)]
    cum = jnp.where((tj >= ti) if rev else (tj <= ti), 1.0, 0.0).astype(BF16)
    sub = lax.broadcasted_iota(jnp.int32, (c, LANES), 0) & (SUBLANES - 1)
    return jnp.concatenate([cum, cum, cum], axis=1), ti == tj, sels, sub


def _hgrn_chunk(q_pre, f_pre, v, lb, st, rev, masks):
    c = CHUNK
    cum3, diag, sels, sub = masks
    q = q_pre * jax.nn.sigmoid(q_pre)
    f = lb + (1.0 - lb) * jax.nn.sigmoid(f_pre)
    kf = 1.0 - f
    b = _exact_rowmix(cum3, jnp.log(f))
    yield
    btot = b[0:1, :] if rev else b[c - 1:c, :]
    q_b, kf_b = q.astype(BF16), kf.astype(BF16)
    attn = jnp.where(diag, _dot_nt(q_b, kf_b), 0.0)
    for lvl in range(HGRN_LEVELS):
        dq = b - _boundary_rows(b, lvl, rev, sub)
        wgt = jnp.exp(-jnp.abs(dq)).astype(BF16)
        attn = jnp.where(sels[lvl], _dot_nt(q_b * wgt, kf_b * wgt), attn)
    yield
    o = _dot_nt(q * jnp.exp(b), st) + _dot(attn, v)
    st_new = st * jnp.exp(btot) + _dot_tn(v, kf * jnp.exp(btot - b))
    return o, st_new


def _hgrn_scan_kernel(*refs, has_init, emit_final, nsteps, nheads, cols, aliased):
    it = iter(refs)
    qf_ref, ff_ref, vf_ref = next(it), next(it), next(it)
    qb_ref, fb_ref, vb_ref = next(it), next(it), next(it)
    lb_ref = next(it)
    s0_ref = next(it) if has_init else None
    if aliased:
        next(it), next(it)
    of_ref, ob_ref = next(it), next(it)
    sfin_ref = next(it) if emit_final else None
    s_scr = next(it)
    in_scr, out_scr = (next(it), next(it)) if cols is not None else (None, None)
    ci = pl.program_id(2)

    @pl.when(ci == 0)
    def _():
        if has_init:
            for d in range(2):
                for h in range(nheads):
                    s_scr[d, h] = s0_ref[d, h].T
        else:
            s_scr[...] = jnp.zeros_like(s_scr)

    ins = ((qf_ref, ff_ref, vf_ref), (qb_ref, fb_ref, vb_ref))
    outs = (of_ref, ob_ref)
    all_masks = [_hgrn_masks(rev=False), _hgrn_masks(rev=True)]

    def advance(load, store):
        chains, sinks = [], []
        for d in range(2):
            for h in range(nheads):
                sl = slice(h * LANES, (h + 1) * LANES)
                qp, fp, v = (load(d, k, sl) for k in range(3))
                chains.append(_hgrn_chunk(qp, fp, v, lb_ref[:, sl], s_scr[d, h], rev=(d == 1),
                                          masks=all_masks[d]))
                sinks.append((d, h, sl))
        for (o, s_new), (d, h, sl) in zip(_lockstep(chains), sinks):
            store(d, sl, o)
            s_scr[d, h] = s_new

    if cols is None:
        def store_rows(d, sl, o):
            outs[d][:, sl] = o
        advance(lambda d, k, sl: ins[d][k][:, sl], store_rows)
    else:
        for d in range(2):
            for k in range(3):
                in_scr[d, k] = jnp.swapaxes(ins[d][k][...], 0, 1)

        def body(j, carry):
            col = (j, cols - 1 - j)

            def store_col(d, sl, o):
                out_scr[d, col[d], :, sl] = o
            advance(lambda d, k, sl: in_scr[d, k, col[d], :, sl], store_col)
            return carry
        lax.fori_loop(0, cols, body, 0)
        for d in range(2):
            outs[d][...] = jnp.swapaxes(out_scr[d], 0, 1)

    if emit_final:
        @pl.when(ci == nsteps - 1)
        def _():
            for d in range(2):
                for h in range(nheads):
                    sfin_ref[d, h] = s_scr[d, h].T


HGRN_COLS = 8


def _hgrn_scan(zhs, lb, s0, *, row_off, nb, t, dg, emit_final, nheads, column_major,
               into=None):
    m = zhs.shape[0]
    ng = dg // (LANES * nheads)
    has_init = s0 is not None
    lanes = LANES * nheads
    if column_major:
        nrow = t // GRID_W
        assert nrow == CHUNK and m % t == 0 and row_off % t == 0
        nsteps = GRID_W // HGRN_COLS
        seq0 = row_off // t
        zsrc = zhs.reshape(m // t, nrow, GRID_W, 4 * dg)
        blk = (None, nrow, HGRN_COLS, lanes)
        fwd = lambda g: (lambda b, h, c: (b + seq0, 0, c, g * ng + h))
        bwd = lambda g: (lambda b, h, c: (b + seq0, 0, nsteps - 1 - c, g * ng + h))
        out_dims = (m // t, nrow, GRID_W, dg)
        ofwd = lambda b, h, c: (b + seq0, 0, c, h)
        obwd = lambda b, h, c: (b + seq0, 0, nsteps - 1 - c, h)
        cols = HGRN_COLS
    else:
        nsteps = t // CHUNK
        rb = row_off // CHUNK
        zsrc = zhs
        blk = (CHUNK, lanes)
        fwd = lambda g: (lambda b, h, c: (rb + b * nsteps + c, g * ng + h))
        bwd = lambda g: (lambda b, h, c: (rb + b * nsteps + nsteps - 1 - c, g * ng + h))
        out_dims = (m, dg)
        ofwd = lambda b, h, c: (rb + b * nsteps + c, h)
        obwd = lambda b, h, c: (rb + b * nsteps + nsteps - 1 - c, h)
        cols = None

    in_specs = [pl.BlockSpec(blk, fwd(0)), pl.BlockSpec(blk, fwd(1)), pl.BlockSpec(blk, fwd(3)),
                pl.BlockSpec(blk, bwd(0)), pl.BlockSpec(blk, bwd(2)), pl.BlockSpec(blk, bwd(3)),
                pl.BlockSpec((1, lanes), lambda b, h, c: (0, h))]
    args = [zsrc] * 6 + [lb]
    sblk = pl.BlockSpec((None, 2, nheads, LANES, LANES), lambda b, h, c: (b, 0, h, 0, 0))
    if has_init:
        in_specs.append(sblk)
        args.append(s0)
    aliases = {}
    if into is not None:
        aliases = {len(args): 0, len(args) + 1: 1}
        in_specs += [pl.BlockSpec(memory_space=pl.ANY)] * 2
        args += [o.reshape(out_dims) for o in into]
    out_shape = [jax.ShapeDtypeStruct(out_dims, F32)] * 2
    out_specs = [pl.BlockSpec(blk, ofwd), pl.BlockSpec(blk, obwd)]
    if emit_final:
        out_shape.append(jax.ShapeDtypeStruct((nb, 2, dg // LANES, LANES, LANES), F32))
        out_specs.append(sblk)
    kern = functools.partial(_hgrn_scan_kernel, has_init=has_init, emit_final=emit_final,
                             nsteps=nsteps, nheads=nheads, cols=cols, aliased=into is not None)
    scratch = [pltpu.VMEM((2, nheads, LANES, LANES), F32)]
    if column_major:
        scratch += [pltpu.VMEM((2, 3, cols, CHUNK, lanes), F32),
                    pltpu.VMEM((2, cols, CHUNK, lanes), F32)]
    outs = pl.pallas_call(
        kern, out_shape=out_shape, grid=(nb, ng, nsteps), in_specs=in_specs, out_specs=out_specs,
        scratch_shapes=scratch, input_output_aliases=aliases,
        compiler_params=_cparams("arbitrary", "arbitrary", "arbitrary"), name="hgrn_scan",
    )(*args)
    return [outs[0].reshape(m, dg), outs[1].reshape(m, dg)] + list(outs[2:])


def _hgrn_post_kernel(of_ref, ob_ref, gate_ref, gain_ref, y_ref, *, dg):
    gain = gain_ref[...]
    for j in range(dg // LANES):
        sl = slice(j * LANES, (j + 1) * LANES)
        o = of_ref[:, sl] + ob_ref[:, sl]
        on = o * lax.rsqrt(jnp.mean(o * o, axis=-1, keepdims=True) + RMS_EPS) * gain
        gt = gate_ref[:, sl]
        y_ref[:, sl] = (on * (gt * jax.nn.sigmoid(gt))).astype(y_ref.dtype)


def _hgrn_post(o_f, o_b, z_gates, gain, *, gate_col, tm):
    rows_all, dg = o_f.shape
    kern = functools.partial(_hgrn_post_kernel, dg=dg)
    gcb = gate_col // dg
    par = pl.BlockSpec((1, LANES), lambda *_: (0, 0))
    ospec = pl.BlockSpec((tm, dg), lambda i: (i, 0))
    return pl.pallas_call(
        kern, out_shape=jax.ShapeDtypeStruct((rows_all, dg), BF16), grid=(rows_all // tm,),
        in_specs=[ospec, ospec, pl.BlockSpec((tm, dg), lambda i: (i, gcb)), par],
        out_specs=ospec,
        compiler_params=_cparams("arbitrary"), name="hgrn_post",
    )(o_f, o_b, z_gates, gain)


def _largest_divisor(n, candidates):
    for cand in candidates:
        if n % cand == 0:
            return cand
    raise ValueError(f"no tile in {candidates} divides {n}")


def _pad_to(x, axis, size):
    pad = size - x.shape[axis]
    if pad == 0:
        return x
    widths = [(0, 0)] * x.ndim
    widths[axis] = (0, pad)
    return jnp.pad(x, widths)


def _pair_state_to_bd(s):
    b, two, h, n, _ = s.shape
    s = s.reshape(b, two, h // 2, 2, n, n)
    z = jnp.zeros_like(s[:, :, :, 0])
    top = jnp.concatenate([s[:, :, :, 0], z], axis=-1)
    bot = jnp.concatenate([z, s[:, :, :, 1]], axis=-1)
    return jnp.concatenate([top, bot], axis=-2)


def _bd_to_pair_state(sbd):
    b, two, hp, n2, _ = sbd.shape
    n = n2 // 2
    s0 = sbd[:, :, :, :n, :n]
    s1 = sbd[:, :, :, n:, n:]
    return jnp.stack([s0, s1], axis=3).reshape(b, two, hp * 2, n, n)


def kernel(x_prompt, x_sample, state_rwkv, state_hgrn, c, c_ctx, w_mod, b_mod, norm_ffn1, norm_mix, norm_ffn2, ffn1_w_in, ffn1_w_out, ffn2_w_in, ffn2_w_out, w_in, rwkv_mu_prev, rwkv_mu_next, rwkv_w0, rwkv_w2, rwkv_a0, rwkv_a2, rwkv_g2, rwkv_k_k, rwkv_k_a, rwkv_r_k, rwkv_ln_w, rwkv_ln_b, hgrn_lb_logits, hgrn_norm, w_branch_rwkv, w_branch_hgrn, w_out, norm_final):
    nbc, tc, d = x_prompt.shape
    nbl, tl, _ = x_sample.shape
    depth = w_mod.shape[0]
    mc, ml = nbc * tc, nbl * tl
    m = mc + ml
    dims = (mc, tc, tl, nbl)
    dr = rwkv_w0.shape[-1]
    dg = w_branch_hgrn.shape[1]
    dff = ffn1_w_out.shape[1]
    wl, al, gl = rwkv_w2.shape[2], rwkv_a2.shape[2], rwkv_g2.shape[1]
    glp = -(-gl // LANES) * LANES
    rw_cols = 3 * dr + 2 * wl + 2 * al + gl
    zr_cols = 3 * dr + 2 * wl + 2 * al + glp
    assert nbl < MOD_ROWS and tl % GRID_W == 0 and tl // GRID_W == CHUNK and tc % CHUNK == 0

    tm_norm = min(256, tc)
    tm_proj = min(1024, mc)
    tm_merge = min(512, mc)
    tm_post = min(512, mc)
    tn_of = lambda n: _largest_divisor(n, (1024, 512, 256, 128))
    tf = _largest_divisor(dff, (256, 128))

    lb_all = jnp.cumsum(jax.nn.softmax(hgrn_lb_logits.astype(F32), axis=0), axis=0)
    cond = jnp.concatenate([c, c_ctx[None, :], jnp.zeros((MOD_ROWS - nbl - 1, d), F32)], axis=0)

    x = (x_prompt.reshape(mc, d), x_sample.reshape(ml, d))
    new_r, new_h = [], []
    for l in range(depth):
        mod3 = _modulation(cond, w_mod[l], b_mod[l], tn_of(N_MOD * d)).reshape(MOD_ROWS, N_MOD, d)

        h = _resnorm(x, None, mod3, norm_ffn1[l], dims, res_scale=None, gate_idx=None, mod_idx=0,
                     emit_x=False, out_dtype=BF16, tm=tm_norm)[0]
        delta = _ffn(h, ffn1_w_in[l].astype(BF16), ffn1_w_out[l].astype(BF16), tm=tm_proj, tf=tf)

        x, h = _resnorm(x, delta, mod3, norm_mix[l], dims, res_scale=FFN_HALF, gate_idx=2,
                        mod_idx=3, emit_x=True, out_dtype=BF16, tm=tm_norm)
        wi = w_in[l]
        hg0 = rw_cols
        w_r = _pad_to(wi[:, :rw_cols], 1, zr_cols).astype(BF16)
        w_hs = wi[:, hg0:hg0 + 4 * dg].astype(BF16)
        w_gates = wi[:, hg0 + 4 * dg:].astype(BF16)
        col_gate, col_gr, col_gh = 0, dg, dg + d
        z_r = _matmul(h, w_r, out_dtype=F32, tm=tm_proj, tn=tn_of(zr_cols), name="proj_rwkv")
        z_g = _matmul(h, w_gates, out_dtype=F32, tm=tm_proj, tn=tn_of(dg), name="proj_gates")
        zhs = _matmul(h, w_hs, out_dtype=F32, tm=tm_proj, tn=tn_of(dg), name="proj_hgrn")

        p = {
            "mu_prev": _pad_to(rwkv_mu_prev[l][None, :], 1, zr_cols),
            "mu_next": _pad_to(rwkv_mu_next[l][None, :], 1, zr_cols),
            "w0": rwkv_w0[l], "w2": rwkv_w2[l].astype(BF16),
            "a0": rwkv_a0[l], "a2": rwkv_a2[l].astype(BF16),
            "g2": _pad_to(rwkv_g2[l], 0, glp).astype(BF16),
            "k_k": rwkv_k_k[l][None, :], "k_a": rwkv_k_a[l][None, :],
            "r_k": rwkv_r_k[l].reshape(1, dr),
            "ln_w": rwkv_ln_w[l][None, :], "ln_b": rwkv_ln_b[l][None, :],
        }
        names = ("r", "k", "v", "kk", "a_f", "a_b", "lw_f", "lw_b", "g", "bonus")
        pre = dict(zip(names, _rwkv_prep(z_r, p, dims, tm=min(128, tc), zr_cols=zr_cols, dr=dr,
                                         wl=wl, al=al, gl=glp)))
        npairs = _largest_divisor(dr // LANES, (16, 8, 4, 2, 1))
        of_c, ob_c, sr_c = _rwkv_scan(pre, p["k_a"], None, row_off=0, nb=nbc, t=tc,
                                      emit_final=True, npairs=npairs)
        o_f, o_b = _rwkv_scan(pre, p["k_a"], _pair_state_to_bd(state_rwkv[:, l].astype(F32)),
                              row_off=mc, nb=nbl, t=tl, emit_final=False, npairs=npairs,
                              into=(of_c, ob_c))
        y_r = _rwkv_post(o_f, o_b, pre, p, tm=tm_post)
        new_r.append(_bd_to_pair_state(sr_c))

        lb = lb_all[l][None, :]
        gain_h = hgrn_norm[l][None, :]
        nheads = _largest_divisor(dg // LANES, (8, 4, 2, 1))
        nheads_cols = _largest_divisor(dg // LANES, (4, 2, 1))
        hf_c, hb_c, sh_c = _hgrn_scan(zhs, lb, None, row_off=0, nb=nbc, t=tc, dg=dg,
                                      emit_final=True, nheads=nheads, column_major=False)
        h_f, h_b = _hgrn_scan(zhs, lb, state_hgrn[:, l].astype(F32), row_off=mc, nb=nbl, t=tl,
                              dg=dg, emit_final=False, nheads=nheads_cols, column_major=True,
                              into=(hf_c, hb_c))
        y_h = _hgrn_post(h_f, h_b, z_g, gain_h, gate_col=col_gate, tm=tm_post)
        new_h.append(sh_c)

        merged = _merge(y_r, y_h, w_branch_rwkv[l].astype(BF16), w_branch_hgrn[l].astype(BF16),
                        z_g, col_gr, col_gh, tm=tm_merge, tn=tn_of(dg))
        delta = _matmul(merged, w_out[l].astype(BF16), out_dtype=F32, tm=tm_proj, tn=tn_of(d),
                        name="mix_out")

        x, h = _resnorm(x, delta, mod3, norm_ffn2[l], dims, res_scale=1.0, gate_idx=5, mod_idx=6,
                        emit_x=True, out_dtype=BF16, tm=tm_norm)
        delta = _ffn(h, ffn2_w_in[l].astype(BF16), ffn2_w_out[l].astype(BF16), tm=tm_proj, tf=tf)
        last = l == depth - 1
        if not last:
            x = _resnorm(x, delta, mod3, norm_ffn2[l], dims, res_scale=FFN_HALF, gate_idx=8,
                         mod_idx=None, emit_x=True, out_dtype=F32, tm=tm_norm)[0]

    y_c = _resnorm(x, delta, mod3, norm_final, dims, res_scale=FFN_HALF, gate_idx=8, mod_idx=None,
                   emit_x=False, out_dtype=F32, tm=tm_norm, row_off=0, rows=mc)[0]
    y_l = _resnorm(x, delta, mod3, norm_final, dims, res_scale=FFN_HALF, gate_idx=8, mod_idx=None,
                   emit_x=False, out_dtype=F32, tm=tm_norm, row_off=mc, rows=ml)[0]
    return (y_c.reshape(nbc, tc, d), y_l.reshape(nbl, tl, d),
            jnp.stack(new_r, axis=1), jnp.stack(new_h, axis=1))
```

```python
import functools
import math

import jax
import jax.numpy as jnp
from jax import lax
from jax.experimental import pallas as pl
from jax.experimental.pallas import tpu as pltpu

F32 = jnp.float32
BF16 = jnp.bfloat16

CHUNK = 64
GRID_W = 64
RWKV_HEAD = 64
N_MOD = 9
RMS_EPS = 1e-6
GN_EPS = 64e-5
FFN_HALF = 0.5
DECAY_SCALE = math.exp(-0.5)
LANES = 128
MOD_ROWS = 16
VMEM_LIMIT_BYTES = 56 * 1024 * 1024


def _cparams(*sem):
    return pltpu.CompilerParams(dimension_semantics=sem, vmem_limit_bytes=VMEM_LIMIT_BYTES)


def _dot(a, b):
    return jnp.dot(a.astype(BF16), b.astype(BF16), preferred_element_type=F32)


def _dot_nt(a, b):
    return lax.dot_general(a.astype(BF16), b.astype(BF16), (((1,), (1,)), ((), ())),
                           preferred_element_type=F32)


def _dot_tn(a, b):
    return jnp.dot(a.T.astype(BF16), b.astype(BF16), preferred_element_type=F32)


def _split2(x):
    hi = x.astype(BF16)
    return hi, (x - hi.astype(F32)).astype(BF16)


def _split3(x):
    x1 = x.astype(BF16)
    d1 = x - x1.astype(F32)
    x2 = d1.astype(BF16)
    x3 = (d1 - x2.astype(F32)).astype(BF16)
    return x1, x2, x3


def _exact_rowmix(mat3, x):
    return jnp.dot(mat3, jnp.concatenate(_split3(x), axis=0), preferred_element_type=F32)


def _group_sum(x, bd_ones):
    x1, x2 = _split2(x)
    return (jnp.dot(x1, bd_ones, preferred_element_type=F32)
            + jnp.dot(x2, bd_ones, preferred_element_type=F32))


def _lockstep(chains):
    results = [None] * len(chains)
    active = list(enumerate(chains))
    while active:
        still = []
        for idx, chain in active:
            try:
                next(chain)
                still.append((idx, chain))
            except StopIteration as done:
                results[idx] = done.value
        active = still
    return results


def _mod_kernel(c_ref, w_ref, b_ref, o_ref):
    c = c_ref[...]
    s = c * jax.nn.sigmoid(c)
    o_ref[...] = _dot(s, w_ref[...]) + b_ref[...]


def _modulation(cond, w_mod, b_mod, tn):
    d, n = w_mod.shape
    return pl.pallas_call(
        _mod_kernel,
        out_shape=jax.ShapeDtypeStruct((MOD_ROWS, n), F32),
        grid=(n // tn,),
        in_specs=[pl.BlockSpec((MOD_ROWS, d), lambda j: (0, 0)),
                  pl.BlockSpec((d, tn), lambda j: (0, j)),
                  pl.BlockSpec((1, tn), lambda j: (0, j))],
        out_specs=pl.BlockSpec((MOD_ROWS, tn), lambda j: (0, j)),
        compiler_params=_cparams("arbitrary"),
        name="modulation",
    )(cond, w_mod, b_mod.reshape(1, n))


def _resnorm_kernel(*refs, has_delta, res_scale, gate_idx, mod_idx, emit_x, ctx_tiles):
    it = iter(refs)
    x_ref = next(it)
    xl_ref = next(it) if ctx_tiles is not None else None
    delta_ref = next(it) if has_delta else None
    mod_ref = next(it) if (has_delta or mod_idx is not None) else None
    gain_ref = next(it)
    xo_ref = next(it) if emit_x else None
    h_ref = next(it)
    x = x_ref[...]
    if ctx_tiles is not None:
        x = jnp.where(pl.program_id(0) < ctx_tiles, x, xl_ref[...])
    if has_delta:
        g = mod_ref[gate_idx:gate_idx + 1, :]
        x = x + g * (res_scale * delta_ref[...])
    if emit_x:
        xo_ref[...] = x
    y = x * lax.rsqrt(jnp.mean(x * x, axis=-1, keepdims=True) + RMS_EPS)
    y = y * gain_ref[...]
    if mod_idx is not None:
        sh = mod_ref[mod_idx:mod_idx + 1, :]
        sc = mod_ref[mod_idx + 1:mod_idx + 2, :]
        y = y * (1.0 + sc) + sh
    h_ref[...] = y.astype(h_ref.dtype)


def _resnorm(x, delta, mod3, gain, dims, *, res_scale, gate_idx, mod_idx, emit_x, out_dtype,
             tm, row_off=0, rows=None):
    mc, tc, tl, nlb = dims
    n_ctx_tiles = mc // tm
    tiles_per_lat = tl // tm
    ro = row_off // tm
    split_x = isinstance(x, tuple)
    if split_x:
        assert row_off == 0 and rows is None
        d = x[0].shape[1]
        m_all = x[0].shape[0] + x[1].shape[0]
        in_specs = [pl.BlockSpec((tm, d), lambda i: (jnp.minimum(i, n_ctx_tiles - 1), 0)),
                    pl.BlockSpec((tm, d), lambda i: (jnp.maximum(i - n_ctx_tiles, 0), 0))]
        args = list(x)
    else:
        d = x.shape[1]
        m_all = rows if rows is not None else x.shape[0]
        in_specs = [pl.BlockSpec((tm, d), lambda i: (i + ro, 0))]
        args = [x]

    def mod_row(i):
        gi = i + ro
        return jnp.where(gi < n_ctx_tiles, nlb, (gi - n_ctx_tiles) // tiles_per_lat)

    has_delta = delta is not None
    if has_delta:
        in_specs.append(pl.BlockSpec((tm, d), lambda i: (i + ro, 0)))
        args.append(delta)
    if has_delta or mod_idx is not None:
        in_specs.append(pl.BlockSpec((None, N_MOD, d), lambda i: (mod_row(i), 0, 0)))
        args.append(mod3)
    in_specs.append(pl.BlockSpec((1, d), lambda i: (0, 0)))
    args.append(gain.reshape(1, d))
    out_shape, out_specs = [], []
    if emit_x:
        out_shape.append(jax.ShapeDtypeStruct((m_all, d), F32))
        out_specs.append(pl.BlockSpec((tm, d), lambda i: (i, 0)))
    out_shape.append(jax.ShapeDtypeStruct((m_all, d), out_dtype))
    out_specs.append(pl.BlockSpec((tm, d), lambda i: (i, 0)))
    kern = functools.partial(_resnorm_kernel, has_delta=has_delta, res_scale=res_scale,
                             gate_idx=gate_idx, mod_idx=mod_idx, emit_x=emit_x,
                             ctx_tiles=n_ctx_tiles if split_x else None)
    return pl.pallas_call(
        kern, out_shape=out_shape, grid=(m_all // tm,), in_specs=in_specs, out_specs=out_specs,
        compiler_params=_cparams("arbitrary"), name="resnorm",
    )(*args)


def _mm_kernel(a_ref, w_ref, o_ref):
    o_ref[...] = jnp.dot(a_ref[...], w_ref[...], preferred_element_type=F32).astype(o_ref.dtype)


def _matmul(a, w, *, out_dtype, tm, tn, row_off=0, rows=None, name="matmul"):
    k = a.shape[1]
    n = w.shape[1]
    rows = a.shape[0] if rows is None else rows
    ro = row_off // tm
    return pl.pallas_call(
        _mm_kernel,
        out_shape=jax.ShapeDtypeStruct((rows, n), out_dtype),
        grid=(rows // tm, n // tn),
        in_specs=[pl.BlockSpec((tm, k), lambda i, j: (i + ro, 0)),
                  pl.BlockSpec((k, tn), lambda i, j: (0, j))],
        out_specs=pl.BlockSpec((tm, tn), lambda i, j: (i, j)),
        compiler_params=_cparams("arbitrary", "arbitrary"), name=name,
    )(a, w)


def _ffn_kernel(h_ref, wa_ref, wb_ref, wo_ref, o_ref):
    @pl.when(pl.program_id(1) == 0)
    def _():
        o_ref[...] = jnp.zeros_like(o_ref)

    h = h_ref[...]
    a = jnp.dot(h, wa_ref[...], preferred_element_type=F32)
    b = jnp.dot(h, wb_ref[...], preferred_element_type=F32)
    u = (a * jax.nn.sigmoid(a) * b).astype(BF16)
    o_ref[...] += jnp.dot(u, wo_ref[...], preferred_element_type=F32)


def _ffn(h, w_in, w_out, *, tm, tf):
    m, d = h.shape
    dff = w_out.shape[0]
    nf = dff // tf
    return pl.pallas_call(
        _ffn_kernel,
        out_shape=jax.ShapeDtypeStruct((m, d), F32),
        grid=(m // tm, nf),
        in_specs=[pl.BlockSpec((tm, d), lambda i, j: (i, 0)),
                  pl.BlockSpec((d, tf), lambda i, j: (0, j)),
                  pl.BlockSpec((d, tf), lambda i, j: (0, j + nf)),
                  pl.BlockSpec((tf, d), lambda i, j: (j, 0))],
        out_specs=pl.BlockSpec((tm, d), lambda i, j: (i, 0), pipeline_mode=pl.Buffered(1)),
        compiler_params=_cparams("arbitrary", "arbitrary"), name="ffn",
    )(h, w_in, w_in, w_out)


def _merge_kernel(yr_ref, yh_ref, wr_ref, wh_ref, gr_ref, gh_ref, o_ref):
    pr = jnp.dot(yr_ref[...], wr_ref[...], preferred_element_type=F32)
    ph = jnp.dot(yh_ref[...], wh_ref[...], preferred_element_type=F32)
    o_ref[...] = (jax.nn.sigmoid(gr_ref[...]) * pr
                  + jax.nn.sigmoid(gh_ref[...]) * ph).astype(o_ref.dtype)


def _merge(y_r, y_h, w_br, w_bh, z_rest, col_gr, col_gh, *, tm, tn):
    m, kr = y_r.shape
    kh = y_h.shape[1]
    d = w_br.shape[1]
    cgr, cgh = col_gr // tn, col_gh // tn
    return pl.pallas_call(
        _merge_kernel,
        out_shape=jax.ShapeDtypeStruct((m, d), BF16),
        grid=(m // tm, d // tn),
        in_specs=[pl.BlockSpec((tm, kr), lambda i, j: (i, 0)),
                  pl.BlockSpec((tm, kh), lambda i, j: (i, 0)),
                  pl.BlockSpec((kr, tn), lambda i, j: (0, j)),
                  pl.BlockSpec((kh, tn), lambda i, j: (0, j)),
                  pl.BlockSpec((tm, tn), lambda i, j: (i, j + cgr)),
                  pl.BlockSpec((tm, tn), lambda i, j: (i, j + cgh))],
        out_specs=pl.BlockSpec((tm, tn), lambda i, j: (i, j)),
        compiler_params=_cparams("arbitrary", "arbitrary"), name="merge",
    )(y_r, y_h, w_br, w_bh, z_rest, z_rest)


def _log2(n):
    assert n & (n - 1) == 0
    return n.bit_length() - 1


def _head_ones(n, group):
    sh = _log2(group)
    i = lax.broadcasted_iota(jnp.int32, (n, n), 0) >> sh
    j = lax.broadcasted_iota(jnp.int32, (n, n), 1) >> sh
    return jnp.where(i == j, 1.0, 0.0).astype(BF16)


def _rwkv_prep_kernel(z_ref, zp_ref, zn_ref, mup_ref, mun_ref, w0_ref, w2_ref, a0_ref, a2_ref,
                      g2_ref, kk_ref, ka_ref, rk_ref,
                      r_o, k_o, v_o, kkn_o, af_o, ab_o, lwf_o, lwb_o, g_o, bonus_o,
                      *, tm, mc, tc, tl, dr, wl, al, gl):
    row0 = pl.program_id(0) * tm
    in_ctx = row0 < mc
    pos = jnp.where(in_ctx, row0 % tc, (row0 - mc) % tl)
    seqlen = jnp.where(in_ctx, tc, tl)
    keep_prev = jnp.where(pos == 0, 0.0, 1.0)
    keep_next = jnp.where(pos + tm == seqlen, 0.0, 1.0)
    rows = lax.broadcasted_iota(jnp.int32, (tm, 1), 0)

    def shifted(c0, c1):
        z = z_ref[:, c0:c1]
        zp = jnp.where(rows == 0, keep_prev * zp_ref[7:8, c0:c1], pltpu.roll(z, 1, 0))
        zn = jnp.where(rows == tm - 1, keep_next * zn_ref[0:1, c0:c1], pltpu.roll(z, tm - 1, 0))
        return z + mup_ref[:, c0:c1] * (zp - z) + mun_ref[:, c0:c1] * (zn - z)

    bd = _head_ones(LANES, RWKV_HEAD)
    for j in range(dr // LANES):
        c0 = j * LANES
        r_o[:, c0:c0 + LANES] = shifted(c0, c0 + LANES)
        k = shifted(dr + c0, dr + c0 + LANES)
        k_o[:, c0:c0 + LANES] = k
        v_o[:, c0:c0 + LANES] = shifted(2 * dr + c0, 2 * dr + c0 + LANES)
        kk = k * kk_ref[:, c0:c0 + LANES]
        ss = _group_sum(kk * kk, bd)
        kkn_o[:, c0:c0 + LANES] = kk * lax.rsqrt(jnp.maximum(ss, 1e-24))

    c = 3 * dr
    for d, (lw_o, a_o) in enumerate(((lwf_o, af_o), (lwb_o, ab_o))):
        wd = shifted(c + d * wl, c + (d + 1) * wl)
        ad = shifted(c + 2 * wl + d * al, c + 2 * wl + (d + 1) * al)
        wpre = w0_ref[d:d + 1, :] + _dot(jnp.tanh(wd), w2_ref[d])
        lw_o[...] = -DECAY_SCALE * jax.nn.sigmoid(wpre)
        a_o[...] = jax.nn.sigmoid(a0_ref[d:d + 1, :] + _dot(ad, a2_ref[d]))
    gd = shifted(c + 2 * wl + 2 * al, c + 2 * wl + 2 * al + gl)
    g_o[...] = _dot(jax.nn.sigmoid(gd), g2_ref[...])

    for j in range(dr // LANES):
        sl = slice(j * LANES, (j + 1) * LANES)
        r, k, v = r_o[:, sl], k_o[:, sl], v_o[:, sl]
        bonus = 0.0
        for a_o in (af_o, ab_o):
            kd = k * (1.0 + (a_o[:, sl] - 1.0) * ka_ref[:, sl])
            bonus = bonus + _group_sum(r * kd * rk_ref[:, sl], bd) * v
        bonus_o[:, sl] = bonus


def _rwkv_prep(z_rest, p, dims, *, tm, zr_cols, dr, wl, al, gl):
    mc, tc, tl, nlb = dims
    m = z_rest.shape[0]
    nblk8 = m // 8
    t8 = tm // 8
    full = lambda shape: pl.BlockSpec(shape, lambda i: (0,) * len(shape))
    outs = [jax.ShapeDtypeStruct((m, dr), F32)] * 10
    kern = functools.partial(_rwkv_prep_kernel, tm=tm, mc=mc, tc=tc, tl=tl, dr=dr, wl=wl, al=al,
                             gl=gl)
    return pl.pallas_call(
        kern, out_shape=outs, grid=(m // tm,),
        in_specs=[pl.BlockSpec((tm, zr_cols), lambda i: (i, 0)),
                  pl.BlockSpec((8, zr_cols), lambda i: (jnp.maximum(i * t8 - 1, 0), 0)),
                  pl.BlockSpec((8, zr_cols), lambda i: (jnp.minimum((i + 1) * t8, nblk8 - 1), 0)),
                  full((1, zr_cols)), full((1, zr_cols)),
                  full((2, dr)), full((2, wl, dr)), full((2, dr)), full((2, al, dr)),
                  full((gl, dr)), full((1, dr)), full((1, dr)), full((1, dr))],
        out_specs=[pl.BlockSpec((tm, dr), lambda i: (i, 0))] * 10,
        compiler_params=_cparams("arbitrary"), name="rwkv_prep",
    )(z_rest, z_rest, z_rest, p["mu_prev"], p["mu_next"], p["w0"], p["w2"], p["a0"], p["a2"],
      p["g2"], p["k_k"], p["k_a"], p["r_k"])


def _stack_heads(x, lo):
    return jnp.concatenate([jnp.where(lo, x, 0.0), jnp.where(lo, 0.0, x)], axis=0)


SUB_BLOCK = 16
assert CHUNK // SUB_BLOCK == 4
assert CHUNK == RWKV_HEAD


def _rwkv_masks(rev):
    c = CHUNK
    n2 = 2 * c
    ti = lax.broadcasted_iota(jnp.int32, (c, c), 0)
    tj = lax.broadcasted_iota(jnp.int32, (c, c), 1)
    cum = jnp.where((tj >= ti) if rev else (tj <= ti), 1.0, 0.0).astype(BF16)
    cum = jnp.concatenate([cum, cum, cum], axis=1)
    row = lax.broadcasted_iota(jnp.int32, (c, LANES), 0)
    lane = lax.broadcasted_iota(jnp.int32, (c, LANES), 1)
    col = lane & (c - 1)
    lo = lane < RWKV_HEAD
    strict = (col > row) if rev else (col < row)
    incl = (col >= row) if rev else (col <= row)
    eye = jnp.where(col == row, 1.0, 0.0)
    sub = (row >> _log2(SUB_BLOCK)) == (col >> _log2(SUB_BLOCK))
    ii = lax.broadcasted_iota(jnp.int32, (n2, n2), 0)
    jj = lax.broadcasted_iota(jnp.int32, (n2, n2), 1)
    same_head = (ii >> _log2(c)) == (jj >> _log2(c))
    return cum, lo, strict, incl, eye, sub, same_head


def _rwkv_chunk(r, k, v, kk, a, lw, k_a, s, rev, masks):
    c = CHUNK
    n2 = 2 * c
    cum, lo, strict, incl, eye, sub, same_head = masks
    bc = _exact_rowmix(cum, lw)
    yield
    bex = bc - lw
    btot = bc[0:1, :] if rev else bc[c - 1:c, :]
    kd = k * (1.0 + (a - 1.0) * k_a)
    bv = kk * a
    e_neg = jnp.exp(-bc)
    e_tot = jnp.exp(btot - bc)
    stack = lambda x: _stack_heads(x.astype(BF16), lo)
    at = (-kk * jnp.exp(bex)).astype(BF16)
    rt_f = r * jnp.exp(bc)
    kc = (kd * e_tot).astype(BF16)
    bcc = (bv * e_tot).astype(BF16)
    s_v = stack(v)

    gram = _dot_nt(jnp.concatenate([at, rt_f.astype(BF16)], axis=0),
                   jnp.concatenate([stack(bv * e_neg), stack(kd * e_neg)], axis=0))
    yield
    nmat = jnp.where(strict, gram[:c, :n2], 0.0)
    a_ka = jnp.where(strict, gram[:c, n2:], 0.0)
    a_rb = jnp.where(incl, gram[c:, :n2], 0.0)
    a_rk = jnp.where(incl, gram[c:, n2:], 0.0)
    n_d = jnp.where(sub, nmat, 0.0)
    t_d = eye + n_d
    pw = _dot(n_d, stack(n_d))
    yield
    for _ in range(_log2(SUB_BLOCK) - 2):
        both = _dot(jnp.concatenate([t_d, pw], axis=0), stack(pw))
        yield
        t_d = t_d + both[:c]
        pw = both[c:]
    t_d = t_d + _dot(t_d, stack(pw))
    xv = _dot(jnp.concatenate([a_ka, a_rk], axis=0), s_v)
    yield
    x1, q_v = xv[:c], xv[c:]
    xm = _dot(t_d, stack(jnp.where(sub, 0.0, nmat)))
    yield
    xm2 = _dot(xm, stack(xm))
    yield
    ym = eye + xm + xm2 + _dot(xm, stack(xm2))
    yield
    tm_ = _dot(ym, stack(t_d))
    yield
    aw = _dot(tm_, jnp.concatenate([stack(at), stack(x1)], axis=1))
    yield
    ah, w = aw[:, :n2], aw[:, n2:]
    wa = _dot(a_rb, jnp.concatenate([stack(w), stack(ah)], axis=1))
    q = q_v + wa[:, :n2]
    rr = rt_f + wa[:, n2:]
    mm = jnp.where(same_head, _dot_tn(ah, bcc), 0.0)
    nn = jnp.where(same_head, _dot_tn(jnp.concatenate([v, w], axis=0),
                                      jnp.concatenate([kc, bcc], axis=0)), 0.0)
    yield
    o = _dot_nt(rr, s) + q
    s_new = s * jnp.exp(btot) + _dot(s, mm) + nn
    return o, s_new


def _rwkv_scan_kernel(*refs, has_init, emit_final, nchunk, npairs, aliased):
    it = iter(refs)
    fwd = [next(it) for _ in range(6)]
    bwd = [next(it) for _ in range(6)]
    ka_ref = next(it)
    s0_ref = next(it) if has_init else None
    if aliased:
        next(it), next(it)
    of_ref = next(it)
    ob_ref = next(it)
    sfin_ref = next(it) if emit_final else None
    s_scr = next(it)
    ci = pl.program_id(2)

    @pl.when(ci == 0)
    def _():
        if has_init:
            s_scr[...] = s0_ref[...]
        else:
            s_scr[...] = jnp.zeros_like(s_scr)

    chains, sinks = [], []
    for d, (ins, o_ref) in enumerate(((fwd, of_ref), (bwd, ob_ref))):
        masks = _rwkv_masks(rev=(d == 1))
        for hp in range(npairs):
            sl = slice(hp * LANES, (hp + 1) * LANES)
            r, k, v, kk, a, lw = (x[:, sl] for x in ins)
            chains.append(_rwkv_chunk(r, k, v, kk, a, lw, ka_ref[:, sl], s_scr[d, hp],
                                      rev=(d == 1), masks=masks))
            sinks.append((o_ref, sl, d, hp))
    for (o, s_new), (o_ref, sl, d, hp) in zip(_lockstep(chains), sinks):
        o_ref[:, sl] = o
        s_scr[d, hp] = s_new

    if emit_final:
        @pl.when(ci == nchunk - 1)
        def _():
            sfin_ref[...] = s_scr[...]


def _rwkv_scan(pre, k_a, s0_bd, *, row_off, nb, t, emit_final, npairs, into=None):
    m, dr = pre["r"].shape
    ngroup = dr // (LANES * npairs)
    nchunk = t // CHUNK
    rb = row_off // CHUNK
    has_init = s0_bd is not None
    fmap = lambda b, h, c: (rb + b * nchunk + c, h)
    bmap = lambda b, h, c: (rb + b * nchunk + (nchunk - 1 - c), h)
    blk = (CHUNK, LANES * npairs)
    in_specs = ([pl.BlockSpec(blk, fmap)] * 6 + [pl.BlockSpec(blk, bmap)] * 6
                + [pl.BlockSpec((1, LANES * npairs), lambda b, h, c: (0, h))])
    args = [pre["r"], pre["k"], pre["v"], pre["kk"], pre["a_f"], pre["lw_f"],
            pre["r"], pre["k"], pre["v"], pre["kk"], pre["a_b"], pre["lw_b"], k_a]
    sblk = pl.BlockSpec((None, 2, npairs, LANES, LANES), lambda b, h, c: (b, 0, h, 0, 0))
    if has_init:
        in_specs.append(sblk)
        args.append(s0_bd)
    aliases = {}
    if into is not None:
        aliases = {len(args): 0, len(args) + 1: 1}
        in_specs += [pl.BlockSpec(memory_space=pl.ANY)] * 2
        args += list(into)
    out_shape = [jax.ShapeDtypeStruct((m, dr), F32)] * 2
    out_specs = [pl.BlockSpec(blk, fmap), pl.BlockSpec(blk, bmap)]
    if emit_final:
        out_shape.append(jax.ShapeDtypeStruct((nb, 2, dr // LANES, LANES, LANES), F32))
        out_specs.append(sblk)
    kern = functools.partial(_rwkv_scan_kernel, has_init=has_init, emit_final=emit_final,
                             nchunk=nchunk, npairs=npairs, aliased=into is not None)
    return pl.pallas_call(
        kern, out_shape=out_shape, grid=(nb, ngroup, nchunk), in_specs=in_specs,
        out_specs=out_specs, scratch_shapes=[pltpu.VMEM((2, npairs, LANES, LANES), F32)],
        input_output_aliases=aliases,
        compiler_params=_cparams("arbitrary", "arbitrary", "arbitrary"), name="rwkv_scan",
    )(*args)


def _rwkv_post_kernel(of_ref, ob_ref, bonus_ref, g_ref, lnw_ref, lnb_ref, y_ref, *, dr):
    bd = _head_ones(LANES, RWKV_HEAD)
    inv = 1.0 / RWKV_HEAD
    for j in range(dr // LANES):
        sl = slice(j * LANES, (j + 1) * LANES)
        o = of_ref[:, sl] + ob_ref[:, sl]
        mean = _group_sum(o, bd) * inv
        oc = o - mean
        var = _group_sum(oc * oc, bd) * inv
        on = oc * lax.rsqrt(var + GN_EPS) * lnw_ref[:, sl] + lnb_ref[:, sl]
        y_ref[:, sl] = ((on + bonus_ref[:, sl]) * g_ref[:, sl]).astype(y_ref.dtype)


def _rwkv_post(o_f, o_b, pre, p, *, tm):
    rows, dr = o_f.shape
    own = pl.BlockSpec((tm, dr), lambda i: (i, 0))
    par = pl.BlockSpec((1, dr), lambda i: (0, 0))
    return pl.pallas_call(
        functools.partial(_rwkv_post_kernel, dr=dr),
        out_shape=jax.ShapeDtypeStruct((rows, dr), BF16),
        grid=(rows // tm,),
        in_specs=[own] * 4 + [par] * 2,
        out_specs=own,
        compiler_params=_cparams("arbitrary"), name="rwkv_post",
    )(o_f, o_b, pre["bonus"], pre["g"], p["ln_w"], p["ln_b"])


HGRN_LEVELS = 6


SUBLANES = 8


def _boundary_rows(b, lvl, rev, sub):
    c = CHUNK
    half = 1 << lvl
    blk = 2 * half
    src = (lambda base: base + half) if rev else (lambda base: base + half - 1)

    def rows(starts, nrows):
        return jnp.concatenate(
            [jnp.broadcast_to(b[src(s):src(s) + 1, :], (nrows, LANES)) for s in starts], axis=0)

    if blk >= SUBLANES:
        return rows(range(0, c, blk), blk)
    if blk == 2:
        if rev:
            return jnp.where((sub & 1) == 1, b, pltpu.roll(b, c - 1, 0))
        return jnp.where((sub & 1) == 0, b, pltpu.roll(b, 1, 0))
    out = None
    for k in range(SUBLANES // blk):
        arr = rows(range(k * blk, c, SUBLANES), SUBLANES)
        out = arr if out is None else jnp.where(sub >= k * blk, arr, out)
    return out


def _hgrn_masks(rev):
    c = CHUNK
    ti = lax.broadcasted_iota(jnp.int32, (c, c), 0)
    tj = lax.broadcasted_iota(jnp.int32, (c, c), 1)
    late, early = (tj, ti) if rev else (ti, tj)
    sels = [((late >> lvl) == (early >> lvl) + 1) & (((early >> lvl) & 1) == 0)
            for lvl in range(HGRN_LEVELS)]
    cum = jnp.where((tj >= ti) if rev else (tj <= ti), 1.0, 0.0).astype(BF16)
    sub = lax.broadcasted_iota(jnp.int32, (c, LANES), 0) & (SUBLANES - 1)
    return jnp.concatenate([cum, cum, cum], axis=1), ti == tj, sels, sub


def _hgrn_chunk(q_pre, f_pre, v, lb, st, rev, masks):
    c = CHUNK
    cum3, diag, sels, sub = masks
    q = q_pre * jax.nn.sigmoid(q_pre)
    f = lb + (1.0 - lb) * jax.nn.sigmoid(f_pre)
    kf = 1.0 - f
    b = _exact_rowmix(cum3, jnp.log(f))
    yield
    btot = b[0:1, :] if rev else b[c - 1:c, :]
    q_b, kf_b = q.astype(BF16), kf.astype(BF16)
    attn = jnp.where(diag, _dot_nt(q_b, kf_b), 0.0)
    for lvl in range(HGRN_LEVELS):
        dq = b - _boundary_rows(b, lvl, rev, sub)
        wgt = jnp.exp(-jnp.abs(dq)).astype(BF16)
        attn = jnp.where(sels[lvl], _dot_nt(q_b * wgt, kf_b * wgt), attn)
    yield
    o = _dot_nt(q * jnp.exp(b), st) + _dot(attn, v)
    st_new = st * jnp.exp(btot) + _dot_tn(v, kf * jnp.exp(btot - b))
    return o, st_new


def _hgrn_scan_kernel(*refs, has_init, emit_final, nsteps, nheads, cols, aliased):
    it = iter(refs)
    qf_ref, ff_ref, vf_ref = next(it), next(it), next(it)
    qb_ref, fb_ref, vb_ref = next(it), next(it), next(it)
    lb_ref = next(it)
    s0_ref = next(it) if has_init else None
    if aliased:
        next(it), next(it)
    of_ref, ob_ref = next(it), next(it)
    sfin_ref = next(it) if emit_final else None
    s_scr = next(it)
    in_scr, out_scr = (next(it), next(it)) if cols is not None else (None, None)
    ci = pl.program_id(2)

    @pl.when(ci == 0)
    def _():
        if has_init:
            for d in range(2):
                for h in range(nheads):
                    s_scr[d, h] = s0_ref[d, h].T
        else:
            s_scr[...] = jnp.zeros_like(s_scr)

    ins = ((qf_ref, ff_ref, vf_ref), (qb_ref, fb_ref, vb_ref))
    outs = (of_ref, ob_ref)
    all_masks = [_hgrn_masks(rev=False), _hgrn_masks(rev=True)]

    def advance(load, store):
        chains, sinks = [], []
        for d in range(2):
            for h in range(nheads):
                sl = slice(h * LANES, (h + 1) * LANES)
                qp, fp, v = (load(d, k, sl) for k in range(3))
                chains.append(_hgrn_chunk(qp, fp, v, lb_ref[:, sl], s_scr[d, h], rev=(d == 1),
                                          masks=all_masks[d]))
                sinks.append((d, h, sl))
        for (o, s_new), (d, h, sl) in zip(_lockstep(chains), sinks):
            store(d, sl, o)
            s_scr[d, h] = s_new

    if cols is None:
        def store_rows(d, sl, o):
            outs[d][:, sl] = o
        advance(lambda d, k, sl: ins[d][k][:, sl], store_rows)
    else:
        for d in range(2):
            for k in range(3):
                in_scr[d, k] = jnp.swapaxes(ins[d][k][...], 0, 1)

        def body(j, carry):
            col = (j, cols - 1 - j)

            def store_col(d, sl, o):
                out_scr[d, col[d], :, sl] = o
            advance(lambda d, k, sl: in_scr[d, k, col[d], :, sl], store_col)
            return carry
        lax.fori_loop(0, cols, body, 0)
        for d in range(2):
            outs[d][...] = jnp.swapaxes(out_scr[d], 0, 1)

    if emit_final:
        @pl.when(ci == nsteps - 1)
        def _():
            for d in range(2):
                for h in range(nheads):
                    sfin_ref[d, h] = s_scr[d, h].T


HGRN_COLS = 8


def _hgrn_scan(zhs, lb, s0, *, row_off, nb, t, dg, emit_final, nheads, column_major,
               into=None):
    m = zhs.shape[0]
    ng = dg // (LANES * nheads)
    has_init = s0 is not None
    lanes = LANES * nheads
    if column_major:
        nrow = t // GRID_W
        assert nrow == CHUNK and m % t == 0 and row_off % t == 0
        nsteps = GRID_W // HGRN_COLS
        seq0 = row_off // t
        zsrc = zhs.reshape(m // t, nrow, GRID_W, 4 * dg)
        blk = (None, nrow, HGRN_COLS, lanes)
        fwd = lambda g: (lambda b, h, c: (b + seq0, 0, c, g * ng + h))
        bwd = lambda g: (lambda b, h, c: (b + seq0, 0, nsteps - 1 - c, g * ng + h))
        out_dims = (m // t, nrow, GRID_W, dg)
        ofwd = lambda b, h, c: (b + seq0, 0, c, h)
        obwd = lambda b, h, c: (b + seq0, 0, nsteps - 1 - c, h)
        cols = HGRN_COLS
    else:
        nsteps = t // CHUNK
        rb = row_off // CHUNK
        zsrc = zhs
        blk = (CHUNK, lanes)
        fwd = lambda g: (lambda b, h, c: (rb + b * nsteps + c, g * ng + h))
        bwd = lambda g: (lambda b, h, c: (rb + b * nsteps + nsteps - 1 - c, g * ng + h))
        out_dims = (m, dg)
        ofwd = lambda b, h, c: (rb + b * nsteps + c, h)
        obwd = lambda b, h, c: (rb + b * nsteps + nsteps - 1 - c, h)
        cols = None

    in_specs = [pl.BlockSpec(blk, fwd(0)), pl.BlockSpec(blk, fwd(1)), pl.BlockSpec(blk, fwd(3)),
                pl.BlockSpec(blk, bwd(0)), pl.BlockSpec(blk, bwd(2)), pl.BlockSpec(blk, bwd(3)),
                pl.BlockSpec((1, lanes), lambda b, h, c: (0, h))]
    args = [zsrc] * 6 + [lb]
    sblk = pl.BlockSpec((None, 2, nheads, LANES, LANES), lambda b, h, c: (b, 0, h, 0, 0))
    if has_init:
        in_specs.append(sblk)
        args.append(s0)
    aliases = {}
    if into is not None:
        aliases = {len(args): 0, len(args) + 1: 1}
        in_specs += [pl.BlockSpec(memory_space=pl.ANY)] * 2
        args += [o.reshape(out_dims) for o in into]
    out_shape = [jax.ShapeDtypeStruct(out_dims, F32)] * 2
    out_specs = [pl.BlockSpec(blk, ofwd), pl.BlockSpec(blk, obwd)]
    if emit_final:
        out_shape.append(jax.ShapeDtypeStruct((nb, 2, dg // LANES, LANES, LANES), F32))
        out_specs.append(sblk)
    kern = functools.partial(_hgrn_scan_kernel, has_init=has_init, emit_final=emit_final,
                             nsteps=nsteps, nheads=nheads, cols=cols, aliased=into is not None)
    scratch = [pltpu.VMEM((2, nheads, LANES, LANES), F32)]
    if column_major:
        scratch += [pltpu.VMEM((2, 3, cols, CHUNK, lanes), F32),
                    pltpu.VMEM((2, cols, CHUNK, lanes), F32)]
    outs = pl.pallas_call(
        kern, out_shape=out_shape, grid=(nb, ng, nsteps), in_specs=in_specs, out_specs=out_specs,
        scratch_shapes=scratch, input_output_aliases=aliases,
        compiler_params=_cparams("arbitrary", "arbitrary", "arbitrary"), name="hgrn_scan",
    )(*args)
    return [outs[0].reshape(m, dg), outs[1].reshape(m, dg)] + list(outs[2:])


def _hgrn_post_kernel(of_ref, ob_ref, gate_ref, gain_ref, y_ref, *, dg):
    gain = gain_ref[...]
    for j in range(dg // LANES):
        sl = slice(j * LANES, (j + 1) * LANES)
        o = of_ref[:, sl] + ob_ref[:, sl]
        on = o * lax.rsqrt(jnp.mean(o * o, axis=-1, keepdims=True) + RMS_EPS) * gain
        gt = gate_ref[:, sl]
        y_ref[:, sl] = (on * (gt * jax.nn.sigmoid(gt))).astype(y_ref.dtype)


def _hgrn_post(o_f, o_b, z_gates, gain, *, gate_col, tm):
    rows_all, dg = o_f.shape
    kern = functools.partial(_hgrn_post_kernel, dg=dg)
    gcb = gate_col // dg
    par = pl.BlockSpec((1, LANES), lambda *_: (0, 0))
    ospec = pl.BlockSpec((tm, dg), lambda i: (i, 0))
    return pl.pallas_call(
        kern, out_shape=jax.ShapeDtypeStruct((rows_all, dg), BF16), grid=(rows_all // tm,),
        in_specs=[ospec, ospec, pl.BlockSpec((tm, dg), lambda i: (i, gcb)), par],
        out_specs=ospec,
        compiler_params=_cparams("arbitrary"), name="hgrn_post",
    )(o_f, o_b, z_gates, gain)


def _largest_divisor(n, candidates):
    for cand in candidates:
        if n % cand == 0:
            return cand
    raise ValueError(f"no tile in {candidates} divides {n}")


def _pad_to(x, axis, size):
    pad = size - x.shape[axis]
    if pad == 0:
        return x
    widths = [(0, 0)] * x.ndim
    widths[axis] = (0, pad)
    return jnp.pad(x, widths)


def _pair_state_to_bd(s):
    b, two, h, n, _ = s.shape
    s = s.reshape(b, two, h // 2, 2, n, n)
    z = jnp.zeros_like(s[:, :, :, 0])
    top = jnp.concatenate([s[:, :, :, 0], z], axis=-1)
    bot = jnp.concatenate([z, s[:, :, :, 1]], axis=-1)
    return jnp.concatenate([top, bot], axis=-2)


def _bd_to_pair_state(sbd):
    b, two, hp, n2, _ = sbd.shape
    n = n2 // 2
    s0 = sbd[:, :, :, :n, :n]
    s1 = sbd[:, :, :, n:, n:]
    return jnp.stack([s0, s1], axis=3).reshape(b, two, hp * 2, n, n)


def kernel(x_prompt, x_sample, state_rwkv, state_hgrn, c, c_ctx, w_mod, b_mod, norm_ffn1, norm_mix, norm_ffn2, ffn1_w_in, ffn1_w_out, ffn2_w_in, ffn2_w_out, w_in, rwkv_mu_prev, rwkv_mu_next, rwkv_w0, rwkv_w2, rwkv_a0, rwkv_a2, rwkv_g2, rwkv_k_k, rwkv_k_a, rwkv_r_k, rwkv_ln_w, rwkv_ln_b, hgrn_lb_logits, hgrn_norm, w_branch_rwkv, w_branch_hgrn, w_out, norm_final):
    nbc, tc, d = x_prompt.shape
    nbl, tl, _ = x_sample.shape
    depth = w_mod.shape[0]
    mc, ml = nbc * tc, nbl * tl
    m = mc + ml
    dims = (mc, tc, tl, nbl)
    dr = rwkv_w0.shape[-1]
    dg = w_branch_hgrn.shape[1]
    dff = ffn1_w_out.shape[1]
    wl, al, gl = rwkv_w2.shape[2], rwkv_a2.shape[2], rwkv_g2.shape[1]
    glp = -(-gl // LANES) * LANES
    rw_cols = 3 * dr + 2 * wl + 2 * al + gl
    zr_cols = 3 * dr + 2 * wl + 2 * al + glp
    assert nbl < MOD_ROWS and tl % GRID_W == 0 and tl // GRID_W == CHUNK and tc % CHUNK == 0

    tm_norm = min(256, tc)
    tm_proj = min(1024, mc)
    tm_merge = min(512, mc)
    tm_post = min(512, mc)
    tn_of = lambda n: _largest_divisor(n, (1024, 512, 256, 128))
    tf = _largest_divisor(dff, (256, 128))

    lb_all = jnp.cumsum(jax.nn.softmax(hgrn_lb_logits.astype(F32), axis=0), axis=0)
    cond = jnp.concatenate([c, c_ctx[None, :], jnp.zeros((MOD_ROWS - nbl - 1, d), F32)], axis=0)

    x = (x_prompt.reshape(mc, d), x_sample.reshape(ml, d))
    new_r, new_h = [], []
    for l in range(depth):
        mod3 = _modulation(cond, w_mod[l], b_mod[l], tn_of(N_MOD * d)).reshape(MOD_ROWS, N_MOD, d)

        h = _resnorm(x, None, mod3, norm_ffn1[l], dims, res_scale=None, gate_idx=None, mod_idx=0,
                     emit_x=False, out_dtype=BF16, tm=tm_norm)[0]
        delta = _ffn(h, ffn1_w_in[l].astype(BF16), ffn1_w_out[l].astype(BF16), tm=tm_proj, tf=tf)

        x, h = _resnorm(x, delta, mod3, norm_mix[l], dims, res_scale=FFN_HALF, gate_idx=2,
                        mod_idx=3, emit_x=True, out_dtype=BF16, tm=tm_norm)
        wi = w_in[l]
        hg0 = rw_cols
        w_r = _pad_to(wi[:, :rw_cols], 1, zr_cols).astype(BF16)
        w_hs = wi[:, hg0:hg0 + 4 * dg].astype(BF16)
        w_gates = wi[:, hg0 + 4 * dg:].astype(BF16)
        col_gate, col_gr, col_gh = 0, dg, dg + d
        z_r = _matmul(h, w_r, out_dtype=F32, tm=tm_proj, tn=tn_of(zr_cols), name="proj_rwkv")
        z_g = _matmul(h, w_gates, out_dtype=F32, tm=tm_proj, tn=tn_of(dg), name="proj_gates")
        zhs = _matmul(h, w_hs, out_dtype=F32, tm=tm_proj, tn=tn_of(dg), name="proj_hgrn")

        p = {
            "mu_prev": _pad_to(rwkv_mu_prev[l][None, :], 1, zr_cols),
            "mu_next": _pad_to(rwkv_mu_next[l][None, :], 1, zr_cols),
            "w0": rwkv_w0[l], "w2": rwkv_w2[l].astype(BF16),
            "a0": rwkv_a0[l], "a2": rwkv_a2[l].astype(BF16),
            "g2": _pad_to(rwkv_g2[l], 0, glp).astype(BF16),
            "k_k": rwkv_k_k[l][None, :], "k_a": rwkv_k_a[l][None, :],
            "r_k": rwkv_r_k[l].reshape(1, dr),
            "ln_w": rwkv_ln_w[l][None, :], "ln_b": rwkv_ln_b[l][None, :],
        }
        names = ("r", "k", "v", "kk", "a_f", "a_b", "lw_f", "lw_b", "g", "bonus")
        pre = dict(zip(names, _rwkv_prep(z_r, p, dims, tm=min(128, tc), zr_cols=zr_cols, dr=dr,
                                         wl=wl, al=al, gl=glp)))
        npairs = _largest_divisor(dr // LANES, (16, 8, 4, 2, 1))
        of_c, ob_c, sr_c = _rwkv_scan(pre, p["k_a"], None, row_off=0, nb=nbc, t=tc,
                                      emit_final=True, npairs=npairs)
        o_f, o_b = _rwkv_scan(pre, p["k_a"], _pair_state_to_bd(state_rwkv[:, l].astype(F32)),
                              row_off=mc, nb=nbl, t=tl, emit_final=False, npairs=npairs,
                              into=(of_c, ob_c))
        y_r = _rwkv_post(o_f, o_b, pre, p, tm=tm_post)
        new_r.append(_bd_to_pair_state(sr_c))

        lb = lb_all[l][None, :]
        gain_h = hgrn_norm[l][None, :]
        nheads = _largest_divisor(dg // LANES, (8, 4, 2, 1))
        nheads_cols = _largest_divisor(dg // LANES, (4, 2, 1))
        hf_c, hb_c, sh_c = _hgrn_scan(zhs, lb, None, row_off=0, nb=nbc, t=tc, dg=dg,
                                      emit_final=True, nheads=nheads, column_major=False)
        h_f, h_b = _hgrn_scan(zhs, lb, state_hgrn[:, l].astype(F32), row_off=mc, nb=nbl, t=tl,
                              dg=dg, emit_final=False, nheads=nheads_cols, column_major=True,
                              into=(hf_c, hb_c))
        y_h = _hgrn_post(h_f, h_b, z_g, gain_h, gate_col=col_gate, tm=tm_post)
        new_h.append(sh_c)

        merged = _merge(y_r, y_h, w_branch_rwkv[l].astype(BF16), w_branch_hgrn[l].astype(BF16),
                        z_g, col_gr, col_gh, tm=tm_merge, tn=tn_of(dg))
        delta = _matmul(merged, w_out[l].astype(BF16), out_dtype=F32, tm=tm_proj, tn=tn_of(d),
                        name="mix_out")

        x, h = _resnorm(x, delta, mod3, norm_ffn2[l], dims, res_scale=1.0, gate_idx=5, mod_idx=6,
                        emit_x=True, out_dtype=BF16, tm=tm_norm)
        delta = _ffn(h, ffn2_w_in[l].astype(BF16), ffn2_w_out[l].astype(BF16), tm=tm_proj, tf=tf)
        last = l == depth - 1
        if not last:
            x = _resnorm(x, delta, mod3, norm_ffn2[l], dims, res_scale=FFN_HALF, gate_idx=8,
                         mod_idx=None, emit_x=True, out_dtype=F32, tm=tm_norm)[0]

    y_c = _resnorm(x, delta, mod3, norm_final, dims, res_scale=FFN_HALF, gate_idx=8, mod_idx=None,
                   emit_x=False, out_dtype=F32, tm=tm_norm, row_off=0, rows=mc)[0]
    y_l = _resnorm(x, delta, mod3, norm_final, dims, res_scale=FFN_HALF, gate_idx=8, mod_idx=None,
                   emit_x=False, out_dtype=F32, tm=tm_norm, row_off=mc, rows=ml)[0]
    return (y_c.reshape(nbc, tc, d), y_l.reshape(nbl, tl, d),
            jnp.stack(new_r, axis=1), jnp.stack(new_h, axis=1))
```
